```python
import math
import jax, jax.numpy as jnp
from jax import lax
import numpy as np

D_MODEL = 2048
BATCH = 4
SEQ = 2048
DEPTH = 1
DEC_BATCH = 128
DEC_SEQ = 8
PAST_LEN = 2048
PAGE_SIZE = 128

MIX_WIDTH = D_MODEL
ATTN_WIDTH = MIX_WIDTH // 2
CONV_CH = MIX_WIDTH - ATTN_WIDTH
HEAD_DIM = 128
N_HEADS = ATTN_WIDTH // HEAD_DIM
MOBA_BLOCK = 256
MOBA_TOPK = 3
MOBA_QCHUNK = 32
NUM_BUCKETS = 32
MAX_EXACT = NUM_BUCKETS // 2
REL_MAX_DIST = 128
CONV_K = 31
N_MEM = 256
X_HEADS = 4
X_HEAD_DIM = 128
X_WIDTH = X_HEADS * X_HEAD_DIM
PEER_HEADS = 8
PEER_NKEYS = 128
PEER_EXPERTS = PEER_NKEYS * PEER_NKEYS
PEER_TOPK = 16
PEER_QDIM = 256
PEER_HALF = PEER_QDIM // 2
PEER_TOKEN_BLOCK = 64
EPS = 1e-6

kernel_name = "hymba_moba_conformer_peer_step"


def rmsnorm(x, g):
    xf = x.astype(jnp.float32)
    y = xf * lax.rsqrt(jnp.mean(xf * xf, axis=-1, keepdims=True) + EPS)
    return (y * g.astype(jnp.float32)).astype(x.dtype)


def layernorm(x, g, b):
    xf = x.astype(jnp.float32)
    mu = jnp.mean(xf, axis=-1, keepdims=True)
    xc = xf - mu
    var = jnp.mean(xc * xc, axis=-1, keepdims=True)
    y = xc * lax.rsqrt(var + EPS) * g.astype(jnp.float32) + b.astype(jnp.float32)
    return y.astype(x.dtype)


def rel_bucket(rel):
    n = jnp.maximum(rel, 0)
    nf = jnp.maximum(n, 1).astype(jnp.float32)
    large = MAX_EXACT + (jnp.log(nf / MAX_EXACT) / math.log(REL_MAX_DIST / MAX_EXACT)
                         * (NUM_BUCKETS - MAX_EXACT)).astype(jnp.int32)
    large = jnp.minimum(large, NUM_BUCKETS - 1)
    return jnp.where(n < MAX_EXACT, n, large)


def moba_attend(q, k, v, q_pos, rel_bias):
    L = k.shape[0]
    nb = -(-L // MOBA_BLOCK)
    pad = nb * MOBA_BLOCK - L
    kb = jnp.pad(k, ((0, pad), (0, 0), (0, 0))).reshape(nb, MOBA_BLOCK, N_HEADS, HEAD_DIM).transpose(2, 0, 1, 3)
    vb = jnp.pad(v, ((0, pad), (0, 0), (0, 0))).reshape(nb, MOBA_BLOCK, N_HEADS, HEAD_DIM).transpose(2, 0, 1, 3)
    k_mean = jnp.mean(kb.astype(jnp.float32), axis=2)
    n_sel = min(MOBA_TOPK, nb)
    tq = q.shape[0]
    qc = min(MOBA_QCHUNK, tq)
    n_chunks = -(-tq // qc)
    padq = n_chunks * qc - tq
    q_c = jnp.pad(q, ((0, padq), (0, 0), (0, 0))).reshape(n_chunks, qc, N_HEADS, HEAD_DIM)
    p_c = jnp.pad(q_pos, (0, padq), mode='edge').reshape(n_chunks, qc)
    h_idx = jnp.arange(N_HEADS)
    offs = jnp.arange(MOBA_BLOCK)
    scale = HEAD_DIM ** -0.5

    def chunk(args):
        qb, pb = args
        own = pb // MOBA_BLOCK
        gate = jnp.einsum('qhd,hnd->qhn', qb.astype(jnp.float32), k_mean)
        is_past = jnp.arange(nb)[None, None, :] < own[:, None, None]
        gate = jnp.where(is_past, gate, -jnp.inf)
        top_s, top_i = lax.top_k(gate, n_sel)
        blk = jnp.concatenate([top_i.astype(jnp.int32),
                               jnp.broadcast_to(own[:, None, None], (qc, N_HEADS, 1)).astype(jnp.int32)], axis=-1)
        ok = jnp.concatenate([jnp.isfinite(top_s), jnp.ones((qc, N_HEADS, 1), bool)], axis=-1)
        kg = kb[h_idx[None, :, None], blk]
        vg = vb[h_idx[None, :, None], blk]
        rel = pb[:, None, None, None] - (blk[..., None] * MOBA_BLOCK + offs)
        visible = ok[..., None] & (rel >= 0)
        bias = rel_bias[rel_bucket(rel), h_idx[None, :, None, None]].astype(jnp.float32)
        logits = jnp.einsum('qhd,qhnkd->qhnk', qb, kg).astype(jnp.float32) * scale + bias
        logits = jnp.where(visible, logits, -jnp.inf)
        p = jax.nn.softmax(logits.reshape(qc, N_HEADS, -1), axis=-1).reshape(logits.shape).astype(v.dtype)
        return jnp.einsum('qhnk,qhnkd->qhd', p, vg)

    out = lax.map(chunk, (q_c, p_c))
    return out.reshape(n_chunks * qc, N_HEADS, HEAD_DIM)[:tq]


def conformer_conv(a, g, buf, w_dw, b_dw, ln_g, ln_b):
    u = a * jax.nn.sigmoid(g)
    ext = jnp.concatenate([buf.astype(u.dtype), u], axis=1)
    y = lax.conv_general_dilated(ext, w_dw[:, None, :].astype(ext.dtype), window_strides=(1,), padding='VALID',
                                 dimension_numbers=('NWC', 'WIO', 'NWC'), feature_group_count=CONV_CH)
    y = layernorm(y + b_dw, ln_g, ln_b)
    return jax.nn.silu(y), ext[:, -(CONV_K - 1):]


def hybrid_mixers(x, conv_buf, attn_fn, norm_g, w_in, conv_w, conv_b, ln_g, ln_b, w_out):
    n, t, _ = x.shape
    z = rmsnorm(x, norm_g) @ w_in
    q, k, v, ga, gg = jnp.split(z, [ATTN_WIDTH, 2 * ATTN_WIDTH, 3 * ATTN_WIDTH, 3 * ATTN_WIDTH + CONV_CH], axis=-1)
    q = q.reshape(n, t, N_HEADS, HEAD_DIM)
    k = k.reshape(n, t, N_HEADS, HEAD_DIM)
    v = v.reshape(n, t, N_HEADS, HEAD_DIM)
    attn_o = attn_fn(q, k, v).reshape(n, t, ATTN_WIDTH)
    conv_o, new_buf = conformer_conv(ga, gg, conv_buf, conv_w, conv_b, ln_g, ln_b)
    y = jnp.concatenate([attn_o, conv_o], axis=-1) @ w_out
    return x + y, k, v, new_buf


def memory_kv(mem, g, w_mk, w_mv):
    n, m, _ = mem.shape
    mn = rmsnorm(mem, g)
    return ((mn @ w_mk).reshape(n, m, X_HEADS, X_HEAD_DIM), (mn @ w_mv).reshape(n, m, X_HEADS, X_HEAD_DIM))


def cross_attend(hn, mk, mv, w_cq, w_co):
    n, t, _ = hn.shape
    q = (hn @ w_cq).reshape(n, t, X_HEADS, X_HEAD_DIM)
    s = jnp.einsum('nthd,nmhd->nhtm', q, mk).astype(jnp.float32) * X_HEAD_DIM ** -0.5
    p = jax.nn.softmax(s, axis=-1).astype(mv.dtype)
    o = jnp.einsum('nhtm,nmhd->nthd', p, mv).reshape(n, t, X_WIDTH)
    return o @ w_co


def peer_ffn(h, w_pq, sub_keys, peer_u, peer_v):
    n, t, d = h.shape
    tt = n * t
    x = h.reshape(tt, d)
    q = (x @ w_pq).reshape(tt, PEER_HEADS, 2, PEER_HALF)
    s = jnp.einsum('thcd,hckd->thck', q, sub_keys).astype(jnp.float32)
    s1, i1 = lax.top_k(s[:, :, 0], PEER_TOPK)
    s2, i2 = lax.top_k(s[:, :, 1], PEER_TOPK)
    cand_s = (s1[..., :, None] + s2[..., None, :]).reshape(tt, PEER_HEADS, PEER_TOPK * PEER_TOPK)
    cand_i = (i1[..., :, None] * PEER_NKEYS + i2[..., None, :]).reshape(tt, PEER_HEADS, PEER_TOPK * PEER_TOPK)
    top_s, top_j = lax.top_k(cand_s, PEER_TOPK)
    e_idx = jnp.take_along_axis(cand_i, top_j, axis=-1).reshape(tt, PEER_HEADS * PEER_TOPK)
    gates = jax.nn.softmax(top_s, axis=-1).reshape(tt, PEER_HEADS * PEER_TOPK).astype(h.dtype)
    nblk = -(-tt // PEER_TOKEN_BLOCK)
    pad = nblk * PEER_TOKEN_BLOCK - tt
    xb = jnp.pad(x, ((0, pad), (0, 0))).reshape(nblk, PEER_TOKEN_BLOCK, d)
    ib = jnp.pad(e_idx, ((0, pad), (0, 0))).reshape(nblk, PEER_TOKEN_BLOCK, -1)
    gb = jnp.pad(gates, ((0, pad), (0, 0))).reshape(nblk, PEER_TOKEN_BLOCK, -1)

    def block(args):
        xt, it, gt = args
        act = jax.nn.gelu(jnp.einsum('td,ted->te', xt, peer_u[it])) * gt
        return jnp.einsum('te,ted->td', act, peer_v[it])

    out = lax.map(block, (xb, ib, gb)).reshape(nblk * PEER_TOKEN_BLOCK, d)[:tt]
    return out.reshape(n, t, d)


def memory_and_peer(h, mk, mv, norm2_g, w_cq, w_co, norm3_g, w_pq, sub_keys, peer_u, peer_v):
    h = h + cross_attend(rmsnorm(h, norm2_g), mk, mv, w_cq, w_co)
    return h + peer_ffn(rmsnorm(h, norm3_g), w_pq, sub_keys, peer_u, peer_v)


def setup_inputs(seed: int = 0) -> dict:
    key = jax.random.key(seed)
    kit = iter(list(jax.random.split(key, 40)))
    f32 = jnp.float32

    def nrm(shape, scale):
        return jax.random.normal(next(kit), shape, f32) * scale

    def gain(shape):
        return 1.0 + nrm(shape, 0.05)

    n_pages = PAST_LEN // PAGE_SIZE
    n_used = DEC_BATCH * n_pages
    n_pool = n_used + n_used // 4
    d_in = 3 * ATTN_WIDTH + 2 * CONV_CH
    page_table = jax.random.permutation(next(kit), n_pool)[:n_used].reshape(DEC_BATCH, n_pages).astype(jnp.int32)
    return {
        'x_prompt': nrm((BATCH, SEQ, D_MODEL), 1.0),
        'x_sample': nrm((DEC_BATCH, DEC_SEQ, D_MODEL), 1.0),
        'mem_prompt': nrm((BATCH, N_MEM, D_MODEL), 1.0),
        'cache_k': nrm((DEPTH, n_pool, PAGE_SIZE, N_HEADS, HEAD_DIM), 1.0),
        'cache_v': nrm((DEPTH, n_pool, PAGE_SIZE, N_HEADS, HEAD_DIM), 1.0),
        'cache_conv': nrm((DEPTH, DEC_BATCH, CONV_K - 1, CONV_CH), 0.5),
        'cache_mem_k': nrm((DEPTH, DEC_BATCH, N_MEM, X_HEADS, X_HEAD_DIM), 1.0),
        'cache_mem_v': nrm((DEPTH, DEC_BATCH, N_MEM, X_HEADS, X_HEAD_DIM), 1.0),
        'page_table': page_table,
        'norm1_g': gain((DEPTH, D_MODEL)),
        'w_in': nrm((DEPTH, D_MODEL, d_in), D_MODEL ** -0.5),
        'rel_bias': nrm((NUM_BUCKETS, N_HEADS), 0.5),
        'conv_w': nrm((DEPTH, CONV_K, CONV_CH), CONV_K ** -0.5),
        'conv_b': nrm((DEPTH, CONV_CH), 0.02),
        'conv_ln_g': gain((DEPTH, CONV_CH)),
        'conv_ln_b': nrm((DEPTH, CONV_CH), 0.02),
        'w_out': nrm((DEPTH, MIX_WIDTH, D_MODEL), MIX_WIDTH ** -0.5),
        'norm2_g': gain((DEPTH, D_MODEL)),
        'mem_norm_g': gain((DEPTH, D_MODEL)),
        'w_cq': nrm((DEPTH, D_MODEL, X_WIDTH), D_MODEL ** -0.5),
        'w_mk': nrm((DEPTH, D_MODEL, X_WIDTH), D_MODEL ** -0.5),
        'w_mv': nrm((DEPTH, D_MODEL, X_WIDTH), D_MODEL ** -0.5),
        'w_co': nrm((DEPTH, X_WIDTH, D_MODEL), X_WIDTH ** -0.5),
        'norm3_g': gain((DEPTH, D_MODEL)),
        'w_pq': nrm((DEPTH, D_MODEL, PEER_HEADS * PEER_QDIM), D_MODEL ** -0.5),
        'peer_sub_keys': nrm((DEPTH, PEER_HEADS, 2, PEER_NKEYS, PEER_HALF), PEER_HALF ** -0.5),
        'peer_u': nrm((DEPTH, PEER_EXPERTS, D_MODEL), D_MODEL ** -0.5),
        'peer_v': nrm((DEPTH, PEER_EXPERTS, D_MODEL), PEER_HEADS ** -0.5),
        'final_g': gain((D_MODEL,)),
    }


def reference(x_prompt, x_sample, mem_prompt, cache_k, cache_v, cache_conv, cache_mem_k, cache_mem_v,
              page_table, norm1_g, w_in, rel_bias, conv_w, conv_b, conv_ln_g, conv_ln_b, w_out,
              norm2_g, mem_norm_g, w_cq, w_mk, w_mv, w_co, norm3_g, w_pq, peer_sub_keys, peer_u, peer_v,
              final_g):
    def prompt_attn(q, k, v):
        pos = jnp.arange(q.shape[1], dtype=jnp.int32)
        return lax.map(lambda a: moba_attend(a[0], a[1], a[2], pos, rel_bias), (q, k, v))

    def make_sample_attn(ck, cv):
        def sample_attn(q, k, v):
            def one(args):
                qb, kn, vn, pt = args
                k_past = ck[pt].reshape(-1, N_HEADS, HEAD_DIM)
                v_past = cv[pt].reshape(-1, N_HEADS, HEAD_DIM)
                pos = k_past.shape[0] + jnp.arange(qb.shape[0], dtype=jnp.int32)
                return moba_attend(qb, jnp.concatenate([k_past, kn], axis=0),
                                   jnp.concatenate([v_past, vn], axis=0), pos, rel_bias)
            return lax.map(one, (q, k, v, page_table))
        return sample_attn

    h_p = x_prompt
    h_s = x_sample
    kp_l, vp_l, cp_l, mkp_l, mvp_l, ks_l, vs_l, cs_l = [], [], [], [], [], [], [], []
    for l in range(DEPTH):
        mk_p, mv_p = memory_kv(mem_prompt, mem_norm_g[l], w_mk[l], w_mv[l])
        zero_buf = jnp.zeros((h_p.shape[0], CONV_K - 1, CONV_CH), h_p.dtype)
        h_p, k_p, v_p, buf_p = hybrid_mixers(h_p, zero_buf, prompt_attn, norm1_g[l], w_in[l], conv_w[l], conv_b[l],
                                             conv_ln_g[l], conv_ln_b[l], w_out[l])
        h_p = memory_and_peer(h_p, mk_p, mv_p, norm2_g[l], w_cq[l], w_co[l], norm3_g[l], w_pq[l],
                              peer_sub_keys[l], peer_u[l], peer_v[l])
        h_s, k_s, v_s, buf_s = hybrid_mixers(h_s, cache_conv[l], make_sample_attn(cache_k[l], cache_v[l]),
                                             norm1_g[l], w_in[l], conv_w[l], conv_b[l], conv_ln_g[l],
                                             conv_ln_b[l], w_out[l])
        h_s = memory_and_peer(h_s, cache_mem_k[l], cache_mem_v[l], norm2_g[l], w_cq[l], w_co[l], norm3_g[l],
                              w_pq[l], peer_sub_keys[l], peer_u[l], peer_v[l])
        n_p, t_p = k_p.shape[0], k_p.shape[1]
        kp_l.append(k_p.reshape(n_p, t_p // PAGE_SIZE, PAGE_SIZE, N_HEADS, HEAD_DIM))
        vp_l.append(v_p.reshape(n_p, t_p // PAGE_SIZE, PAGE_SIZE, N_HEADS, HEAD_DIM))
        cp_l.append(buf_p)
        mkp_l.append(mk_p)
        mvp_l.append(mv_p)
        ks_l.append(k_s)
        vs_l.append(v_s)
        cs_l.append(buf_s)
    y_prompt = rmsnorm(h_p, final_g)
    y_sample = rmsnorm(h_s, final_g)
    k_prompt = jnp.stack(kp_l)
    v_prompt = jnp.stack(vp_l)
    conv_prompt = jnp.stack(cp_l)
    mem_k_prompt = jnp.stack(mkp_l)
    mem_v_prompt = jnp.stack(mvp_l)
    k_sample = jnp.stack(ks_l)
    v_sample = jnp.stack(vs_l)
    conv_sample = jnp.stack(cs_l)
    return (y_prompt, y_sample, k_prompt, v_prompt, conv_prompt, mem_k_prompt, mem_v_prompt, k_sample, v_sample, conv_sample)
```

```python
import functools
import math

import numpy as np
import jax
import jax.numpy as jnp
from jax import lax
from jax.experimental import pallas as pl
from jax.experimental.pallas import tpu as pltpu

F32 = jnp.float32
BF16 = jnp.bfloat16

D_MODEL = 2048
HEAD_DIM = 128
N_HEADS = 8
ATTN_WIDTH = N_HEADS * HEAD_DIM
CONV_CH = 1024
MOBA_BLOCK = 256
MOBA_TOPK = 3
PAGE_SIZE = 128
NUM_BUCKETS = 32
MAX_EXACT = 16
REL_MAX_DIST = 128
CONV_K = 31
N_MEM = 256
X_HEADS = 4
X_WIDTH = X_HEADS * HEAD_DIM
PEER_HEADS = 8
PEER_NKEYS = 128
PEER_TOPK = 16
PEER_QDIM = 256
EPS = 1e-6
NEG = -1e30

MIB = 1024 * 1024
NT_DIMS = (((1,), (1,)), ((), ()))


def _cparams(n_grid, vmem_mib):
    return pltpu.CompilerParams(dimension_semantics=("arbitrary",) * n_grid,
                                vmem_limit_bytes=vmem_mib * MIB)


def _rms(x, g):
    return x * lax.rsqrt(jnp.mean(x * x, axis=-1, keepdims=True) + EPS) * g


def _rms_matmul_kernel(x_ref, g_ref, w_ref, *refs, n_out):
    outs, xn_ref = refs[:n_out], refs[n_out]
    j = pl.program_id(1)

    @pl.when(j == 0)
    def _():
        xn_ref[...] = _rms(x_ref[...], g_ref[...]).astype(BF16)

    res = jnp.dot(xn_ref[...], w_ref[...], preferred_element_type=F32)
    for s in range(n_out):
        @pl.when(j == s)
        def _(s=s):
            outs[s][...] = res


def rms_matmul(x, g, w_bf, n_out, tm):
    t, d = x.shape
    tn = w_bf.shape[1] // n_out
    return pl.pallas_call(
        functools.partial(_rms_matmul_kernel, n_out=n_out),
        grid=(t // tm, n_out),
        in_specs=[pl.BlockSpec((tm, d), lambda i, j: (i, 0)),
                  pl.BlockSpec((1, d), lambda i, j: (0, 0)),
                  pl.BlockSpec((d, tn), lambda i, j: (0, j))],
        out_specs=[pl.BlockSpec((tm, tn), lambda i, j: (i, 0))] * n_out,
        out_shape=[jax.ShapeDtypeStruct((t, tn), F32)] * n_out,
        scratch_shapes=[pltpu.VMEM((tm, d), BF16)],
        compiler_params=_cparams(2, 48),
        name="rms_matmul",
    )(x, g.reshape(1, d), w_bf)


def _bucket_np(rel):
    n = np.maximum(rel, 0)
    nf = np.maximum(n, 1).astype(np.float32)
    large = MAX_EXACT + (np.log(nf / MAX_EXACT) / np.float32(math.log(REL_MAX_DIST / MAX_EXACT))
                         * (NUM_BUCKETS - MAX_EXACT)).astype(np.int32)
    large = np.minimum(large, NUM_BUCKETS - 1)
    return np.where(n < MAX_EXACT, n, large).astype(np.int32)


def _bias_kernel(rb_ref, rbx_ref, bkp_ref, bkf_ref, bkl_ref, bko_ref, tp_ref, tf_ref, tl_ref, to_ref):
    def lookup(bk, table):
        acc = jnp.full(bk.shape, NEG, F32)
        for b in range(NUM_BUCKETS):
            acc = jnp.where(bk == b, table(b), acc)
        return acc

    for h in range(N_HEADS):
        for t in range(2):
            tp_ref[h, t] = lookup(bkp_ref[t], lambda b: rb_ref[b, h])
    by_row = lambda b: rbx_ref[b]
    tf_ref[...] = lookup(bkf_ref[...], by_row)
    tl_ref[...] = lookup(bkl_ref[...], by_row)
    to_ref[...] = lookup(bko_ref[...], by_row)


def bias_tiles(rel_bias, n_past):
    i = np.arange(MOBA_BLOCK)[:, None]
    j = np.arange(MOBA_BLOCK)[None, :]
    bkp = np.stack([_bucket_np(i - j), _bucket_np(MOBA_BLOCK + i - j)])
    r = np.arange(N_HEADS * 8)[:, None]
    rh, rq = r // 8, r % 8
    c = np.arange(PAGE_SIZE * N_HEADS)[None, :]
    ct, ch = c // N_HEADS, c % N_HEADS
    bkf = np.where(rh == ch, NUM_BUCKETS - 1, -1)
    bkl = np.where(rh == ch, _bucket_np(n_past + rq - (n_past - PAGE_SIZE + ct)), -1)
    assert PAGE_SIZE >= REL_MAX_DIST
    co = np.arange(128)[None, :]
    coh, cot = co // 8, co % 8
    bko = np.where((rh == coh) & (cot <= rq), _bucket_np(rq - cot), -1)
    rbx = jnp.repeat(rel_bias, 8, axis=1)[:, :, None]
    vm = pl.BlockSpec(memory_space=pltpu.VMEM)
    i32 = lambda a: jnp.asarray(a.astype(np.int32))
    return pl.pallas_call(
        _bias_kernel,
        in_specs=[pl.BlockSpec(memory_space=pltpu.SMEM), vm, vm, vm, vm, vm],
        out_specs=[vm, vm, vm, vm],
        out_shape=[jax.ShapeDtypeStruct((N_HEADS, 2, MOBA_BLOCK, MOBA_BLOCK), F32),
                   jax.ShapeDtypeStruct(bkf.shape, F32),
                   jax.ShapeDtypeStruct(bkl.shape, F32),
                   jax.ShapeDtypeStruct(bko.shape, F32)],
        compiler_params=pltpu.CompilerParams(vmem_limit_bytes=32 * MIB),
        name="bias_tiles",
    )(rel_bias, rbx, i32(bkp), i32(bkf), i32(bkl), i32(bko))


def _split_bf16(x):
    hi = x.astype(BF16)
    lo = (x - hi.astype(F32)).astype(BF16)
    return hi, lo


def _moba_prompt_kernel(q_ref, k_ref, v_ref, bias_ref, c31_ref, o_ref,
                        kbf, vbf, kmean, sel_ref, m_ref, l_ref, acc_ref):
    qi = pl.program_id(2)
    nb = kbf.shape[0] // MOBA_BLOCK
    scale = HEAD_DIM ** -0.5

    @pl.when(qi == 0)
    def _():
        kbf[...] = k_ref[...].astype(BF16)
        vbf[...] = v_ref[...].astype(BF16)
        kmean[...] = jnp.zeros(kmean.shape, F32)
        for b in range(nb):
            kmean[b:b + 1, :] = jnp.mean(k_ref[b * MOBA_BLOCK:(b + 1) * MOBA_BLOCK, :], axis=0, keepdims=True)

    q = q_ref[...]
    q_hi, q_lo = _split_bf16(q)
    km_hi, km_lo = _split_bf16(kmean[...])
    gate = (lax.dot_general(q_hi, km_hi, NT_DIMS, preferred_element_type=F32)
            + lax.dot_general(q_hi, km_lo, NT_DIMS, preferred_element_type=F32)
            + lax.dot_general(q_lo, km_hi, NT_DIMS, preferred_element_type=F32))

    lane = lax.broadcasted_iota(jnp.int32, gate.shape, 1)
    rank = jnp.zeros(gate.shape, F32)
    for b2 in range(nb):
        gb = gate[:, b2:b2 + 1]
        beats = (gb > gate) | ((gb == gate) & (b2 < lane))
        rank = rank + jnp.where(beats, jnp.where(b2 < qi, 1.0, 0.0), 0.0)
    sel = jnp.where((lane < qi) & (rank < float(MOBA_TOPK)), 1.0, 0.0)
    for b in range(nb):
        sel_ref[b] = jnp.broadcast_to(sel[:, b:b + 1], (MOBA_BLOCK, 128))

    def attend(kb, bias, mask):
        start = pl.multiple_of(kb * MOBA_BLOCK, MOBA_BLOCK)
        s = lax.dot_general(q_hi, kbf[pl.ds(start, MOBA_BLOCK), :], NT_DIMS, preferred_element_type=F32)
        s = jnp.where(mask, s * scale + bias, NEG)
        m_old = m_ref[...]
        m_new = jnp.maximum(m_old, jnp.max(s, axis=-1, keepdims=True))
        alpha = jnp.exp(m_old - m_new)
        p = jnp.exp(s - m_new)
        l_ref[...] = alpha * l_ref[...] + jnp.sum(p, axis=-1, keepdims=True)
        acc_ref[...] = alpha * acc_ref[...] + jnp.dot(p.astype(BF16), vbf[pl.ds(start, MOBA_BLOCK), :],
                                                      preferred_element_type=F32)
        m_ref[...] = m_new

    m_ref[...] = jnp.full(m_ref.shape, NEG, F32)
    l_ref[...] = jnp.zeros(l_ref.shape, F32)
    acc_ref[...] = jnp.zeros(acc_ref.shape, F32)
    row = lax.broadcasted_iota(jnp.int32, (MOBA_BLOCK, MOBA_BLOCK), 0)
    col = lax.broadcasted_iota(jnp.int32, (MOBA_BLOCK, MOBA_BLOCK), 1)
    attend(qi, bias_ref[0, 0], row >= col)

    def sel_mask(kb):
        sb = sel_ref[kb] > 0.5
        return jnp.concatenate([sb, sb], axis=1)

    @pl.when(qi >= 1)
    def _():
        attend(qi - 1, bias_ref[0, 1], sel_mask(qi - 1))

    c31 = c31_ref[pl.program_id(1)]

    def far(kb, carry):
        attend(kb, c31, sel_mask(kb))
        return carry

    lax.fori_loop(0, jnp.maximum(qi - 1, 0), far, 0)
    o_ref[...] = (acc_ref[...] / l_ref[...]).astype(o_ref.dtype)


def moba_prompt(q, k, v, bias_p, c31_h, n_seq, seq):
    nq = seq // MOBA_BLOCK
    return pl.pallas_call(
        _moba_prompt_kernel,
        grid=(n_seq, N_HEADS, nq),
        in_specs=[pl.BlockSpec((MOBA_BLOCK, HEAD_DIM), lambda n, h, i: (n * nq + i, h)),
                  pl.BlockSpec((seq, HEAD_DIM), lambda n, h, i: (n, h)),
                  pl.BlockSpec((seq, HEAD_DIM), lambda n, h, i: (n, h)),
                  pl.BlockSpec((1, 2, MOBA_BLOCK, MOBA_BLOCK), lambda n, h, i: (h, 0, 0, 0)),
                  pl.BlockSpec(memory_space=pltpu.SMEM)],
        out_specs=pl.BlockSpec((MOBA_BLOCK, HEAD_DIM), lambda n, h, i: (n * nq + i, h)),
        out_shape=jax.ShapeDtypeStruct(q.shape, BF16),
        scratch_shapes=[pltpu.VMEM((seq, HEAD_DIM), BF16), pltpu.VMEM((seq, HEAD_DIM), BF16),
                        pltpu.VMEM((128, HEAD_DIM), F32), pltpu.VMEM((nq, MOBA_BLOCK, 128), F32),
                        pltpu.VMEM((MOBA_BLOCK, 1), F32), pltpu.VMEM((MOBA_BLOCK, 1), F32),
                        pltpu.VMEM((MOBA_BLOCK, HEAD_DIM), F32)],
        compiler_params=_cparams(3, 32),
        name="moba_prompt",
    )(q, k, v, bias_p, c31_h)


def _by_head_rows(x):
    return jnp.concatenate([x[:, h * HEAD_DIM:(h + 1) * HEAD_DIM] for h in range(x.shape[1] // HEAD_DIM)], axis=0)


def _by_head_lanes(x, n_heads):
    t = x.shape[0] // n_heads
    return jnp.concatenate([x[h * t:(h + 1) * t] for h in range(n_heads)], axis=1)


def _moba_sample_kernel(pt_ref, q_ref, kn_ref, vn_ref, tf_ref, tl_ref, to_ref, *refs, n_pages):
    kp, vp, o_ref = refs[:n_pages], refs[n_pages:2 * n_pages], refs[2 * n_pages]
    ppb = MOBA_BLOCK // PAGE_SIZE
    nb = n_pages // ppb
    scale = HEAD_DIM ** -0.5
    t_new = q_ref.shape[1]
    nr = N_HEADS * t_new

    q = _by_head_rows(q_ref[0])
    q_bf = q.astype(BF16)
    zpad = jnp.zeros((128 - nr, HEAD_DIM), F32)
    kn = jnp.concatenate([_by_head_rows(kn_ref[0]), zpad], axis=0).astype(BF16)
    vn = jnp.concatenate([_by_head_rows(vn_ref[0]), zpad], axis=0).astype(BF16)

    gates = []
    for b in range(nb):
        ksum = jnp.zeros((N_HEADS, HEAD_DIM), F32)
        for pg in range(ppb):
            ksum = ksum + jnp.sum(kp[b * ppb + pg][...].reshape(PAGE_SIZE, N_HEADS, HEAD_DIM), axis=0)
        kmean = ksum / float(MOBA_BLOCK)
        krep = jnp.concatenate([jnp.broadcast_to(kmean[h:h + 1, :], (t_new, HEAD_DIM)) for h in range(N_HEADS)],
                               axis=0)
        gates.append(jnp.sum(q * krep, axis=1, keepdims=True))

    penalty = []
    for b in range(nb):
        rank = jnp.zeros((nr, 1), F32)
        for b2 in range(nb):
            if b2 != b:
                beats = (gates[b2] > gates[b]) | ((gates[b2] == gates[b]) & (b2 < b))
                rank = rank + jnp.where(beats, 1.0, 0.0)
        penalty.append(jnp.where(rank < float(MOBA_TOPK), 0.0, NEG))

    s = lax.dot_general(q_bf, kn, NT_DIMS, preferred_element_type=F32) * scale + to_ref[...]
    m = jnp.max(s, axis=-1, keepdims=True)
    p = jnp.exp(s - m)
    lsum = jnp.sum(p, axis=-1, keepdims=True)
    acc = jnp.dot(p.astype(BF16), vn, preferred_element_type=F32)
    for pg in range(n_pages):
        s = lax.dot_general(q_bf, kp[pg][...].astype(BF16), NT_DIMS, preferred_element_type=F32)
        bias = tl_ref[...] if pg == n_pages - 1 else tf_ref[...]
        s = s * scale + bias + penalty[pg // ppb]
        m_new = jnp.maximum(m, jnp.max(s, axis=-1, keepdims=True))
        alpha = jnp.exp(m - m_new)
        p = jnp.exp(s - m_new)
        lsum = alpha * lsum + jnp.sum(p, axis=-1, keepdims=True)
        acc = alpha * acc + jnp.dot(p.astype(BF16), vp[pg][...].astype(BF16), preferred_element_type=F32)
        m = m_new
    o_ref[0] = _by_head_lanes(acc / lsum, N_HEADS)


def moba_sample(q, kn, vn, ck, cv, page_table, t_far, t_last, t_own):
    nseq, n_pages = page_table.shape
    tok = pl.BlockSpec((1,) + q.shape[1:], lambda b, pt: (b, 0, 0))

    def page_spec(p):
        return pl.BlockSpec((None,) + ck.shape[1:], lambda b, pt, p=p: (pt[b, p], 0, 0))

    const = lambda a: pl.BlockSpec(a.shape, lambda b, pt: (0, 0))
    grid_spec = pltpu.PrefetchScalarGridSpec(
        num_scalar_prefetch=1, grid=(nseq,),
        in_specs=[tok, tok, tok, const(t_far), const(t_last), const(t_own)]
                 + [page_spec(p) for p in range(n_pages)] * 2,
        out_specs=tok)
    return pl.pallas_call(
        functools.partial(_moba_sample_kernel, n_pages=n_pages),
        grid_spec=grid_spec,
        out_shape=jax.ShapeDtypeStruct(q.shape, F32),
        compiler_params=_cparams(1, 56),
        name="moba_sample",
    )(page_table, q, kn, vn, t_far, t_last, t_own, *([ck] * n_pages), *([cv] * n_pages))


HIST_ROWS = 32
HIST_OFF = HIST_ROWS - (CONV_K - 1)


def _conv_kernel(ga_ref, gg_ref, hist_ref, w_ref, b_ref, lg_ref, lb_ref, o_ref, nb_ref, ext_ref, y_ref,
                 *, tt, n_t):
    t = pl.program_id(1)

    @pl.when(t == 0)
    def _():
        ext_ref[HIST_OFF:HIST_ROWS, :] = hist_ref[0]

    ext_ref[HIST_ROWS:HIST_ROWS + tt, :] = ga_ref[...] * jax.nn.sigmoid(gg_ref[...])
    for c in range(CONV_CH // 128):
        cs = slice(c * 128, (c + 1) * 128)
        acc = jnp.zeros((tt, 128), F32)
        for j in range(CONV_K):
            acc = acc + w_ref[j:j + 1, cs] * ext_ref[HIST_OFF + j:HIST_OFF + j + tt, cs]
        y_ref[:, cs] = acc + b_ref[:, cs]
    y = y_ref[...]
    mu = jnp.mean(y, axis=-1, keepdims=True)
    yc = y - mu
    var = jnp.mean(yc * yc, axis=-1, keepdims=True)
    yn = yc * lax.rsqrt(var + EPS) * lg_ref[...] + lb_ref[...]
    o_ref[...] = (yn * jax.nn.sigmoid(yn)).astype(o_ref.dtype)

    @pl.when(t == n_t - 1)
    def _():
        nb_ref[0] = ext_ref[tt + HIST_OFF:tt + HIST_ROWS, :]

    if n_t > 1:
        ext_ref[0:HIST_ROWS, :] = ext_ref[tt:tt + HIST_ROWS, :]


def conformer_conv(ga, gg, hist, w_dw, b_dw, ln_g, ln_b, n_seq, seq, tt):
    n_t = seq // tt
    out_dtype = BF16 if tt % 16 == 0 else F32
    row = lambda a: a.reshape(1, CONV_CH)
    cvec = pl.BlockSpec((1, CONV_CH), lambda n, t: (0, 0))
    tile = pl.BlockSpec((tt, CONV_CH), lambda n, t: (n * n_t + t, 0))
    hist_spec = pl.BlockSpec((1, CONV_K - 1, CONV_CH), lambda n, t: (n, 0, 0))
    return pl.pallas_call(
        functools.partial(_conv_kernel, tt=tt, n_t=n_t),
        grid=(n_seq, n_t),
        in_specs=[tile, tile, hist_spec, pl.BlockSpec((CONV_K, CONV_CH), lambda n, t: (0, 0)),
                  cvec, cvec, cvec],
        out_specs=[tile, hist_spec],
        out_shape=[jax.ShapeDtypeStruct((n_seq * seq, CONV_CH), out_dtype),
                   jax.ShapeDtypeStruct((n_seq, CONV_K - 1, CONV_CH), F32)],
        scratch_shapes=[pltpu.VMEM((HIST_ROWS + tt, CONV_CH), F32), pltpu.VMEM((tt, CONV_CH), F32)],
        compiler_params=_cparams(2, 32),
        name="conformer_conv",
    )(ga, gg, hist, w_dw, row(b_dw), row(ln_g), row(ln_b))


def _out_proj_kernel(x_ref, a_ref, c_ref, w_ref, o_ref):
    wa = a_ref.shape[1]
    y = jnp.dot(a_ref[...].astype(BF16), w_ref[0:wa, :], preferred_element_type=F32)
    y = y + jnp.dot(c_ref[...].astype(BF16), w_ref[wa:, :], preferred_element_type=F32)
    o_ref[...] = x_ref[...] + y


def out_proj(x, a, c, w_bf, tm):
    t, d = x.shape
    return pl.pallas_call(
        _out_proj_kernel,
        grid=(t // tm,),
        in_specs=[pl.BlockSpec((tm, d), lambda i: (i, 0)),
                  pl.BlockSpec((tm, a.shape[1]), lambda i: (i, 0)),
                  pl.BlockSpec((tm, c.shape[1]), lambda i: (i, 0)),
                  pl.BlockSpec(w_bf.shape, lambda i: (0, 0))],
        out_specs=pl.BlockSpec((tm, d), lambda i: (i, 0)),
        out_shape=jax.ShapeDtypeStruct((t, d), F32),
        compiler_params=_cparams(1, 48),
        name="out_proj",
    )(x, a, c, w_bf)


def _softmax_pv(s, mv_bf):
    m = jnp.max(s, axis=-1, keepdims=True)
    p = jnp.exp(s - m)
    o = jnp.dot(p.astype(BF16), mv_bf, preferred_element_type=F32)
    return o / jnp.sum(p, axis=-1, keepdims=True)


def _cross_kernel(h_ref, g_ref, wq_ref, wo_ref, mk_ref, mv_ref, o_ref, *, n_grp, rows):
    scale = HEAD_DIM ** -0.5
    h = h_ref[...]
    hn = _rms(h, g_ref[...]).astype(BF16)
    q = jnp.dot(hn, wq_ref[...], preferred_element_type=F32)
    outs = []
    for g in range(n_grp):
        mk = mk_ref[g].astype(BF16)
        mv = mv_ref[g].astype(BF16)
        qg = q[g * rows:(g + 1) * rows]
        if rows >= 128:
            heads = []
            for hd in range(X_HEADS):
                cs = slice(hd * HEAD_DIM, (hd + 1) * HEAD_DIM)
                s = lax.dot_general(qg[:, cs].astype(BF16), mk[:, cs], NT_DIMS, preferred_element_type=F32)
                heads.append(_softmax_pv(s * scale, mv[:, cs]))
            outs.append(jnp.concatenate(heads, axis=1))
        else:
            qx = _by_head_rows(qg).astype(BF16)
            s = lax.dot_general(qx, mk, NT_DIMS, preferred_element_type=F32) * scale
            rowh = lax.broadcasted_iota(jnp.int32, s.shape, 0) // rows
            colh = lax.broadcasted_iota(jnp.int32, s.shape, 1) % X_HEADS
            o = _softmax_pv(jnp.where(rowh == colh, s, NEG), mv)
            outs.append(_by_head_lanes(o, X_HEADS))
    o_all = outs[0] if n_grp == 1 else jnp.concatenate(outs, axis=0)
    o_ref[...] = h + jnp.dot(o_all.astype(BF16), wo_ref[...], preferred_element_type=F32)


def cross_block(h, g, wq_bf, wo_bf, mk, mv, n_grp, rows, tiles_per_seq):
    t, d = h.shape
    tm = n_grp * rows
    if n_grp == 1:
        mem_map = lambda i: (i // tiles_per_seq, 0, 0)
    else:
        mem_map = lambda i: (i, 0, 0)
    mem_spec = pl.BlockSpec((n_grp,) + mk.shape[1:], mem_map)
    return pl.pallas_call(
        functools.partial(_cross_kernel, n_grp=n_grp, rows=rows),
        grid=(t // tm,),
        in_specs=[pl.BlockSpec((tm, d), lambda i: (i, 0)),
                  pl.BlockSpec((1, d), lambda i: (0, 0)),
                  pl.BlockSpec(wq_bf.shape, lambda i: (0, 0)),
                  pl.BlockSpec(wo_bf.shape, lambda i: (0, 0)),
                  mem_spec, mem_spec],
        out_specs=pl.BlockSpec((tm, d), lambda i: (i, 0)),
        out_shape=jax.ShapeDtypeStruct((t, d), F32),
        compiler_params=_cparams(1, 48),
        name="cross_block",
    )(h, g.reshape(1, d), wq_bf, wo_bf, mk, mv)


def _top16(s):
    n, t = s.shape
    row = lax.broadcasted_iota(jnp.int32, (n, t), 0).astype(F32)
    row16 = lax.broadcasted_iota(jnp.int32, (PEER_TOPK, t), 0)
    rank = jnp.full((n, t), float(PEER_TOPK), F32)
    vals = jnp.zeros((PEER_TOPK, t), F32)
    for r in range(PEER_TOPK):
        m = jnp.max(s, axis=0, keepdims=True)
        idx = jnp.min(jnp.where(s == m, row, float(n)), axis=0, keepdims=True)
        hit = row == idx
        rank = jnp.where(hit, float(r), rank)
        s = jnp.where(hit, -jnp.inf, s)
        vals = jnp.where(row16 == r, m, vals)
    return vals, rank


def _pair_pieces(v1, v2, e1, e2):
    t = v1.shape[1]
    sub = lax.broadcasted_iota(jnp.int32, (8, t), 0)
    subf = sub.astype(F32)
    pieces = []

    def col(b, a0, a_max):
        a = sub + a0
        pieces.append(dict(c=v1[a0:a0 + 8] + v2[b:b + 1], e=e1[a0:a0 + 8] * e2[b:b + 1],
                           f=(subf + a0) * 16.0 + b, ok=a <= a_max, a0=a0, row_a=None))

    def rowp(a, b0, b_min, b_max):
        b = sub + b0
        pieces.append(dict(c=v1[a:a + 1] + v2[b0:b0 + 8], e=e1[a:a + 1] * e2[b0:b0 + 8],
                           f=a * 16.0 + (subf + b0), ok=(b >= b_min) & (b <= b_max), a0=None, row_a=a))

    col(0, 0, 15), col(0, 8, 15), col(1, 0, 7), col(2, 0, 4), col(3, 0, 3)
    rowp(0, 8, 8, 15), rowp(0, 0, 4, 7), rowp(1, 0, 4, 7), rowp(2, 0, 4, 4)
    for p in pieces:
        p["c"] = jnp.where(p["ok"], p["c"], -jnp.inf)
        p["f"] = jnp.where(p["ok"], p["f"], -1.0)
    return pieces


def _peer_route_kernel(h_ref, g_ref, wpq_ref, sk_ref, xn_ref, cnt_ref, p1_ref, rk2_ref, e2_ref, q_scr):
    tq = h_ref.shape[0]
    hn = _rms(h_ref[...], g_ref[...]).astype(BF16)
    xn_ref[...] = hn
    q = jnp.dot(hn, wpq_ref[...], preferred_element_type=F32)
    for hh in range(PEER_HEADS):
        q_scr[hh] = q[:, hh * PEER_QDIM:(hh + 1) * PEER_QDIM].astype(BF16)

    def head(hh, carry):
        qh = q_scr[hh]
        half = PEER_QDIM // 2
        s1 = lax.dot_general(sk_ref[hh, 0], qh[:, :half], NT_DIMS, preferred_element_type=F32)
        s2 = lax.dot_general(sk_ref[hh, 1], qh[:, half:], NT_DIMS, preferred_element_type=F32)
        v1, rank1 = _top16(s1)
        v2, rank2 = _top16(s2)
        e1 = jnp.exp(v1 - v1[0:1])
        e2 = jnp.exp(v2 - v2[0:1])
        pieces = _pair_pieces(v1, v2, e1, e2)

        taken = [jnp.zeros((8, tq), F32) for _ in pieces]
        cs = [p["c"] for p in pieces]
        for _ in range(PEER_TOPK):
            m = cs[0]
            for c in cs[1:]:
                m = jnp.maximum(m, c)
            m = jnp.max(m, axis=0, keepdims=True)
            fm = None
            for c, p in zip(cs, pieces):
                cand = jnp.where(c == m, p["f"], 1e9)
                fm = cand if fm is None else jnp.minimum(fm, cand)
            fm = jnp.min(fm, axis=0, keepdims=True)
            for i, p in enumerate(pieces):
                hit = p["f"] == fm
                taken[i] = jnp.where(hit, 1.0, taken[i])
                cs[i] = jnp.where(hit, -jnp.inf, cs[i])

        row16 = lax.broadcasted_iota(jnp.int32, (PEER_TOPK, tq), 0)
        n1 = jnp.zeros((PEER_TOPK, tq), F32)
        z = jnp.zeros((1, tq), F32)
        for tf, p in zip(taken, pieces):
            z = z + jnp.sum(tf * p["e"], axis=0, keepdims=True)
            if p["row_a"] is None:
                pad = jnp.zeros((8, tq), F32)
                n1 = n1 + (jnp.concatenate([tf, pad], axis=0) if p["a0"] == 0
                           else jnp.concatenate([pad, tf], axis=0))
            else:
                n1 = n1 + jnp.where(row16 == p["row_a"], jnp.sum(tf, axis=0, keepdims=True), 0.0)

        cnt1 = jnp.zeros(rank1.shape, F32)
        for r in range(PEER_TOPK):
            cnt1 = jnp.where(rank1 == float(r), n1[r:r + 1], cnt1)
        cnt_ref[hh] = cnt1
        p1_ref[hh] = jnp.where(rank1 < float(PEER_TOPK), jnp.exp(s1 - v1[0:1]) / z, 0.0)
        rk2_ref[hh] = rank2
        e2_ref[hh] = jnp.where(rank2 < float(PEER_TOPK), jnp.exp(s2 - v2[0:1]), 0.0)
        return carry

    lax.fori_loop(0, PEER_HEADS, head, 0)


def peer_route(h, g, wpq_bf, sk_bf, tq):
    t, d = h.shape
    tab = pl.BlockSpec((PEER_HEADS, PEER_NKEYS, tq), lambda i: (0, 0, i))
    tab_shape = jax.ShapeDtypeStruct((PEER_HEADS, PEER_NKEYS, t), F32)
    return pl.pallas_call(
        _peer_route_kernel,
        grid=(t // tq,),
        in_specs=[pl.BlockSpec((tq, d), lambda i: (i, 0)),
                  pl.BlockSpec((1, d), lambda i: (0, 0)),
                  pl.BlockSpec(wpq_bf.shape, lambda i: (0, 0)),
                  pl.BlockSpec(sk_bf.shape, lambda i: (0, 0, 0, 0))],
        out_specs=[pl.BlockSpec((tq, d), lambda i: (i, 0)), tab, tab, tab, tab],
        out_shape=[jax.ShapeDtypeStruct((t, d), BF16), tab_shape, tab_shape, tab_shape, tab_shape],
        scratch_shapes=[pltpu.VMEM((PEER_HEADS, tq, PEER_QDIM), BF16)],
        compiler_params=_cparams(1, 48),
        name="peer_route",
    )(h, g.reshape(1, d), wpq_bf, sk_bf)


def _peer_dense_kernel(xn_ref, u_ref, v_ref, cnt_ref, p1_ref, rk2_ref, e2_ref, h_ref, fg_ref, y_ref, acc_ref,
                       *, te, n_e):
    j = pl.program_id(1)

    @pl.when(j == 0)
    def _():
        acc_ref[...] = jnp.zeros(acc_ref.shape, F32)

    act = jax.nn.gelu(lax.dot_general(u_ref[...], xn_ref[...], NT_DIMS, preferred_element_type=F32))
    groups = te // PEER_NKEYS
    parts = []
    for a in range(groups):
        i1 = j * groups + a
        gate = jnp.zeros((PEER_NKEYS, act.shape[1]), F32)
        for hh in range(PEER_HEADS):
            cnt_row = cnt_ref[hh, pl.ds(i1, 1), :]
            p1_row = p1_ref[hh, pl.ds(i1, 1), :]
            gate = gate + jnp.where(rk2_ref[hh] < cnt_row, p1_row * e2_ref[hh], 0.0)
        parts.append(act[a * PEER_NKEYS:(a + 1) * PEER_NKEYS] * gate)
    hm_t = parts[0] if groups == 1 else jnp.concatenate(parts, axis=0)
    acc_ref[...] += jnp.dot(hm_t.T.astype(BF16), v_ref[...], preferred_element_type=F32)

    @pl.when(j == n_e - 1)
    def _():
        y_ref[...] = _rms(h_ref[...] + acc_ref[...], fg_ref[...])


def peer_dense(xn, u_bf, v_bf, tabs, h, final_g, tq, te):
    t, d = h.shape
    n_e = u_bf.shape[0] // te
    tab = pl.BlockSpec((PEER_HEADS, PEER_NKEYS, tq), lambda i, j: (0, 0, i))
    tok = pl.BlockSpec((tq, d), lambda i, j: (i, 0))
    exp = pl.BlockSpec((te, d), lambda i, j: (j, 0))
    return pl.pallas_call(
        functools.partial(_peer_dense_kernel, te=te, n_e=n_e),
        grid=(t // tq, n_e),
        in_specs=[tok, exp, exp, tab, tab, tab, tab, tok, pl.BlockSpec((1, d), lambda i, j: (0, 0))],
        out_specs=tok,
        out_shape=jax.ShapeDtypeStruct((t, d), F32),
        scratch_shapes=[pltpu.VMEM((tq, d), F32)],
        compiler_params=_cparams(2, 48),
        name="peer_dense",
    )(xn, u_bf, v_bf, *tabs, h, final_g.reshape(1, d))


def kernel(x_prompt, x_sample, mem_prompt, cache_k, cache_v, cache_conv, cache_mem_k, cache_mem_v, page_table,
           norm1_g, w_in, rel_bias, conv_w, conv_b, conv_ln_g, conv_ln_b, w_out, norm2_g, mem_norm_g, w_cq,
           w_mk, w_mv, w_co, norm3_g, w_pq, peer_sub_keys, peer_u, peer_v, final_g):
    depth = w_in.shape[0]
    assert depth == 1, "single-layer step"
    n_p, seq, d = x_prompt.shape
    n_s, t_s, _ = x_sample.shape
    l = 0
    bf = lambda a: a.astype(BF16)

    w_in_bf, w_out_bf = bf(w_in[l]), bf(w_out[l])
    w_cq_bf, w_co_bf, w_pq_bf = bf(w_cq[l]), bf(w_co[l]), bf(w_pq[l])
    w_mem_bf = bf(jnp.concatenate([w_mk[l], w_mv[l]], axis=1))
    sk_bf, u_bf, v_bf = bf(peer_sub_keys[l]), bf(peer_u[l]), bf(peer_v[l])

    n_pool, n_pages = cache_k.shape[1], page_table.shape[1]
    bias_p, t_far, t_last, t_own = bias_tiles(rel_bias, n_pages * PAGE_SIZE)
    c31_h = rel_bias[NUM_BUCKETS - 1]

    def tail(h, mk, mv, n_grp, rows, tiles_per_seq):
        h = cross_block(h, norm2_g[l], w_cq_bf, w_co_bf, mk, mv, n_grp, rows, tiles_per_seq)
        xn, *tabs = peer_route(h, norm3_g[l], w_pq_bf, sk_bf, tq=256)
        return peer_dense(xn, u_bf, v_bf, tabs, h, final_g, tq=256, te=512)

    xp = x_prompt.reshape(n_p * seq, d)
    mk_p, mv_p = rms_matmul(mem_prompt.reshape(n_p * N_MEM, d), mem_norm_g[l], w_mem_bf, 2, tm=512)
    q_p, k_p, v_p, ga_p, gg_p = rms_matmul(xp, norm1_g[l], w_in_bf, 5, tm=512)
    attn_p = moba_prompt(q_p, k_p, v_p, bias_p, c31_h, n_p, seq)
    zero_buf = jnp.zeros((n_p, CONV_K - 1, CONV_CH), F32)
    conv_p, buf_p = conformer_conv(ga_p, gg_p, zero_buf, conv_w[l], conv_b[l], conv_ln_g[l], conv_ln_b[l],
                                   n_p, seq, tt=256)
    h_p = out_proj(xp, attn_p, conv_p, w_out_bf, tm=512)
    y_p = tail(h_p, mk_p.reshape(n_p, N_MEM, X_WIDTH), mv_p.reshape(n_p, N_MEM, X_WIDTH), 1, 256, seq // 256)

    xs = x_sample.reshape(n_s * t_s, d)
    q_s, k_s, v_s, ga_s, gg_s = rms_matmul(xs, norm1_g[l], w_in_bf, 5, tm=512)
    tok3 = lambda a: a.reshape(n_s, t_s, ATTN_WIDTH)
    attn_s = moba_sample(tok3(q_s), tok3(k_s), tok3(v_s),
                         cache_k[l].reshape(n_pool, PAGE_SIZE * N_HEADS, HEAD_DIM),
                         cache_v[l].reshape(n_pool, PAGE_SIZE * N_HEADS, HEAD_DIM),
                         page_table, t_far, t_last, t_own)
    conv_s, buf_s = conformer_conv(ga_s, gg_s, cache_conv[l], conv_w[l], conv_b[l], conv_ln_g[l], conv_ln_b[l],
                                   n_s, t_s, tt=t_s)
    h_s = out_proj(xs, attn_s.reshape(n_s * t_s, ATTN_WIDTH), conv_s, w_out_bf, tm=512)
    y_s = tail(h_s, cache_mem_k[l].reshape(n_s, N_MEM * X_HEADS, HEAD_DIM),
               cache_mem_v[l].reshape(n_s, N_MEM * X_HEADS, HEAD_DIM), 8, t_s, 1)

    n_pg = seq // PAGE_SIZE
    return (y_p.reshape(n_p, seq, d), y_s.reshape(n_s, t_s, d),
            k_p.reshape(1, n_p, n_pg, PAGE_SIZE, N_HEADS, HEAD_DIM),
            v_p.reshape(1, n_p, n_pg, PAGE_SIZE, N_HEADS, HEAD_DIM),
            buf_p[None],
            mk_p.reshape(1, n_p, N_MEM, X_HEADS, HEAD_DIM), mv_p.reshape(1, n_p, N_MEM, X_HEADS, HEAD_DIM),
            k_s.reshape(1, n_s, t_s, N_HEADS, HEAD_DIM), v_s.reshape(1, n_s, t_s, N_HEADS, HEAD_DIM),
            buf_s[None])
```

```python
import functools
import math

import numpy as np
import jax
import jax.numpy as jnp
from jax import lax
from jax.experimental import pallas as pl
from jax.experimental.pallas import tpu as pltpu

F32 = jnp.float32
BF16 = jnp.bfloat16

D_MODEL = 2048
HEAD_DIM = 128
N_HEADS = 8
ATTN_WIDTH = N_HEADS * HEAD_DIM
CONV_CH = 1024
MOBA_BLOCK = 256
MOBA_TOPK = 3
PAGE_SIZE = 128
NUM_BUCKETS = 32
MAX_EXACT = 16
REL_MAX_DIST = 128
CONV_K = 31
N_MEM = 256
X_HEADS = 4
X_WIDTH = X_HEADS * HEAD_DIM
PEER_HEADS = 8
PEER_NKEYS = 128
PEER_TOPK = 16
PEER_QDIM = 256
EPS = 1e-6
NEG = -1e30

MIB = 1024 * 1024
NT_DIMS = (((1,), (1,)), ((), ()))


def _cparams(n_grid, vmem_mib):
    return pltpu.CompilerParams(dimension_semantics=("arbitrary",) * n_grid,
                                vmem_limit_bytes=vmem_mib * MIB)


def _rms(x, g):
    return x * lax.rsqrt(jnp.mean(x * x, axis=-1, keepdims=True) + EPS) * g


def _rms_matmul_kernel(x_ref, g_ref, w_ref, *refs, n_out):
    outs, xn_ref = refs[:n_out], refs[n_out]
    j = pl.program_id(1)

    @pl.when(j == 0)
    def _():
        xn_ref[...] = _rms(x_ref[...], g_ref[...]).astype(BF16)

    res = jnp.dot(xn_ref[...], w_ref[...], preferred_element_type=F32)
    for s in range(n_out):
        @pl.when(j == s)
        def _(s=s):
            outs[s][...] = res


def rms_matmul(x, g, w_bf, n_out, tm):
    t, d = x.shape
    tn = w_bf.shape[1] // n_out
    return pl.pallas_call(
        functools.partial(_rms_matmul_kernel, n_out=n_out),
        grid=(t // tm, n_out),
        in_specs=[pl.BlockSpec((tm, d), lambda i, j: (i, 0)),
                  pl.BlockSpec((1, d), lambda i, j: (0, 0)),
                  pl.BlockSpec((d, tn), lambda i, j: (0, j))],
        out_specs=[pl.BlockSpec((tm, tn), lambda i, j: (i, 0))] * n_out,
        out_shape=[jax.ShapeDtypeStruct((t, tn), F32)] * n_out,
        scratch_shapes=[pltpu.VMEM((tm, d), BF16)],
        compiler_params=_cparams(2, 48),
        name="rms_matmul",
    )(x, g.reshape(1, d), w_bf)


def _bucket_np(rel):
    n = np.maximum(rel, 0)
    nf = np.maximum(n, 1).astype(np.float32)
    large = MAX_EXACT + (np.log(nf / MAX_EXACT) / np.float32(math.log(REL_MAX_DIST / MAX_EXACT))
                         * (NUM_BUCKETS - MAX_EXACT)).astype(np.int32)
    large = np.minimum(large, NUM_BUCKETS - 1)
    return np.where(n < MAX_EXACT, n, large).astype(np.int32)


def _bias_kernel(rb_ref, rbx_ref, bkp_ref, bkf_ref, bkl_ref, bko_ref, tp_ref, tf_ref, tl_ref, to_ref):
    def lookup(bk, table):
        acc = jnp.full(bk.shape, NEG, F32)
        for b in range(NUM_BUCKETS):
            acc = jnp.where(bk == b, table(b), acc)
        return acc

    for h in range(N_HEADS):
        for t in range(2):
            tp_ref[h, t] = lookup(bkp_ref[t], lambda b: rb_ref[b, h])
    by_row = lambda b: rbx_ref[b]
    tf_ref[...] = lookup(bkf_ref[...], by_row)
    tl_ref[...] = lookup(bkl_ref[...], by_row)
    to_ref[...] = lookup(bko_ref[...], by_row)


def bias_tiles(rel_bias, n_past):
    key = np.arange(MOBA_BLOCK)[:, None]
    qry = np.arange(MOBA_BLOCK)[None, :]
    bkp = np.stack([_bucket_np(qry - key), _bucket_np(MOBA_BLOCK + qry - key)])
    r = np.arange(N_HEADS * 8)[:, None]
    rh, rq = r // 8, r % 8
    c = np.arange(PAGE_SIZE * N_HEADS)[None, :]
    ct, ch = c // N_HEADS, c % N_HEADS
    bkf = np.where(rh == ch, NUM_BUCKETS - 1, -1)
    bkl = np.where(rh == ch, _bucket_np(n_past + rq - (n_past - PAGE_SIZE + ct)), -1)
    assert PAGE_SIZE >= REL_MAX_DIST
    co = np.arange(128)[None, :]
    coh, cot = co // 8, co % 8
    bko = np.where((rh == coh) & (cot <= rq), _bucket_np(rq - cot), -1)
    rbx = jnp.repeat(rel_bias, 8, axis=1)[:, :, None]
    vm = pl.BlockSpec(memory_space=pltpu.VMEM)
    i32 = lambda a: jnp.asarray(a.astype(np.int32))
    return pl.pallas_call(
        _bias_kernel,
        in_specs=[pl.BlockSpec(memory_space=pltpu.SMEM), vm, vm, vm, vm, vm],
        out_specs=[vm, vm, vm, vm],
        out_shape=[jax.ShapeDtypeStruct((N_HEADS, 2, MOBA_BLOCK, MOBA_BLOCK), F32),
                   jax.ShapeDtypeStruct(bkf.shape, F32),
                   jax.ShapeDtypeStruct(bkl.shape, F32),
                   jax.ShapeDtypeStruct(bko.shape, F32)],
        compiler_params=pltpu.CompilerParams(vmem_limit_bytes=32 * MIB),
        name="bias_tiles",
    )(rel_bias, rbx, i32(bkp), i32(bkf), i32(bkl), i32(bko))


def _split_bf16(x):
    hi = x.astype(BF16)
    lo = (x - hi.astype(F32)).astype(BF16)
    return hi, lo


def _moba_prompt_kernel(q_ref, k_ref, v_ref, bias_ref, c31_ref, o_ref, qbf, kbf, vt, pen_ref, s_ref):
    nb = kbf.shape[0] // MOBA_BLOCK
    scale = HEAD_DIM ** -0.5
    blk = lambda b: slice(b * MOBA_BLOCK, (b + 1) * MOBA_BLOCK)

    q = q_ref[...]
    q_hi, q_lo = _split_bf16(q)
    qbf[...] = q_hi
    kbf[...] = k_ref[...].astype(BF16)
    km = jnp.concatenate([jnp.mean(k_ref[blk(b), :], axis=0, keepdims=True) for b in range(nb)]
                         + [jnp.zeros((16 - nb, HEAD_DIM), F32)], axis=0)
    for b in range(nb):
        vt[b] = v_ref[blk(b), :].T.astype(BF16)
    km_hi, km_lo = _split_bf16(km)
    gate = (lax.dot_general(km_hi, q_hi, NT_DIMS, preferred_element_type=F32)
            + lax.dot_general(km_lo, q_hi, NT_DIMS, preferred_element_type=F32)
            + lax.dot_general(km_hi, q_lo, NT_DIMS, preferred_element_type=F32))

    row = lax.broadcasted_iota(jnp.int32, gate.shape, 0)
    own = lax.broadcasted_iota(jnp.int32, gate.shape, 1) // MOBA_BLOCK
    rank = jnp.zeros(gate.shape, F32)
    for b2 in range(nb):
        gb = gate[b2:b2 + 1, :]
        beats = ((gb > gate) | ((gb == gate) & (b2 < row))) & (b2 < own)
        rank = rank + jnp.where(beats, 1.0, 0.0)
    pen_ref[...] = jnp.where((row < own) & (rank < float(MOBA_TOPK)), 0.0, NEG)

    key = lax.broadcasted_iota(jnp.int32, (MOBA_BLOCK, MOBA_BLOCK), 0)
    qry = lax.broadcasted_iota(jnp.int32, (MOBA_BLOCK, MOBA_BLOCK), 1)
    c31 = c31_ref[pl.program_id(1)]
    for qi in range(nb):
        m = None
        for kb in range(qi + 1):
            s = lax.dot_general(kbf[blk(kb), :], qbf[blk(qi), :], NT_DIMS, preferred_element_type=F32) * scale
            if kb == qi:
                s = jnp.where(key <= qry, s + bias_ref[0, 0], NEG)
            else:
                s = s + (bias_ref[0, 1] if kb == qi - 1 else c31) + pen_ref[kb:kb + 1, blk(qi)]
            s_ref[kb] = s
            cm = jnp.max(s, axis=0, keepdims=True)
            m = cm if m is None else jnp.maximum(m, cm)
        lsum = jnp.zeros((1, MOBA_BLOCK), F32)
        acc = jnp.zeros((HEAD_DIM, MOBA_BLOCK), F32)
        for kb in range(qi + 1):
            p = jnp.exp(s_ref[kb] - m)
            lsum = lsum + jnp.sum(p, axis=0, keepdims=True)
            acc = acc + jnp.dot(vt[kb], p.astype(BF16), preferred_element_type=F32)
        o_ref[blk(qi), :] = (acc / lsum).T.astype(o_ref.dtype)


def moba_prompt(q, k, v, bias_p, c31_h, n_seq, seq):
    nq = seq // MOBA_BLOCK
    tok = pl.BlockSpec((seq, HEAD_DIM), lambda n, h: (n, h))
    return pl.pallas_call(
        _moba_prompt_kernel,
        grid=(n_seq, N_HEADS),
        in_specs=[tok, tok, tok,
                  pl.BlockSpec((1, 2, MOBA_BLOCK, MOBA_BLOCK), lambda n, h: (h, 0, 0, 0)),
                  pl.BlockSpec(memory_space=pltpu.SMEM)],
        out_specs=tok,
        out_shape=jax.ShapeDtypeStruct(q.shape, BF16),
        scratch_shapes=[pltpu.VMEM((seq, HEAD_DIM), BF16), pltpu.VMEM((seq, HEAD_DIM), BF16),
                        pltpu.VMEM((nq, HEAD_DIM, MOBA_BLOCK), BF16), pltpu.VMEM((16, seq), F32),
                        pltpu.VMEM((nq, MOBA_BLOCK, MOBA_BLOCK), F32)],
        compiler_params=_cparams(2, 48),
        name="moba_prompt",
    )(q, k, v, bias_p, c31_h)


def _by_head_rows(x):
    return jnp.concatenate([x[:, h * HEAD_DIM:(h + 1) * HEAD_DIM] for h in range(x.shape[1] // HEAD_DIM)], axis=0)


def _by_head_lanes(x, n_heads):
    t = x.shape[0] // n_heads
    return jnp.concatenate([x[h * t:(h + 1) * t] for h in range(n_heads)], axis=1)


def _moba_sample_kernel(pt_ref, q_ref, kn_ref, vn_ref, tf_ref, tl_ref, to_ref, *refs, n_pages):
    kp, vp, o_ref, s_ref = refs[:n_pages], refs[n_pages:2 * n_pages], refs[2 * n_pages], refs[2 * n_pages + 1]
    ppb = MOBA_BLOCK // PAGE_SIZE
    nb = n_pages // ppb
    scale = HEAD_DIM ** -0.5
    t_new = q_ref.shape[1]
    nr = N_HEADS * t_new

    q = _by_head_rows(q_ref[0])
    q_bf = q.astype(BF16)
    zpad = jnp.zeros((128 - nr, HEAD_DIM), F32)
    kn = jnp.concatenate([_by_head_rows(kn_ref[0]), zpad], axis=0).astype(BF16)
    vn = jnp.concatenate([_by_head_rows(vn_ref[0]), zpad], axis=0).astype(BF16)

    gates = []
    for b in range(nb):
        ksum = jnp.zeros((N_HEADS, HEAD_DIM), F32)
        for pg in range(ppb):
            ksum = ksum + jnp.sum(kp[b * ppb + pg][...].reshape(PAGE_SIZE, N_HEADS, HEAD_DIM), axis=0)
        kmean = ksum / float(MOBA_BLOCK)
        krep = jnp.concatenate([jnp.broadcast_to(kmean[h:h + 1, :], (t_new, HEAD_DIM)) for h in range(N_HEADS)],
                               axis=0)
        gates.append(jnp.sum(q * krep, axis=1, keepdims=True))

    penalty = []
    for b in range(nb):
        rank = jnp.zeros((nr, 1), F32)
        for b2 in range(nb):
            if b2 != b:
                beats = (gates[b2] > gates[b]) | ((gates[b2] == gates[b]) & (b2 < b))
                rank = rank + jnp.where(beats, 1.0, 0.0)
        penalty.append(jnp.where(rank < float(MOBA_TOPK), 0.0, NEG))

    s_own = lax.dot_general(q_bf, kn, NT_DIMS, preferred_element_type=F32) * scale + to_ref[...]
    m = jnp.max(s_own, axis=-1, keepdims=True)
    for pg in range(n_pages):
        s = lax.dot_general(q_bf, kp[pg][...].astype(BF16), NT_DIMS, preferred_element_type=F32)
        bias = tl_ref[...] if pg == n_pages - 1 else tf_ref[...]
        s = s * scale + bias + penalty[pg // ppb]
        s_ref[pg] = s
        m = jnp.maximum(m, jnp.max(s, axis=-1, keepdims=True))
    p = jnp.exp(s_own - m)
    lsum = jnp.sum(p, axis=-1, keepdims=True)
    acc = jnp.dot(p.astype(BF16), vn, preferred_element_type=F32)
    for pg in range(n_pages):
        p = jnp.exp(s_ref[pg] - m)
        lsum = lsum + jnp.sum(p, axis=-1, keepdims=True)
        acc = acc + jnp.dot(p.astype(BF16), vp[pg][...].astype(BF16), preferred_element_type=F32)
    o_ref[0] = _by_head_lanes(acc / lsum, N_HEADS)


def moba_sample(q, kn, vn, ck, cv, page_table, t_far, t_last, t_own):
    nseq, n_pages = page_table.shape
    tok = pl.BlockSpec((1,) + q.shape[1:], lambda b, pt: (b, 0, 0))

    def page_spec(p):
        return pl.BlockSpec((None,) + ck.shape[1:], lambda b, pt, p=p: (pt[b, p], 0, 0))

    const = lambda a: pl.BlockSpec(a.shape, lambda b, pt: (0, 0))
    grid_spec = pltpu.PrefetchScalarGridSpec(
        num_scalar_prefetch=1, grid=(nseq,),
        in_specs=[tok, tok, tok, const(t_far), const(t_last), const(t_own)]
                 + [page_spec(p) for p in range(n_pages)] * 2,
        out_specs=tok,
        scratch_shapes=[pltpu.VMEM((n_pages, N_HEADS * q.shape[1], ck.shape[1]), F32)])
    return pl.pallas_call(
        functools.partial(_moba_sample_kernel, n_pages=n_pages),
        grid_spec=grid_spec,
        out_shape=jax.ShapeDtypeStruct(q.shape, F32),
        compiler_params=_cparams(1, 56),
        name="moba_sample",
    )(page_table, q, kn, vn, t_far, t_last, t_own, *([ck] * n_pages), *([cv] * n_pages))


HIST_ROWS = 32
HIST_OFF = HIST_ROWS - (CONV_K - 1)


def _conv_kernel(ga_ref, gg_ref, hist_ref, w_ref, b_ref, lg_ref, lb_ref, o_ref, nb_ref, ext_ref, y_ref,
                 *, tt, n_t, sb):
    t = pl.program_id(1)

    @pl.when(t == 0)
    def _():
        for s in range(sb):
            ext_ref[s, HIST_OFF:HIST_ROWS, :] = hist_ref[s]

    rt = min(tt, 128)
    for s in range(sb):
        rows = slice(s * tt, (s + 1) * tt)
        ext_ref[s, HIST_ROWS:HIST_ROWS + tt, :] = ga_ref[rows, :] * jax.nn.sigmoid(gg_ref[rows, :])
        for c in range(CONV_CH // 128):
            cs = slice(c * 128, (c + 1) * 128)
            for r0 in range(0, tt, rt):
                acc = jnp.zeros((rt, 128), F32)
                for res in range(8):
                    taps = [j for j in range(CONV_K) if (HIST_OFF + j) % 8 == res]
                    q0 = (HIST_OFF + taps[0]) // 8
                    q1 = (HIST_OFF + taps[-1]) // 8
                    if res:
                        win = ext_ref[s, r0 + 8 * q0:r0 + 8 * (q1 + 1) + rt, cs]
                        win = pltpu.roll(win, win.shape[0] - res, axis=0)
                    else:
                        win = ext_ref[s, r0 + 8 * q0:r0 + 8 * q1 + rt, cs]
                    for j in taps:
                        off = 8 * ((HIST_OFF + j) // 8 - q0)
                        acc = acc + w_ref[j:j + 1, cs] * win[off:off + rt]
                y_ref[s * tt + r0:s * tt + r0 + rt, cs] = acc + b_ref[:, cs]
    y = y_ref[...]
    mu = jnp.mean(y, axis=-1, keepdims=True)
    yc = y - mu
    var = jnp.mean(yc * yc, axis=-1, keepdims=True)
    yn = yc * lax.rsqrt(var + EPS) * lg_ref[...] + lb_ref[...]
    o_ref[...] = (yn * jax.nn.sigmoid(yn)).astype(o_ref.dtype)

    @pl.when(t == n_t - 1)
    def _():
        for s in range(sb):
            nb_ref[s] = ext_ref[s, tt + HIST_OFF:tt + HIST_ROWS, :]

    if n_t > 1:
        for s in range(sb):
            ext_ref[s, 0:HIST_ROWS, :] = ext_ref[s, tt:tt + HIST_ROWS, :]


def conformer_conv(ga, gg, hist, w_dw, b_dw, ln_g, ln_b, n_seq, seq, tt, sb):
    n_t = seq // tt
    assert sb == 1 or n_t == 1
    row = lambda a: a.reshape(1, CONV_CH)
    cvec = pl.BlockSpec((1, CONV_CH), lambda n, t: (0, 0))
    tile = pl.BlockSpec((sb * tt, CONV_CH), lambda n, t: (n * n_t + t, 0))
    hist_spec = pl.BlockSpec((sb, CONV_K - 1, CONV_CH), lambda n, t: (n, 0, 0))
    return pl.pallas_call(
        functools.partial(_conv_kernel, tt=tt, n_t=n_t, sb=sb),
        grid=(n_seq // sb, n_t),
        in_specs=[tile, tile, hist_spec, pl.BlockSpec((CONV_K, CONV_CH), lambda n, t: (0, 0)),
                  cvec, cvec, cvec],
        out_specs=[tile, hist_spec],
        out_shape=[jax.ShapeDtypeStruct((n_seq * seq, CONV_CH), BF16),
                   jax.ShapeDtypeStruct((n_seq, CONV_K - 1, CONV_CH), F32)],
        scratch_shapes=[pltpu.VMEM((sb, HIST_ROWS + tt, CONV_CH), F32), pltpu.VMEM((sb * tt, CONV_CH), F32)],
        compiler_params=_cparams(2, 32),
        name="conformer_conv",
    )(ga, gg, hist, w_dw, row(b_dw), row(ln_g), row(ln_b))


def _out_proj_kernel(x_ref, a_ref, c_ref, w_ref, o_ref):
    wa = a_ref.shape[1]
    y = jnp.dot(a_ref[...].astype(BF16), w_ref[0:wa, :], preferred_element_type=F32)
    y = y + jnp.dot(c_ref[...].astype(BF16), w_ref[wa:, :], preferred_element_type=F32)
    o_ref[...] = x_ref[...] + y


def out_proj(x, a, c, w_bf, tm):
    t, d = x.shape
    return pl.pallas_call(
        _out_proj_kernel,
        grid=(t // tm,),
        in_specs=[pl.BlockSpec((tm, d), lambda i: (i, 0)),
                  pl.BlockSpec((tm, a.shape[1]), lambda i: (i, 0)),
                  pl.BlockSpec((tm, c.shape[1]), lambda i: (i, 0)),
                  pl.BlockSpec(w_bf.shape, lambda i: (0, 0))],
        out_specs=pl.BlockSpec((tm, d), lambda i: (i, 0)),
        out_shape=jax.ShapeDtypeStruct((t, d), F32),
        compiler_params=_cparams(1, 48),
        name="out_proj",
    )(x, a, c, w_bf)


def _softmax_pv(s, mv_bf):
    m = jnp.max(s, axis=-1, keepdims=True)
    p = jnp.exp(s - m)
    o = jnp.dot(p.astype(BF16), mv_bf, preferred_element_type=F32)
    return o / jnp.sum(p, axis=-1, keepdims=True)


def _cross_kernel(h_ref, g_ref, wq_ref, wo_ref, mk_ref, mv_ref, o_ref, *, n_grp, rows):
    scale = HEAD_DIM ** -0.5
    h = h_ref[...]
    hn = _rms(h, g_ref[...]).astype(BF16)
    q = jnp.dot(hn, wq_ref[...], preferred_element_type=F32)
    outs = []
    for g in range(n_grp):
        mk = mk_ref[g].astype(BF16)
        mv = mv_ref[g].astype(BF16)
        qg = q[g * rows:(g + 1) * rows]
        if rows >= 128:
            heads = []
            for hd in range(X_HEADS):
                cs = slice(hd * HEAD_DIM, (hd + 1) * HEAD_DIM)
                s = lax.dot_general(qg[:, cs].astype(BF16), mk[:, cs], NT_DIMS, preferred_element_type=F32)
                heads.append(_softmax_pv(s * scale, mv[:, cs]))
            outs.append(jnp.concatenate(heads, axis=1))
        else:
            qx = _by_head_rows(qg).astype(BF16)
            s = lax.dot_general(qx, mk, NT_DIMS, preferred_element_type=F32) * scale
            rowh = lax.broadcasted_iota(jnp.int32, s.shape, 0) // rows
            colh = lax.broadcasted_iota(jnp.int32, s.shape, 1) % X_HEADS
            o = _softmax_pv(jnp.where(rowh == colh, s, NEG), mv)
            outs.append(_by_head_lanes(o, X_HEADS))
    o_all = outs[0] if n_grp == 1 else jnp.concatenate(outs, axis=0)
    o_ref[...] = h + jnp.dot(o_all.astype(BF16), wo_ref[...], preferred_element_type=F32)


def cross_block(h, g, wq_bf, wo_bf, mk, mv, n_grp, rows, tiles_per_seq):
    t, d = h.shape
    tm = n_grp * rows
    if n_grp == 1:
        mem_map = lambda i: (i // tiles_per_seq, 0, 0)
    else:
        mem_map = lambda i: (i, 0, 0)
    mem_spec = pl.BlockSpec((n_grp,) + mk.shape[1:], mem_map)
    return pl.pallas_call(
        functools.partial(_cross_kernel, n_grp=n_grp, rows=rows),
        grid=(t // tm,),
        in_specs=[pl.BlockSpec((tm, d), lambda i: (i, 0)),
                  pl.BlockSpec((1, d), lambda i: (0, 0)),
                  pl.BlockSpec(wq_bf.shape, lambda i: (0, 0)),
                  pl.BlockSpec(wo_bf.shape, lambda i: (0, 0)),
                  mem_spec, mem_spec],
        out_specs=pl.BlockSpec((tm, d), lambda i: (i, 0)),
        out_shape=jax.ShapeDtypeStruct((t, d), F32),
        compiler_params=_cparams(1, 48),
        name="cross_block",
    )(h, g.reshape(1, d), wq_bf, wo_bf, mk, mv)


def _top16(s):
    n, t = s.shape
    row = lax.broadcasted_iota(jnp.int32, (n, t), 0).astype(F32)
    row16 = lax.broadcasted_iota(jnp.int32, (PEER_TOPK, t), 0)
    rank = jnp.full((n, t), float(PEER_TOPK), F32)
    vals = jnp.zeros((PEER_TOPK, t), F32)
    for r in range(PEER_TOPK):
        m = jnp.max(s, axis=0, keepdims=True)
        idx = jnp.min(jnp.where(s == m, row, float(n)), axis=0, keepdims=True)
        hit = row == idx
        rank = jnp.where(hit, float(r), rank)
        s = jnp.where(hit, -jnp.inf, s)
        vals = jnp.where(row16 == r, m, vals)
    return vals, rank


def _pair_pieces(v1, v2, e1, e2):
    t = v1.shape[1]
    sub = lax.broadcasted_iota(jnp.int32, (8, t), 0)
    subf = sub.astype(F32)
    pieces = []

    def col(b, a0, a_max):
        a = sub + a0
        pieces.append(dict(c=v1[a0:a0 + 8] + v2[b:b + 1], e=e1[a0:a0 + 8] * e2[b:b + 1],
                           f=(subf + a0) * 16.0 + b, ok=a <= a_max, a0=a0, row_a=None))

    def rowp(a, b0, b_min, b_max):
        b = sub + b0
        pieces.append(dict(c=v1[a:a + 1] + v2[b0:b0 + 8], e=e1[a:a + 1] * e2[b0:b0 + 8],
                           f=a * 16.0 + (subf + b0), ok=(b >= b_min) & (b <= b_max), a0=None, row_a=a))

    col(0, 0, 15), col(0, 8, 15), col(1, 0, 7), col(2, 0, 4), col(3, 0, 3)
    rowp(0, 8, 8, 15), rowp(0, 0, 4, 7), rowp(1, 0, 4, 7), rowp(2, 0, 4, 4)
    for p in pieces:
        p["c"] = jnp.where(p["ok"], p["c"], -jnp.inf)
        p["f"] = jnp.where(p["ok"], p["f"], -1.0)
    return pieces


def _peer_route_kernel(h_ref, g_ref, wpq_ref, sk_ref, xnt_ref, cnt_ref, p1_ref, rk2_ref, e2_ref, q_scr):
    tq = h_ref.shape[0]
    hn = _rms(h_ref[...], g_ref[...])
    xnt_ref[...] = hn.T.astype(BF16)
    q = jnp.dot(hn.astype(BF16), wpq_ref[...], preferred_element_type=F32)
    for hh in range(PEER_HEADS):
        q_scr[hh] = q[:, hh * PEER_QDIM:(hh + 1) * PEER_QDIM].astype(BF16)

    def head(hh, carry):
        qh = q_scr[hh]
        half = PEER_QDIM // 2
        s1 = lax.dot_general(sk_ref[hh, 0], qh[:, :half], NT_DIMS, preferred_element_type=F32)
        s2 = lax.dot_general(sk_ref[hh, 1], qh[:, half:], NT_DIMS, preferred_element_type=F32)
        v1, rank1 = _top16(s1)
        v2, rank2 = _top16(s2)
        e1 = jnp.exp(v1 - v1[0:1])
        e2 = jnp.exp(v2 - v2[0:1])
        pieces = _pair_pieces(v1, v2, e1, e2)

        taken = [jnp.zeros((8, tq), F32) for _ in pieces]
        cs = [p["c"] for p in pieces]
        for _ in range(PEER_TOPK):
            m = cs[0]
            for c in cs[1:]:
                m = jnp.maximum(m, c)
            m = jnp.max(m, axis=0, keepdims=True)
            fm = None
            for c, p in zip(cs, pieces):
                cand = jnp.where(c == m, p["f"], 1e9)
                fm = cand if fm is None else jnp.minimum(fm, cand)
            fm = jnp.min(fm, axis=0, keepdims=True)
            for i, p in enumerate(pieces):
                hit = p["f"] == fm
                taken[i] = jnp.where(hit, 1.0, taken[i])
                cs[i] = jnp.where(hit, -jnp.inf, cs[i])

        row16 = lax.broadcasted_iota(jnp.int32, (PEER_TOPK, tq), 0)
        n1 = jnp.zeros((PEER_TOPK, tq), F32)
        z = jnp.zeros((1, tq), F32)
        for tf, p in zip(taken, pieces):
            z = z + jnp.sum(tf * p["e"], axis=0, keepdims=True)
            if p["row_a"] is None:
                pad = jnp.zeros((8, tq), F32)
                n1 = n1 + (jnp.concatenate([tf, pad], axis=0) if p["a0"] == 0
                           else jnp.concatenate([pad, tf], axis=0))
            else:
                n1 = n1 + jnp.where(row16 == p["row_a"], jnp.sum(tf, axis=0, keepdims=True), 0.0)

        cnt1 = jnp.zeros(rank1.shape, F32)
        for r in range(PEER_TOPK):
            cnt1 = jnp.where(rank1 == float(r), n1[r:r + 1], cnt1)
        cnt_ref[hh] = cnt1
        p1_ref[hh] = jnp.where(rank1 < float(PEER_TOPK), jnp.exp(s1 - v1[0:1]) / z, 0.0)
        rk2_ref[hh] = rank2.astype(BF16)
        e2_ref[hh] = jnp.where(rank2 < float(PEER_TOPK), jnp.exp(s2 - v2[0:1]), 0.0).astype(BF16)
        return carry

    lax.fori_loop(0, PEER_HEADS, head, 0)


def peer_route(h, g, wpq_bf, sk_bf, tq):
    t, d = h.shape
    tab = pl.BlockSpec((PEER_HEADS, PEER_NKEYS, tq), lambda i: (0, 0, i))
    tab_f32 = jax.ShapeDtypeStruct((PEER_HEADS, PEER_NKEYS, t), F32)
    tab_bf16 = jax.ShapeDtypeStruct((PEER_HEADS, PEER_NKEYS, t), BF16)
    return pl.pallas_call(
        _peer_route_kernel,
        grid=(t // tq,),
        in_specs=[pl.BlockSpec((tq, d), lambda i: (i, 0)),
                  pl.BlockSpec((1, d), lambda i: (0, 0)),
                  pl.BlockSpec(wpq_bf.shape, lambda i: (0, 0)),
                  pl.BlockSpec(sk_bf.shape, lambda i: (0, 0, 0, 0))],
        out_specs=[pl.BlockSpec((d, tq), lambda i: (0, i)), tab, tab, tab, tab],
        out_shape=[jax.ShapeDtypeStruct((d, t), BF16), tab_f32, tab_f32, tab_bf16, tab_bf16],
        scratch_shapes=[pltpu.VMEM((PEER_HEADS, tq, PEER_QDIM), BF16)],
        compiler_params=_cparams(1, 48),
        name="peer_route",
    )(h, g.reshape(1, d), wpq_bf, sk_bf)


def _peer_dense_kernel(xnt_ref, u_ref, vt_ref, cnt_ref, p1_ref, rk2_ref, e2_ref, h_ref, fg_ref, y_ref, acc_ref,
                       *, te, n_e):
    j = pl.program_id(1)

    @pl.when(j == 0)
    def _():
        acc_ref[...] = jnp.zeros(acc_ref.shape, F32)

    tq = xnt_ref.shape[1]
    act = jax.nn.gelu(jnp.dot(u_ref[...], xnt_ref[...], preferred_element_type=F32))
    groups = te // PEER_NKEYS
    parts = []
    for a in range(groups):
        i1 = j * groups + a
        gate = jnp.zeros((PEER_NKEYS, tq), BF16)
        for hh in range(PEER_HEADS):
            cnt_row = cnt_ref[hh, pl.ds(i1, 1), :].astype(BF16)
            p1_row = p1_ref[hh, pl.ds(i1, 1), :].astype(BF16)
            gate = gate + jnp.where(rk2_ref[hh] < cnt_row, p1_row * e2_ref[hh], jnp.zeros((), BF16))
        parts.append(act[a * PEER_NKEYS:(a + 1) * PEER_NKEYS].astype(BF16) * gate)
    hm_t = parts[0] if groups == 1 else jnp.concatenate(parts, axis=0)
    acc_ref[...] += jnp.dot(vt_ref[...], hm_t, preferred_element_type=F32)

    @pl.when(j == n_e - 1)
    def _():
        y_ref[...] = _rms(h_ref[...] + acc_ref[...].T, fg_ref[...])


def peer_dense(xnt, u_bf, vt_bf, tabs, h, final_g, tq, te):
    t, d = h.shape
    n_e = u_bf.shape[0] // te
    once = dict(pipeline_mode=pl.Buffered(1))
    tab = pl.BlockSpec((PEER_HEADS, PEER_NKEYS, tq), lambda i, j: (0, 0, i), **once)
    return pl.pallas_call(
        functools.partial(_peer_dense_kernel, te=te, n_e=n_e),
        grid=(t // tq, n_e),
        in_specs=[pl.BlockSpec((d, tq), lambda i, j: (0, i), **once),
                  pl.BlockSpec((te, d), lambda i, j: (j, 0)),
                  pl.BlockSpec((d, te), lambda i, j: (0, j)),
                  tab, tab, tab, tab,
                  pl.BlockSpec((tq, d), lambda i, j: (i, 0), **once),
                  pl.BlockSpec((1, d), lambda i, j: (0, 0))],
        out_specs=pl.BlockSpec((tq, d), lambda i, j: (i, 0)),
        out_shape=jax.ShapeDtypeStruct((t, d), F32),
        scratch_shapes=[pltpu.VMEM((d, tq), F32)],
        compiler_params=_cparams(2, 56),
        name="peer_dense",
    )(xnt, u_bf, vt_bf, *tabs, h, final_g.reshape(1, d))


def kernel(x_prompt, x_sample, mem_prompt, cache_k, cache_v, cache_conv, cache_mem_k, cache_mem_v, page_table,
           norm1_g, w_in, rel_bias, conv_w, conv_b, conv_ln_g, conv_ln_b, w_out, norm2_g, mem_norm_g, w_cq,
           w_mk, w_mv, w_co, norm3_g, w_pq, peer_sub_keys, peer_u, peer_v, final_g):
    depth = w_in.shape[0]
    assert depth == 1, "single-layer step"
    n_p, seq, d = x_prompt.shape
    n_s, t_s, _ = x_sample.shape
    l = 0
    bf = lambda a: a.astype(BF16)

    w_in_bf, w_out_bf = bf(w_in[l]), bf(w_out[l])
    w_cq_bf, w_co_bf, w_pq_bf = bf(w_cq[l]), bf(w_co[l]), bf(w_pq[l])
    w_mem_bf = bf(jnp.concatenate([w_mk[l], w_mv[l]], axis=1))
    sk_bf, u_bf, vt_bf = bf(peer_sub_keys[l]), bf(peer_u[l]), bf(peer_v[l].T)

    n_pool, n_pages = cache_k.shape[1], page_table.shape[1]
    bias_p, t_far, t_last, t_own = bias_tiles(rel_bias, n_pages * PAGE_SIZE)
    c31_h = rel_bias[NUM_BUCKETS - 1]

    def tail(h, mk, mv, n_grp, rows, tiles_per_seq):
        h = cross_block(h, norm2_g[l], w_cq_bf, w_co_bf, mk, mv, n_grp, rows, tiles_per_seq)
        xnt, *tabs = peer_route(h, norm3_g[l], w_pq_bf, sk_bf, tq=256)
        return peer_dense(xnt, u_bf, vt_bf, tabs, h, final_g, tq=512, te=1024)

    xp = x_prompt.reshape(n_p * seq, d)
    mk_p, mv_p = rms_matmul(mem_prompt.reshape(n_p * N_MEM, d), mem_norm_g[l], w_mem_bf, 2, tm=512)
    q_p, k_p, v_p, ga_p, gg_p = rms_matmul(xp, norm1_g[l], w_in_bf, 5, tm=512)
    attn_p = moba_prompt(q_p, k_p, v_p, bias_p, c31_h, n_p, seq)
    zero_buf = jnp.zeros((n_p, CONV_K - 1, CONV_CH), F32)
    conv_p, buf_p = conformer_conv(ga_p, gg_p, zero_buf, conv_w[l], conv_b[l], conv_ln_g[l], conv_ln_b[l],
                                   n_p, seq, tt=256, sb=1)
    h_p = out_proj(xp, attn_p, conv_p, w_out_bf, tm=512)
    y_p = tail(h_p, mk_p.reshape(n_p, N_MEM, X_WIDTH), mv_p.reshape(n_p, N_MEM, X_WIDTH), 1, 256, seq // 256)

    xs = x_sample.reshape(n_s * t_s, d)
    q_s, k_s, v_s, ga_s, gg_s = rms_matmul(xs, norm1_g[l], w_in_bf, 5, tm=512)
    tok3 = lambda a: a.reshape(n_s, t_s, ATTN_WIDTH)
    attn_s = moba_sample(tok3(q_s), tok3(k_s), tok3(v_s),
                         cache_k[l].reshape(n_pool, PAGE_SIZE * N_HEADS, HEAD_DIM),
                         cache_v[l].reshape(n_pool, PAGE_SIZE * N_HEADS, HEAD_DIM),
                         page_table, t_far, t_last, t_own)
    conv_s, buf_s = conformer_conv(ga_s, gg_s, cache_conv[l], conv_w[l], conv_b[l], conv_ln_g[l], conv_ln_b[l],
                                   n_s, t_s, tt=t_s, sb=8)
    h_s = out_proj(xs, attn_s.reshape(n_s * t_s, ATTN_WIDTH), conv_s, w_out_bf, tm=512)
    y_s = tail(h_s, cache_mem_k[l].reshape(n_s, N_MEM * X_HEADS, HEAD_DIM),
               cache_mem_v[l].reshape(n_s, N_MEM * X_HEADS, HEAD_DIM), 8, t_s, 1)

    n_pg = seq // PAGE_SIZE
    return (y_p.reshape(n_p, seq, d), y_s.reshape(n_s, t_s, d),
            k_p.reshape(1, n_p, n_pg, PAGE_SIZE, N_HEADS, HEAD_DIM),
            v_p.reshape(1, n_p, n_pg, PAGE_SIZE, N_HEADS, HEAD_DIM),
            buf_p[None],
            mk_p.reshape(1, n_p, N_MEM, X_HEADS, HEAD_DIM), mv_p.reshape(1, n_p, N_MEM, X_HEADS, HEAD_DIM),
            k_s.reshape(1, n_s, t_s, N_HEADS, HEAD_DIM), v_s.reshape(1, n_s, t_s, N_HEADS, HEAD_DIM),
            buf_s[None])
```

```python
import functools
import math

import numpy as np
import jax
import jax.numpy as jnp
from jax import lax
from jax.experimental import pallas as pl
from jax.experimental.pallas import tpu as pltpu

F32 = jnp.float32
BF16 = jnp.bfloat16

D_MODEL = 2048
HEAD_DIM = 128
N_HEADS = 8
ATTN_WIDTH = N_HEADS * HEAD_DIM
CONV_CH = 1024
MOBA_BLOCK = 256
MOBA_TOPK = 3
PAGE_SIZE = 128
NUM_BUCKETS = 32
MAX_EXACT = 16
REL_MAX_DIST = 128
CONV_K = 31
N_MEM = 256
X_HEADS = 4
X_WIDTH = X_HEADS * HEAD_DIM
PEER_HEADS = 8
PEER_NKEYS = 128
PEER_TOPK = 16
PEER_QDIM = 256
PEER_ROW_CHUNK = 512
EPS = 1e-6
NEG = -1e30

MIB = 1024 * 1024
NT_DIMS = (((1,), (1,)), ((), ()))


def _cparams(n_grid, vmem_mib):
    return pltpu.CompilerParams(dimension_semantics=("arbitrary",) * n_grid,
                                vmem_limit_bytes=vmem_mib * MIB)


def _rms(x, g):
    return x * lax.rsqrt(jnp.mean(x * x, axis=-1, keepdims=True) + EPS) * g


def _rms_matmul_kernel(x_ref, g_ref, w_ref, *outs):
    xn = _rms(x_ref[...], g_ref[...]).astype(BF16)
    tn = outs[0].shape[1]
    for s, o_ref in enumerate(outs):
        o_ref[...] = jnp.dot(xn, w_ref[:, s * tn:(s + 1) * tn], preferred_element_type=F32)


def rms_matmul(x, g, w_bf, n_out, tm):
    t, d = x.shape
    tn = w_bf.shape[1] // n_out
    return pl.pallas_call(
        _rms_matmul_kernel,
        grid=(t // tm,),
        in_specs=[pl.BlockSpec((tm, d), lambda i: (i, 0)),
                  pl.BlockSpec((1, d), lambda i: (0, 0)),
                  pl.BlockSpec(w_bf.shape, lambda i: (0, 0), pipeline_mode=pl.Buffered(1))],
        out_specs=[pl.BlockSpec((tm, tn), lambda i: (i, 0))] * n_out,
        out_shape=[jax.ShapeDtypeStruct((t, tn), F32)] * n_out,
        compiler_params=_cparams(1, 56),
        name="rms_matmul",
    )(x, g.reshape(1, d), w_bf)


def _bucket_np(rel):
    n = np.maximum(rel, 0)
    nf = np.maximum(n, 1).astype(np.float32)
    large = MAX_EXACT + (np.log(nf / MAX_EXACT) / np.float32(math.log(REL_MAX_DIST / MAX_EXACT))
                         * (NUM_BUCKETS - MAX_EXACT)).astype(np.int32)
    large = np.minimum(large, NUM_BUCKETS - 1)
    return np.where(n < MAX_EXACT, n, large).astype(np.int32)


def _bias_kernel(rb_ref, rbx_ref, bkp_ref, bkf_ref, bkl_ref, bko_ref, tp_ref, tf_ref, tl_ref, to_ref):
    def lookup(bk, table):
        acc = jnp.full(bk.shape, NEG, F32)
        for b in range(NUM_BUCKETS):
            acc = jnp.where(bk == b, table(b), acc)
        return acc

    for h in range(N_HEADS):
        for t in range(2):
            tp_ref[h, t] = lookup(bkp_ref[t], lambda b: rb_ref[b, h])
    by_row = lambda b: rbx_ref[b]
    tf_ref[...] = lookup(bkf_ref[...], by_row)
    tl_ref[...] = lookup(bkl_ref[...], by_row)
    to_ref[...] = lookup(bko_ref[...], by_row)


def bias_tiles(rel_bias, n_past):
    key = np.arange(MOBA_BLOCK)[:, None]
    qry = np.arange(MOBA_BLOCK)[None, :]
    bkp = np.stack([_bucket_np(qry - key), _bucket_np(MOBA_BLOCK + qry - key)])
    r = np.arange(N_HEADS * 8)[:, None]
    rh, rq = r // 8, r % 8
    c = np.arange(PAGE_SIZE * N_HEADS)[None, :]
    ct, ch = c // N_HEADS, c % N_HEADS
    bkf = np.where(rh == ch, NUM_BUCKETS - 1, -1)
    bkl = np.where(rh == ch, _bucket_np(n_past + rq - (n_past - PAGE_SIZE + ct)), -1)
    assert PAGE_SIZE >= REL_MAX_DIST
    co = np.arange(128)[None, :]
    coh, cot = co // 8, co % 8
    bko = np.where((rh == coh) & (cot <= rq), _bucket_np(rq - cot), -1)
    rbx = jnp.repeat(rel_bias, 8, axis=1)[:, :, None]
    vm = pl.BlockSpec(memory_space=pltpu.VMEM)
    i32 = lambda a: jnp.asarray(a.astype(np.int32))
    return pl.pallas_call(
        _bias_kernel,
        in_specs=[pl.BlockSpec(memory_space=pltpu.SMEM), vm, vm, vm, vm, vm],
        out_specs=[vm, vm, vm, vm],
        out_shape=[jax.ShapeDtypeStruct((N_HEADS, 2, MOBA_BLOCK, MOBA_BLOCK), F32),
                   jax.ShapeDtypeStruct(bkf.shape, F32),
                   jax.ShapeDtypeStruct(bkl.shape, F32),
                   jax.ShapeDtypeStruct(bko.shape, F32)],
        compiler_params=pltpu.CompilerParams(vmem_limit_bytes=32 * MIB),
        name="bias_tiles",
    )(rel_bias, rbx, i32(bkp), i32(bkf), i32(bkl), i32(bko))


def _split_bf16(x):
    hi = x.astype(BF16)
    lo = (x - hi.astype(F32)).astype(BF16)
    return hi, lo


def _moba_prompt_kernel(q_ref, k_ref, v_ref, bias_ref, c31_ref, o_ref, qbf, kbf, vt, pen_ref, s_ref):
    nb = kbf.shape[0] // MOBA_BLOCK
    scale = HEAD_DIM ** -0.5
    blk = lambda b: slice(b * MOBA_BLOCK, (b + 1) * MOBA_BLOCK)

    q = q_ref[...]
    q_hi, q_lo = _split_bf16(q)
    qbf[...] = q_hi
    kbf[...] = k_ref[...].astype(BF16)
    km = jnp.concatenate([jnp.mean(k_ref[blk(b), :], axis=0, keepdims=True) for b in range(nb)]
                         + [jnp.zeros((16 - nb, HEAD_DIM), F32)], axis=0)
    for b in range(nb):
        vt[b] = v_ref[blk(b), :].T.astype(BF16)
    km_hi, km_lo = _split_bf16(km)
    gate = (lax.dot_general(km_hi, q_hi, NT_DIMS, preferred_element_type=F32)
            + lax.dot_general(km_lo, q_hi, NT_DIMS, preferred_element_type=F32)
            + lax.dot_general(km_hi, q_lo, NT_DIMS, preferred_element_type=F32))

    row = lax.broadcasted_iota(jnp.int32, gate.shape, 0)
    own = lax.broadcasted_iota(jnp.int32, gate.shape, 1) // MOBA_BLOCK
    rank = jnp.zeros(gate.shape, F32)
    for b2 in range(nb):
        gb = gate[b2:b2 + 1, :]
        beats = ((gb > gate) | ((gb == gate) & (b2 < row))) & (b2 < own)
        rank = rank + jnp.where(beats, 1.0, 0.0)
    pen_ref[...] = jnp.where((row < own) & (rank < float(MOBA_TOPK)), 0.0, NEG)

    key = lax.broadcasted_iota(jnp.int32, (MOBA_BLOCK, MOBA_BLOCK), 0)
    qry = lax.broadcasted_iota(jnp.int32, (MOBA_BLOCK, MOBA_BLOCK), 1)
    c31 = c31_ref[pl.program_id(1)]
    for qi in range(nb):
        m = None
        for kb in range(qi + 1):
            s = lax.dot_general(kbf[blk(kb), :], qbf[blk(qi), :], NT_DIMS, preferred_element_type=F32) * scale
            if kb == qi:
                s = jnp.where(key <= qry, s + bias_ref[0, 0], NEG)
            else:
                s = s + (bias_ref[0, 1] if kb == qi - 1 else c31) + pen_ref[kb:kb + 1, blk(qi)]
            s_ref[kb] = s
            cm = jnp.max(s, axis=0, keepdims=True)
            m = cm if m is None else jnp.maximum(m, cm)
        lsum = jnp.zeros((1, MOBA_BLOCK), F32)
        acc = jnp.zeros((HEAD_DIM, MOBA_BLOCK), F32)
        for kb in range(qi + 1):
            p = jnp.exp(s_ref[kb] - m)
            lsum = lsum + jnp.sum(p, axis=0, keepdims=True)
            acc = acc + jnp.dot(vt[kb], p.astype(BF16), preferred_element_type=F32)
        o_ref[blk(qi), :] = (acc / lsum).T.astype(o_ref.dtype)


def moba_prompt(q, k, v, bias_p, c31_h, n_seq, seq):
    nq = seq // MOBA_BLOCK
    tok = pl.BlockSpec((seq, HEAD_DIM), lambda n, h: (n, h))
    return pl.pallas_call(
        _moba_prompt_kernel,
        grid=(n_seq, N_HEADS),
        in_specs=[tok, tok, tok,
                  pl.BlockSpec((1, 2, MOBA_BLOCK, MOBA_BLOCK), lambda n, h: (h, 0, 0, 0)),
                  pl.BlockSpec(memory_space=pltpu.SMEM)],
        out_specs=tok,
        out_shape=jax.ShapeDtypeStruct(q.shape, BF16),
        scratch_shapes=[pltpu.VMEM((seq, HEAD_DIM), BF16), pltpu.VMEM((seq, HEAD_DIM), BF16),
                        pltpu.VMEM((nq, HEAD_DIM, MOBA_BLOCK), BF16), pltpu.VMEM((16, seq), F32),
                        pltpu.VMEM((nq, MOBA_BLOCK, MOBA_BLOCK), F32)],
        compiler_params=_cparams(2, 48),
        name="moba_prompt",
    )(q, k, v, bias_p, c31_h)


def _by_head_rows(x):
    return jnp.concatenate([x[:, h * HEAD_DIM:(h + 1) * HEAD_DIM] for h in range(x.shape[1] // HEAD_DIM)], axis=0)


def _by_head_lanes(x, n_heads):
    t = x.shape[0] // n_heads
    return jnp.concatenate([x[h * t:(h + 1) * t] for h in range(n_heads)], axis=1)


def _moba_sample_kernel(pt_ref, q_ref, kn_ref, vn_ref, tf_ref, tl_ref, to_ref, *refs, n_pages):
    kp, vp, o_ref, s_ref = refs[:n_pages], refs[n_pages:2 * n_pages], refs[2 * n_pages], refs[2 * n_pages + 1]
    ppb = MOBA_BLOCK // PAGE_SIZE
    nb = n_pages // ppb
    scale = HEAD_DIM ** -0.5
    t_new = q_ref.shape[1]
    nr = N_HEADS * t_new

    q = _by_head_rows(q_ref[0])
    q_bf = q.astype(BF16)
    zpad = jnp.zeros((128 - nr, HEAD_DIM), F32)
    kn = jnp.concatenate([_by_head_rows(kn_ref[0]), zpad], axis=0).astype(BF16)
    vn = jnp.concatenate([_by_head_rows(vn_ref[0]), zpad], axis=0).astype(BF16)

    gates = []
    for b in range(nb):
        ksum = jnp.zeros((N_HEADS, HEAD_DIM), F32)
        for pg in range(ppb):
            ksum = ksum + jnp.sum(kp[b * ppb + pg][...].reshape(PAGE_SIZE, N_HEADS, HEAD_DIM), axis=0)
        kmean = ksum / float(MOBA_BLOCK)
        krep = jnp.concatenate([jnp.broadcast_to(kmean[h:h + 1, :], (t_new, HEAD_DIM)) for h in range(N_HEADS)],
                               axis=0)
        gates.append(jnp.sum(q * krep, axis=1, keepdims=True))

    penalty = []
    for b in range(nb):
        rank = jnp.zeros((nr, 1), F32)
        for b2 in range(nb):
            if b2 != b:
                beats = (gates[b2] > gates[b]) | ((gates[b2] == gates[b]) & (b2 < b))
                rank = rank + jnp.where(beats, 1.0, 0.0)
        penalty.append(jnp.where(rank < float(MOBA_TOPK), 0.0, NEG))

    s_own = lax.dot_general(q_bf, kn, NT_DIMS, preferred_element_type=F32) * scale + to_ref[...]
    m = jnp.max(s_own, axis=-1, keepdims=True)
    for pg in range(n_pages):
        s = lax.dot_general(q_bf, kp[pg][...].astype(BF16), NT_DIMS, preferred_element_type=F32)
        bias = tl_ref[...] if pg == n_pages - 1 else tf_ref[...]
        s = s * scale + bias + penalty[pg // ppb]
        s_ref[pg] = s
        m = jnp.maximum(m, jnp.max(s, axis=-1, keepdims=True))
    p = jnp.exp(s_own - m)
    lsum = jnp.sum(p, axis=-1, keepdims=True)
    acc = jnp.dot(p.astype(BF16), vn, preferred_element_type=F32)
    for pg in range(n_pages):
        p = jnp.exp(s_ref[pg] - m)
        lsum = lsum + jnp.sum(p, axis=-1, keepdims=True)
        acc = acc + jnp.dot(p.astype(BF16), vp[pg][...].astype(BF16), preferred_element_type=F32)
    o_ref[0] = _by_head_lanes(acc / lsum, N_HEADS)


def moba_sample(q, kn, vn, ck, cv, page_table, t_far, t_last, t_own):
    nseq, n_pages = page_table.shape
    tok = pl.BlockSpec((1,) + q.shape[1:], lambda b, pt: (b, 0, 0))

    def page_spec(p):
        return pl.BlockSpec((None,) + ck.shape[1:], lambda b, pt, p=p: (pt[b, p], 0, 0))

    const = lambda a: pl.BlockSpec(a.shape, lambda b, pt: (0, 0))
    grid_spec = pltpu.PrefetchScalarGridSpec(
        num_scalar_prefetch=1, grid=(nseq,),
        in_specs=[tok, tok, tok, const(t_far), const(t_last), const(t_own)]
                 + [page_spec(p) for p in range(n_pages)] * 2,
        out_specs=tok,
        scratch_shapes=[pltpu.VMEM((n_pages, N_HEADS * q.shape[1], ck.shape[1]), F32)])
    return pl.pallas_call(
        functools.partial(_moba_sample_kernel, n_pages=n_pages),
        grid_spec=grid_spec,
        out_shape=jax.ShapeDtypeStruct(q.shape, F32),
        compiler_params=_cparams(1, 56),
        name="moba_sample",
    )(page_table, q, kn, vn, t_far, t_last, t_own, *([ck] * n_pages), *([cv] * n_pages))


HIST_ROWS = 32
HIST_OFF = HIST_ROWS - (CONV_K - 1)


def _conv_kernel(ga_ref, gg_ref, hist_ref, w_ref, b_ref, lg_ref, lb_ref, o_ref, nb_ref, ext_ref, y_ref,
                 *, tt, n_t, sb):
    t = pl.program_id(1)

    @pl.when(t == 0)
    def _():
        for s in range(sb):
            ext_ref[s, HIST_OFF:HIST_ROWS, :] = hist_ref[s]

    rt = min(tt, 128)
    for s in range(sb):
        rows = slice(s * tt, (s + 1) * tt)
        ext_ref[s, HIST_ROWS:HIST_ROWS + tt, :] = ga_ref[rows, :] * jax.nn.sigmoid(gg_ref[rows, :])
        for c in range(CONV_CH // 128):
            cs = slice(c * 128, (c + 1) * 128)
            for r0 in range(0, tt, rt):
                acc = jnp.zeros((rt, 128), F32)
                for res in range(8):
                    taps = [j for j in range(CONV_K) if (HIST_OFF + j) % 8 == res]
                    q0 = (HIST_OFF + taps[0]) // 8
                    q1 = (HIST_OFF + taps[-1]) // 8
                    if res:
                        win = ext_ref[s, r0 + 8 * q0:r0 + 8 * (q1 + 1) + rt, cs]
                        win = pltpu.roll(win, win.shape[0] - res, axis=0)
                    else:
                        win = ext_ref[s, r0 + 8 * q0:r0 + 8 * q1 + rt, cs]
                    for j in taps:
                        off = 8 * ((HIST_OFF + j) // 8 - q0)
                        acc = acc + w_ref[j:j + 1, cs] * win[off:off + rt]
                y_ref[s * tt + r0:s * tt + r0 + rt, cs] = acc + b_ref[:, cs]
    y = y_ref[...]
    mu = jnp.mean(y, axis=-1, keepdims=True)
    yc = y - mu
    var = jnp.mean(yc * yc, axis=-1, keepdims=True)
    yn = yc * lax.rsqrt(var + EPS) * lg_ref[...] + lb_ref[...]
    o_ref[...] = (yn * jax.nn.sigmoid(yn)).astype(o_ref.dtype)

    @pl.when(t == n_t - 1)
    def _():
        for s in range(sb):
            nb_ref[s] = ext_ref[s, tt + HIST_OFF:tt + HIST_ROWS, :]

    if n_t > 1:
        for s in range(sb):
            ext_ref[s, 0:HIST_ROWS, :] = ext_ref[s, tt:tt + HIST_ROWS, :]


def conformer_conv(ga, gg, hist, w_dw, b_dw, ln_g, ln_b, n_seq, seq, tt, sb):
    n_t = seq // tt
    assert sb == 1 or n_t == 1
    row = lambda a: a.reshape(1, CONV_CH)
    cvec = pl.BlockSpec((1, CONV_CH), lambda n, t: (0, 0))
    tile = pl.BlockSpec((sb * tt, CONV_CH), lambda n, t: (n * n_t + t, 0))
    hist_spec = pl.BlockSpec((sb, CONV_K - 1, CONV_CH), lambda n, t: (n, 0, 0))
    return pl.pallas_call(
        functools.partial(_conv_kernel, tt=tt, n_t=n_t, sb=sb),
        grid=(n_seq // sb, n_t),
        in_specs=[tile, tile, hist_spec, pl.BlockSpec((CONV_K, CONV_CH), lambda n, t: (0, 0)),
                  cvec, cvec, cvec],
        out_specs=[tile, hist_spec],
        out_shape=[jax.ShapeDtypeStruct((n_seq * seq, CONV_CH), BF16),
                   jax.ShapeDtypeStruct((n_seq, CONV_K - 1, CONV_CH), F32)],
        scratch_shapes=[pltpu.VMEM((sb, HIST_ROWS + tt, CONV_CH), F32), pltpu.VMEM((sb * tt, CONV_CH), F32)],
        compiler_params=_cparams(2, 32),
        name="conformer_conv",
    )(ga, gg, hist, w_dw, row(b_dw), row(ln_g), row(ln_b))


def _out_proj_kernel(x_ref, a_ref, c_ref, w_ref, o_ref):
    wa = a_ref.shape[1]
    y = jnp.dot(a_ref[...].astype(BF16), w_ref[0:wa, :], preferred_element_type=F32)
    y = y + jnp.dot(c_ref[...].astype(BF16), w_ref[wa:, :], preferred_element_type=F32)
    o_ref[...] = x_ref[...] + y


def out_proj(x, a, c, w_bf, tm):
    t, d = x.shape
    return pl.pallas_call(
        _out_proj_kernel,
        grid=(t // tm,),
        in_specs=[pl.BlockSpec((tm, d), lambda i: (i, 0)),
                  pl.BlockSpec((tm, a.shape[1]), lambda i: (i, 0)),
                  pl.BlockSpec((tm, c.shape[1]), lambda i: (i, 0)),
                  pl.BlockSpec(w_bf.shape, lambda i: (0, 0))],
        out_specs=pl.BlockSpec((tm, d), lambda i: (i, 0)),
        out_shape=jax.ShapeDtypeStruct((t, d), F32),
        compiler_params=_cparams(1, 48),
        name="out_proj",
    )(x, a, c, w_bf)


def _softmax_pv(s, mv_bf):
    m = jnp.max(s, axis=-1, keepdims=True)
    p = jnp.exp(s - m)
    o = jnp.dot(p.astype(BF16), mv_bf, preferred_element_type=F32)
    return o / jnp.sum(p, axis=-1, keepdims=True)


def _cross_kernel(h_ref, g_ref, wq_ref, wo_ref, mk_ref, mv_ref, o_ref, *, n_grp, rows):
    scale = HEAD_DIM ** -0.5
    h = h_ref[...]
    hn = _rms(h, g_ref[...]).astype(BF16)
    q = jnp.dot(hn, wq_ref[...], preferred_element_type=F32)
    outs = []
    for g in range(n_grp):
        mk = mk_ref[g].astype(BF16)
        mv = mv_ref[g].astype(BF16)
        qg = q[g * rows:(g + 1) * rows]
        if rows >= 128:
            heads = []
            for hd in range(X_HEADS):
                cs = slice(hd * HEAD_DIM, (hd + 1) * HEAD_DIM)
                s = lax.dot_general(qg[:, cs].astype(BF16), mk[:, cs], NT_DIMS, preferred_element_type=F32)
                heads.append(_softmax_pv(s * scale, mv[:, cs]))
            outs.append(jnp.concatenate(heads, axis=1))
        else:
            qx = _by_head_rows(qg).astype(BF16)
            s = lax.dot_general(qx, mk, NT_DIMS, preferred_element_type=F32) * scale
            rowh = lax.broadcasted_iota(jnp.int32, s.shape, 0) // rows
            colh = lax.broadcasted_iota(jnp.int32, s.shape, 1) % X_HEADS
            o = _softmax_pv(jnp.where(rowh == colh, s, NEG), mv)
            outs.append(_by_head_lanes(o, X_HEADS))
    o_all = outs[0] if n_grp == 1 else jnp.concatenate(outs, axis=0)
    o_ref[...] = h + jnp.dot(o_all.astype(BF16), wo_ref[...], preferred_element_type=F32)


def cross_block(h, g, wq_bf, wo_bf, mk, mv, n_grp, rows, tiles_per_seq):
    t, d = h.shape
    tm = n_grp * rows
    if n_grp == 1:
        mem_map = lambda i: (i // tiles_per_seq, 0, 0)
    else:
        mem_map = lambda i: (i, 0, 0)
    mem_spec = pl.BlockSpec((n_grp,) + mk.shape[1:], mem_map)
    return pl.pallas_call(
        functools.partial(_cross_kernel, n_grp=n_grp, rows=rows),
        grid=(t // tm,),
        in_specs=[pl.BlockSpec((tm, d), lambda i: (i, 0)),
                  pl.BlockSpec((1, d), lambda i: (0, 0)),
                  pl.BlockSpec(wq_bf.shape, lambda i: (0, 0)),
                  pl.BlockSpec(wo_bf.shape, lambda i: (0, 0)),
                  mem_spec, mem_spec],
        out_specs=pl.BlockSpec((tm, d), lambda i: (i, 0)),
        out_shape=jax.ShapeDtypeStruct((t, d), F32),
        compiler_params=_cparams(1, 48),
        name="cross_block",
    )(h, g.reshape(1, d), wq_bf, wo_bf, mk, mv)


def _top16(s, exact_ties):
    n, t = s.shape
    row = lax.broadcasted_iota(jnp.int32, (n, t), 0).astype(F32)
    row16 = lax.broadcasted_iota(jnp.int32, (PEER_TOPK, t), 0)
    rank = jnp.full((n, t), float(PEER_TOPK), F32)
    vals = jnp.zeros((PEER_TOPK, t), F32)
    for r in range(PEER_TOPK):
        m = jnp.max(s, axis=0, keepdims=True)
        if exact_ties:
            idx = jnp.min(jnp.where(s == m, row, float(n)), axis=0, keepdims=True)
            hit = row == idx
        else:
            hit = s == m
        rank = jnp.where(hit, float(r), rank)
        s = jnp.where(hit, -jnp.inf, s)
        vals = jnp.where(row16 == r, m, vals)
    return vals, rank


def _pair_pieces(v1, v2, e1, e2):
    t = v1.shape[1]
    sub = lax.broadcasted_iota(jnp.int32, (8, t), 0)
    subf = sub.astype(F32)
    pieces = []

    def col(b, a0, a_max):
        a = sub + a0
        pieces.append(dict(c=v1[a0:a0 + 8] + v2[b:b + 1], e=e1[a0:a0 + 8] * e2[b:b + 1],
                           f=(subf + a0) * 16.0 + b, ok=a <= a_max, a0=a0, row_a=None))

    def rowp(a, b0, b_min, b_max):
        b = sub + b0
        pieces.append(dict(c=v1[a:a + 1] + v2[b0:b0 + 8], e=e1[a:a + 1] * e2[b0:b0 + 8],
                           f=a * 16.0 + (subf + b0), ok=(b >= b_min) & (b <= b_max), a0=None, row_a=a))

    col(0, 0, 15), col(0, 8, 15), col(1, 0, 7), col(2, 0, 4), col(3, 0, 3)
    rowp(0, 8, 8, 15), rowp(0, 0, 4, 7), rowp(1, 0, 4, 7), rowp(2, 0, 4, 4)
    for p in pieces:
        p["c"] = jnp.where(p["ok"], p["c"], -jnp.inf)
        p["f"] = jnp.where(p["ok"], p["f"], -1.0)
    return pieces


def _route_head(s1, s2, exact_ties):
    tq = s1.shape[1]
    v1, rank1 = _top16(s1, exact_ties)
    v2, rank2 = _top16(s2, exact_ties)
    e1 = jnp.exp(v1 - v1[0:1])
    e2 = jnp.exp(v2 - v2[0:1])
    pieces = _pair_pieces(v1, v2, e1, e2)

    taken = [jnp.zeros((8, tq), F32) for _ in pieces]
    cs = [p["c"] for p in pieces]
    for _ in range(PEER_TOPK):
        m = cs[0]
        for c in cs[1:]:
            m = jnp.maximum(m, c)
        m = jnp.max(m, axis=0, keepdims=True)
        if exact_ties:
            fm = None
            for c, p in zip(cs, pieces):
                cand = jnp.where(c == m, p["f"], 1e9)
                fm = cand if fm is None else jnp.minimum(fm, cand)
            fm = jnp.min(fm, axis=0, keepdims=True)
        for i, p in enumerate(pieces):
            hit = (p["f"] == fm) if exact_ties else (cs[i] == m)
            taken[i] = jnp.where(hit, 1.0, taken[i])
            cs[i] = jnp.where(hit, -jnp.inf, cs[i])

    row16 = lax.broadcasted_iota(jnp.int32, (PEER_TOPK, tq), 0)
    n1 = jnp.zeros((PEER_TOPK, tq), F32)
    z = jnp.zeros((1, tq), F32)
    for tf, p in zip(taken, pieces):
        z = z + jnp.sum(tf * p["e"], axis=0, keepdims=True)
        if p["row_a"] is None:
            pad = jnp.zeros((8, tq), F32)
            n1 = n1 + (jnp.concatenate([tf, pad], axis=0) if p["a0"] == 0
                       else jnp.concatenate([pad, tf], axis=0))
        else:
            n1 = n1 + jnp.where(row16 == p["row_a"], jnp.sum(tf, axis=0, keepdims=True), 0.0)

    in1 = rank1 < float(PEER_TOPK)
    in2 = rank2 < float(PEER_TOPK)
    cnt1 = jnp.zeros(rank1.shape, F32)
    for r in range(PEER_TOPK):
        cnt1 = jnp.where(rank1 == float(r), n1[r:r + 1], cnt1)
    p1 = jnp.where(in1, jnp.exp(s1 - v1[0:1]) / z, 0.0)
    e2_dense = jnp.where(in2, jnp.exp(s2 - v2[0:1]), 0.0)
    winners = (jnp.sum(jnp.where(in1, 1.0, 0.0), axis=0, keepdims=True)
               + jnp.sum(jnp.where(in2, 1.0, 0.0), axis=0, keepdims=True)
               + jnp.sum(n1, axis=0, keepdims=True))
    return cnt1, p1, rank2, e2_dense, winners


def _peer_route_kernel(h_ref, g_ref, wpq_ref, sk_ref, xnt_ref, cnt_ref, p1_ref, rk2_ref, e2_ref, q_scr):
    hn = _rms(h_ref[...], g_ref[...])
    xnt_ref[...] = hn.T.astype(BF16)
    q = jnp.dot(hn.astype(BF16), wpq_ref[...], preferred_element_type=F32)
    for hh in range(PEER_HEADS):
        q_scr[hh] = q[:, hh * PEER_QDIM:(hh + 1) * PEER_QDIM].astype(BF16)

    def head(hh, carry):
        qh = q_scr[hh]
        half = PEER_QDIM // 2
        s1 = lax.dot_general(sk_ref[hh, 0], qh[:, :half], NT_DIMS, preferred_element_type=F32)
        s2 = lax.dot_general(sk_ref[hh, 1], qh[:, half:], NT_DIMS, preferred_element_type=F32)

        def emit(exact_ties):
            cnt1, p1, rank2, e2_dense, winners = _route_head(s1, s2, exact_ties)
            cnt_ref[hh] = cnt1
            p1_ref[hh] = p1
            rk2_ref[hh] = rank2.astype(BF16)
            e2_ref[hh] = e2_dense.astype(BF16)
            return winners

        winners = emit(False)
        tie = jnp.max(winners) > float(3 * PEER_TOPK)

        @pl.when(tie)
        def _():
            emit(True)

        return carry

    lax.fori_loop(0, PEER_HEADS, head, 0)


def peer_route(h, g, wpq_bf, sk_bf, tq):
    t, d = h.shape
    tab = pl.BlockSpec((PEER_HEADS, PEER_NKEYS, tq), lambda i: (0, 0, i))
    tab_f32 = jax.ShapeDtypeStruct((PEER_HEADS, PEER_NKEYS, t), F32)
    tab_bf16 = jax.ShapeDtypeStruct((PEER_HEADS, PEER_NKEYS, t), BF16)
    return pl.pallas_call(
        _peer_route_kernel,
        grid=(t // tq,),
        in_specs=[pl.BlockSpec((tq, d), lambda i: (i, 0)),
                  pl.BlockSpec((1, d), lambda i: (0, 0)),
                  pl.BlockSpec(wpq_bf.shape, lambda i: (0, 0)),
                  pl.BlockSpec(sk_bf.shape, lambda i: (0, 0, 0, 0))],
        out_specs=[pl.BlockSpec((d, tq), lambda i: (0, i)), tab, tab, tab, tab],
        out_shape=[jax.ShapeDtypeStruct((d, t), BF16), tab_f32, tab_f32, tab_bf16, tab_bf16],
        scratch_shapes=[pltpu.VMEM((PEER_HEADS, tq, PEER_QDIM), BF16)],
        compiler_params=_cparams(1, 48),
        name="peer_route",
    )(h, g.reshape(1, d), wpq_bf, sk_bf)


def _peer_dense_kernel(xnt_ref, u_ref, vt_ref, cnt_ref, p1_ref, rk2_ref, e2_ref, h_ref, fg_ref, y_ref, acc_ref,
                       *, te, n_e):
    j = pl.program_id(1)

    @pl.when(j == 0)
    def _():
        acc_ref[...] = jnp.zeros(acc_ref.shape, F32)

    tq = xnt_ref.shape[1]
    groups = te // PEER_NKEYS
    gpc = PEER_ROW_CHUNK // PEER_NKEYS
    parts = []
    for a in range(groups):
        if a % gpc == 0:
            rows = slice(a * PEER_NKEYS, (a + gpc) * PEER_NKEYS)
            act = jax.nn.gelu(jnp.dot(u_ref[rows, :], xnt_ref[...], preferred_element_type=F32))
        i1 = j * groups + a
        gate = jnp.zeros((PEER_NKEYS, tq), BF16)
        for hh in range(PEER_HEADS):
            cnt_row = cnt_ref[hh, pl.ds(i1, 1), :].astype(BF16)
            p1_row = p1_ref[hh, pl.ds(i1, 1), :].astype(BF16)
            gate = gate + jnp.where(rk2_ref[hh] < cnt_row, p1_row * e2_ref[hh], jnp.zeros((), BF16))
        lo = (a % gpc) * PEER_NKEYS
        parts.append(act[lo:lo + PEER_NKEYS].astype(BF16) * gate)
    hm_t = parts[0] if groups == 1 else jnp.concatenate(parts, axis=0)
    acc_ref[...] += jnp.dot(vt_ref[...], hm_t, preferred_element_type=F32)

    @pl.when(j == n_e - 1)
    def _():
        y_ref[...] = _rms(h_ref[...] + acc_ref[...].T, fg_ref[...])


def peer_dense(xnt, u_bf, vt_bf, tabs, h, final_g, tq, te):
    t, d = h.shape
    n_e = u_bf.shape[0] // te
    once = dict(pipeline_mode=pl.Buffered(1))
    tab = pl.BlockSpec((PEER_HEADS, PEER_NKEYS, tq), lambda i, j: (0, 0, i), **once)
    return pl.pallas_call(
        functools.partial(_peer_dense_kernel, te=te, n_e=n_e),
        grid=(t // tq, n_e),
        in_specs=[pl.BlockSpec((d, tq), lambda i, j: (0, i), **once),
                  pl.BlockSpec((te, d), lambda i, j: (j, 0)),
                  pl.BlockSpec((d, te), lambda i, j: (0, j)),
                  tab, tab, tab, tab,
                  pl.BlockSpec((tq, d), lambda i, j: (i, 0), **once),
                  pl.BlockSpec((1, d), lambda i, j: (0, 0))],
        out_specs=pl.BlockSpec((tq, d), lambda i, j: (i, 0)),
        out_shape=jax.ShapeDtypeStruct((t, d), F32),
        scratch_shapes=[pltpu.VMEM((d, tq), F32)],
        compiler_params=_cparams(2, 56),
        name="peer_dense",
    )(xnt, u_bf, vt_bf, *tabs, h, final_g.reshape(1, d))


def kernel(x_prompt, x_sample, mem_prompt, cache_k, cache_v, cache_conv, cache_mem_k, cache_mem_v, page_table,
           norm1_g, w_in, rel_bias, conv_w, conv_b, conv_ln_g, conv_ln_b, w_out, norm2_g, mem_norm_g, w_cq,
           w_mk, w_mv, w_co, norm3_g, w_pq, peer_sub_keys, peer_u, peer_v, final_g):
    depth = w_in.shape[0]
    assert depth == 1, "single-layer step"
    n_p, seq, d = x_prompt.shape
    n_s, t_s, _ = x_sample.shape
    l = 0
    bf = lambda a: a.astype(BF16)

    w_in_bf, w_out_bf = bf(w_in[l]), bf(w_out[l])
    w_cq_bf, w_co_bf, w_pq_bf = bf(w_cq[l]), bf(w_co[l]), bf(w_pq[l])
    w_mem_bf = bf(jnp.concatenate([w_mk[l], w_mv[l]], axis=1))
    sk_bf, u_bf, vt_bf = bf(peer_sub_keys[l]), bf(peer_u[l]), bf(peer_v[l].T)

    n_pool, n_pages = cache_k.shape[1], page_table.shape[1]
    bias_p, t_far, t_last, t_own = bias_tiles(rel_bias, n_pages * PAGE_SIZE)
    c31_h = rel_bias[NUM_BUCKETS - 1]

    def tail(h, mk, mv, n_grp, rows, tiles_per_seq):
        h = cross_block(h, norm2_g[l], w_cq_bf, w_co_bf, mk, mv, n_grp, rows, tiles_per_seq)
        xnt, *tabs = peer_route(h, norm3_g[l], w_pq_bf, sk_bf, tq=256)
        return peer_dense(xnt, u_bf, vt_bf, tabs, h, final_g, tq=512, te=1024)

    xp = x_prompt.reshape(n_p * seq, d)
    mk_p, mv_p = rms_matmul(mem_prompt.reshape(n_p * N_MEM, d), mem_norm_g[l], w_mem_bf, 2, tm=512)
    q_p, k_p, v_p, ga_p, gg_p = rms_matmul(xp, norm1_g[l], w_in_bf, 5, tm=256)
    attn_p = moba_prompt(q_p, k_p, v_p, bias_p, c31_h, n_p, seq)
    zero_buf = jnp.zeros((n_p, CONV_K - 1, CONV_CH), F32)
    conv_p, buf_p = conformer_conv(ga_p, gg_p, zero_buf, conv_w[l], conv_b[l], conv_ln_g[l], conv_ln_b[l],
                                   n_p, seq, tt=256, sb=1)
    h_p = out_proj(xp, attn_p, conv_p, w_out_bf, tm=512)
    y_p = tail(h_p, mk_p.reshape(n_p, N_MEM, X_WIDTH), mv_p.reshape(n_p, N_MEM, X_WIDTH), 1, 256, seq // 256)

    xs = x_sample.reshape(n_s * t_s, d)
    q_s, k_s, v_s, ga_s, gg_s = rms_matmul(xs, norm1_g[l], w_in_bf, 5, tm=256)
    tok3 = lambda a: a.reshape(n_s, t_s, ATTN_WIDTH)
    attn_s = moba_sample(tok3(q_s), tok3(k_s), tok3(v_s),
                         cache_k[l].reshape(n_pool, PAGE_SIZE * N_HEADS, HEAD_DIM),
                         cache_v[l].reshape(n_pool, PAGE_SIZE * N_HEADS, HEAD_DIM),
                         page_table, t_far, t_last, t_own)
    conv_s, buf_s = conformer_conv(ga_s, gg_s, cache_conv[l], conv_w[l], conv_b[l], conv_ln_g[l], conv_ln_b[l],
                                   n_s, t_s, tt=t_s, sb=8)
    h_s = out_proj(xs, attn_s.reshape(n_s * t_s, ATTN_WIDTH), conv_s, w_out_bf, tm=512)
    y_s = tail(h_s, cache_mem_k[l].reshape(n_s, N_MEM * X_HEADS, HEAD_DIM),
               cache_mem_v[l].reshape(n_s, N_MEM * X_HEADS, HEAD_DIM), 8, t_s, 1)

    n_pg = seq // PAGE_SIZE
    return (y_p.reshape(n_p, seq, d), y_s.reshape(n_s, t_s, d),
            k_p.reshape(1, n_p, n_pg, PAGE_SIZE, N_HEADS, HEAD_DIM),
            v_p.reshape(1, n_p, n_pg, PAGE_SIZE, N_HEADS, HEAD_DIM),
            buf_p[None],
            mk_p.reshape(1, n_p, N_MEM, X_HEADS, HEAD_DIM), mv_p.reshape(1, n_p, N_MEM, X_HEADS, HEAD_DIM),
            k_s.reshape(1, n_s, t_s, N_HEADS, HEAD_DIM), v_s.reshape(1, n_s, t_s, N_HEADS, HEAD_DIM),
            buf_s[None])
```

```python
import functools
import math

import numpy as np
import jax
import jax.numpy as jnp
from jax import lax
from jax.experimental import pallas as pl
from jax.experimental.pallas import tpu as pltpu

F32 = jnp.float32
BF16 = jnp.bfloat16

D_MODEL = 2048
HEAD_DIM = 128
N_HEADS = 8
ATTN_WIDTH = N_HEADS * HEAD_DIM
CONV_CH = 1024
MOBA_BLOCK = 256
MOBA_TOPK = 3
PAGE_SIZE = 128
NUM_BUCKETS = 32
MAX_EXACT = 16
REL_MAX_DIST = 128
CONV_K = 31
N_MEM = 256
X_HEADS = 4
X_WIDTH = X_HEADS * HEAD_DIM
PEER_HEADS = 8
PEER_NKEYS = 128
PEER_TOPK = 16
PEER_QDIM = 256
PEER_ROW_CHUNK = 512
ROUTE_TQ = 256
EPS = 1e-6
NEG = -1e30

MIB = 1024 * 1024
NT_DIMS = (((1,), (1,)), ((), ()))


def _cparams(n_grid, vmem_mib):
    return pltpu.CompilerParams(dimension_semantics=("arbitrary",) * n_grid,
                                vmem_limit_bytes=vmem_mib * MIB)


def _rms(x, g):
    return x * lax.rsqrt(jnp.mean(x * x, axis=-1, keepdims=True) + EPS) * g


def _rms_matmul_kernel(x_ref, g_ref, w_ref, *outs):
    xn = _rms(x_ref[...], g_ref[...]).astype(BF16)
    tn = outs[0].shape[1]
    for s, o_ref in enumerate(outs):
        o_ref[...] = jnp.dot(xn, w_ref[:, s * tn:(s + 1) * tn], preferred_element_type=F32)


def rms_matmul(x, g, w_bf, n_out, tm):
    t, d = x.shape
    tn = w_bf.shape[1] // n_out
    return pl.pallas_call(
        _rms_matmul_kernel,
        grid=(t // tm,),
        in_specs=[pl.BlockSpec((tm, d), lambda i: (i, 0)),
                  pl.BlockSpec((1, d), lambda i: (0, 0)),
                  pl.BlockSpec(w_bf.shape, lambda i: (0, 0), pipeline_mode=pl.Buffered(1))],
        out_specs=[pl.BlockSpec((tm, tn), lambda i: (i, 0))] * n_out,
        out_shape=[jax.ShapeDtypeStruct((t, tn), F32)] * n_out,
        compiler_params=_cparams(1, 56),
        name="rms_matmul",
    )(x, g.reshape(1, d), w_bf)


def _bucket_np(rel):
    n = np.maximum(rel, 0)
    nf = np.maximum(n, 1).astype(np.float32)
    large = MAX_EXACT + (np.log(nf / MAX_EXACT) / np.float32(math.log(REL_MAX_DIST / MAX_EXACT))
                         * (NUM_BUCKETS - MAX_EXACT)).astype(np.int32)
    large = np.minimum(large, NUM_BUCKETS - 1)
    return np.where(n < MAX_EXACT, n, large).astype(np.int32)


def _bias_kernel(rb_ref, rbx_ref, bkp_ref, bkf_ref, bkl_ref, bko_ref, tp_ref, tf_ref, tl_ref, to_ref):
    def lookup(bk, table):
        acc = jnp.full(bk.shape, NEG, F32)
        for b in range(NUM_BUCKETS):
            acc = jnp.where(bk == b, table(b), acc)
        return acc

    for h in range(N_HEADS):
        for t in range(2):
            tp_ref[h, t] = lookup(bkp_ref[t], lambda b: rb_ref[b, h])
    by_row = lambda b: rbx_ref[b]
    tf_ref[...] = lookup(bkf_ref[...], by_row)
    tl_ref[...] = lookup(bkl_ref[...], by_row)
    to_ref[...] = lookup(bko_ref[...], by_row)


def bias_tiles(rel_bias, n_past):
    key = np.arange(MOBA_BLOCK)[:, None]
    qry = np.arange(MOBA_BLOCK)[None, :]
    bkp = np.stack([_bucket_np(qry - key), _bucket_np(MOBA_BLOCK + qry - key)])
    r = np.arange(N_HEADS * 8)[:, None]
    rh, rq = r // 8, r % 8
    c = np.arange(PAGE_SIZE * N_HEADS)[None, :]
    ct, ch = c // N_HEADS, c % N_HEADS
    bkf = np.where(rh == ch, NUM_BUCKETS - 1, -1)
    bkl = np.where(rh == ch, _bucket_np(n_past + rq - (n_past - PAGE_SIZE + ct)), -1)
    assert PAGE_SIZE >= REL_MAX_DIST
    co = np.arange(128)[None, :]
    coh, cot = co // 8, co % 8
    bko = np.where((rh == coh) & (cot <= rq), _bucket_np(rq - cot), -1)
    rbx = jnp.repeat(rel_bias, 8, axis=1)[:, :, None]
    vm = pl.BlockSpec(memory_space=pltpu.VMEM)
    i32 = lambda a: jnp.asarray(a.astype(np.int32))
    return pl.pallas_call(
        _bias_kernel,
        in_specs=[pl.BlockSpec(memory_space=pltpu.SMEM), vm, vm, vm, vm, vm],
        out_specs=[vm, vm, vm, vm],
        out_shape=[jax.ShapeDtypeStruct((N_HEADS, 2, MOBA_BLOCK, MOBA_BLOCK), F32),
                   jax.ShapeDtypeStruct(bkf.shape, F32),
                   jax.ShapeDtypeStruct(bkl.shape, F32),
                   jax.ShapeDtypeStruct(bko.shape, F32)],
        compiler_params=pltpu.CompilerParams(vmem_limit_bytes=32 * MIB),
        name="bias_tiles",
    )(rel_bias, rbx, i32(bkp), i32(bkf), i32(bkl), i32(bko))


def _split_bf16(x):
    hi = x.astype(BF16)
    lo = (x - hi.astype(F32)).astype(BF16)
    return hi, lo


def _moba_prompt_kernel(q_ref, k_ref, v_ref, bias_ref, c31_ref, o_ref, qbf, kbf, vt, pen_ref, s_ref):
    nb = kbf.shape[0] // MOBA_BLOCK
    scale = HEAD_DIM ** -0.5
    blk = lambda b: slice(b * MOBA_BLOCK, (b + 1) * MOBA_BLOCK)

    q = q_ref[...]
    q_hi, q_lo = _split_bf16(q)
    qbf[...] = q_hi
    kbf[...] = k_ref[...].astype(BF16)
    km = jnp.concatenate([jnp.mean(k_ref[blk(b), :], axis=0, keepdims=True) for b in range(nb)]
                         + [jnp.zeros((16 - nb, HEAD_DIM), F32)], axis=0)
    for b in range(nb):
        vt[b] = v_ref[blk(b), :].T.astype(BF16)
    km_hi, km_lo = _split_bf16(km)
    gate = (lax.dot_general(km_hi, q_hi, NT_DIMS, preferred_element_type=F32)
            + lax.dot_general(km_lo, q_hi, NT_DIMS, preferred_element_type=F32)
            + lax.dot_general(km_hi, q_lo, NT_DIMS, preferred_element_type=F32))

    row = lax.broadcasted_iota(jnp.int32, gate.shape, 0)
    own = lax.broadcasted_iota(jnp.int32, gate.shape, 1) // MOBA_BLOCK
    rank = jnp.zeros(gate.shape, F32)
    for b2 in range(nb):
        gb = gate[b2:b2 + 1, :]
        beats = ((gb > gate) | ((gb == gate) & (b2 < row))) & (b2 < own)
        rank = rank + jnp.where(beats, 1.0, 0.0)
    pen_ref[...] = jnp.where((row < own) & (rank < float(MOBA_TOPK)), 0.0, NEG)

    key = lax.broadcasted_iota(jnp.int32, (MOBA_BLOCK, MOBA_BLOCK), 0)
    qry = lax.broadcasted_iota(jnp.int32, (MOBA_BLOCK, MOBA_BLOCK), 1)
    c31 = c31_ref[pl.program_id(1)]
    for qi in range(nb):
        m = None
        for kb in range(qi + 1):
            s = lax.dot_general(kbf[blk(kb), :], qbf[blk(qi), :], NT_DIMS, preferred_element_type=F32) * scale
            if kb == qi:
                s = jnp.where(key <= qry, s + bias_ref[0, 0], NEG)
            else:
                s = s + (bias_ref[0, 1] if kb == qi - 1 else c31) + pen_ref[kb:kb + 1, blk(qi)]
            s_ref[kb] = s
            cm = jnp.max(s, axis=0, keepdims=True)
            m = cm if m is None else jnp.maximum(m, cm)
        lsum = jnp.zeros((1, MOBA_BLOCK), F32)
        acc = jnp.zeros((HEAD_DIM, MOBA_BLOCK), F32)
        for kb in range(qi + 1):
            p = jnp.exp(s_ref[kb] - m)
            lsum = lsum + jnp.sum(p, axis=0, keepdims=True)
            acc = acc + jnp.dot(vt[kb], p.astype(BF16), preferred_element_type=F32)
        o_ref[blk(qi), :] = (acc / lsum).T.astype(o_ref.dtype)


def moba_prompt(q, k, v, bias_p, c31_h, n_seq, seq):
    nq = seq // MOBA_BLOCK
    tok = pl.BlockSpec((seq, HEAD_DIM), lambda n, h: (n, h))
    return pl.pallas_call(
        _moba_prompt_kernel,
        grid=(n_seq, N_HEADS),
        in_specs=[tok, tok, tok,
                  pl.BlockSpec((1, 2, MOBA_BLOCK, MOBA_BLOCK), lambda n, h: (h, 0, 0, 0)),
                  pl.BlockSpec(memory_space=pltpu.SMEM)],
        out_specs=tok,
        out_shape=jax.ShapeDtypeStruct(q.shape, BF16),
        scratch_shapes=[pltpu.VMEM((seq, HEAD_DIM), BF16), pltpu.VMEM((seq, HEAD_DIM), BF16),
                        pltpu.VMEM((nq, HEAD_DIM, MOBA_BLOCK), BF16), pltpu.VMEM((16, seq), F32),
                        pltpu.VMEM((nq, MOBA_BLOCK, MOBA_BLOCK), F32)],
        compiler_params=_cparams(2, 48),
        name="moba_prompt",
    )(q, k, v, bias_p, c31_h)


def _by_head_rows(x):
    return jnp.concatenate([x[:, h * HEAD_DIM:(h + 1) * HEAD_DIM] for h in range(x.shape[1] // HEAD_DIM)], axis=0)


def _by_head_lanes(x, n_heads):
    t = x.shape[0] // n_heads
    return jnp.concatenate([x[h * t:(h + 1) * t] for h in range(n_heads)], axis=1)


def _moba_sample_kernel(pt_ref, q_ref, kn_ref, vn_ref, tf_ref, tl_ref, to_ref, *refs, n_pages, n_route):
    kp, vp = refs[:n_pages], refs[n_pages:2 * n_pages]
    refs = refs[2 * n_pages:]
    if n_route:
        qh_ref, sk_ref, o_ref, cnt_ref, p1_ref, rk2_ref, e2_ref, s_ref = refs
    else:
        o_ref, s_ref = refs
    ppb = MOBA_BLOCK // PAGE_SIZE
    nb = n_pages // ppb
    scale = HEAD_DIM ** -0.5
    t_new = q_ref.shape[1]
    nr = N_HEADS * t_new

    q = _by_head_rows(q_ref[0])
    q_bf = q.astype(BF16)
    zpad = jnp.zeros((128 - nr, HEAD_DIM), F32)
    kn = jnp.concatenate([_by_head_rows(kn_ref[0]), zpad], axis=0).astype(BF16)
    vn = jnp.concatenate([_by_head_rows(vn_ref[0]), zpad], axis=0).astype(BF16)

    gates = []
    for b in range(nb):
        ksum = jnp.zeros((N_HEADS, HEAD_DIM), F32)
        for pg in range(ppb):
            ksum = ksum + jnp.sum(kp[b * ppb + pg][...].reshape(PAGE_SIZE, N_HEADS, HEAD_DIM), axis=0)
        kmean = ksum / float(MOBA_BLOCK)
        krep = jnp.concatenate([jnp.broadcast_to(kmean[h:h + 1, :], (t_new, HEAD_DIM)) for h in range(N_HEADS)],
                               axis=0)
        gates.append(jnp.sum(q * krep, axis=1, keepdims=True))

    penalty = []
    for b in range(nb):
        rank = jnp.zeros((nr, 1), F32)
        for b2 in range(nb):
            if b2 != b:
                beats = (gates[b2] > gates[b]) | ((gates[b2] == gates[b]) & (b2 < b))
                rank = rank + jnp.where(beats, 1.0, 0.0)
        penalty.append(jnp.where(rank < float(MOBA_TOPK), 0.0, NEG))

    s_own = lax.dot_general(q_bf, kn, NT_DIMS, preferred_element_type=F32) * scale + to_ref[...]
    m = jnp.max(s_own, axis=-1, keepdims=True)
    for pg in range(n_pages):
        s = lax.dot_general(q_bf, kp[pg][...].astype(BF16), NT_DIMS, preferred_element_type=F32)
        bias = tl_ref[...] if pg == n_pages - 1 else tf_ref[...]
        s = s * scale + bias + penalty[pg // ppb]
        s_ref[pg] = s
        m = jnp.maximum(m, jnp.max(s, axis=-1, keepdims=True))
    p = jnp.exp(s_own - m)
    lsum = jnp.sum(p, axis=-1, keepdims=True)
    acc = jnp.dot(p.astype(BF16), vn, preferred_element_type=F32)
    for pg in range(n_pages):
        p = jnp.exp(s_ref[pg] - m)
        lsum = lsum + jnp.sum(p, axis=-1, keepdims=True)
        acc = acc + jnp.dot(p.astype(BF16), vp[pg][...].astype(BF16), preferred_element_type=F32)
    o_ref[0] = _by_head_lanes(acc / lsum, N_HEADS)

    if n_route:
        tabs = (cnt_ref, p1_ref, rk2_ref, e2_ref)
        winners = [_route_unit(qh_ref[u], sk_ref[u], [t.at[u] for t in tabs], False) for u in range(n_route)]
        for u in range(n_route):
            @pl.when(jnp.max(winners[u]) > float(3 * PEER_TOPK))
            def _(u=u):
                _route_unit(qh_ref[u], sk_ref[u], [t.at[u] for t in tabs], True)


def moba_sample(q, kn, vn, ck, cv, page_table, t_far, t_last, t_own, route=None):
    nseq, n_pages = page_table.shape
    tok = pl.BlockSpec((1,) + q.shape[1:], lambda b, pt: (b, 0, 0))

    def page_spec(p):
        return pl.BlockSpec((None,) + ck.shape[1:], lambda b, pt, p=p: (pt[b, p], 0, 0))

    const = lambda a: pl.BlockSpec(a.shape, lambda b, pt: (0, 0))
    in_specs = ([tok, tok, tok, const(t_far), const(t_last), const(t_own)]
                + [page_spec(p) for p in range(n_pages)] * 2)
    out_specs, out_shape, extra, n_route = [tok], [jax.ShapeDtypeStruct(q.shape, F32)], [], 0
    if route is not None:
        qh, sk_bf = route
        t = qh.shape[1]
        units = PEER_HEADS * (t // ROUTE_TQ)
        n_route = units // nseq
        assert n_route * nseq == units and PEER_HEADS % n_route == 0
        hps = PEER_HEADS // n_route
        in_specs += [pl.BlockSpec((n_route, ROUTE_TQ, PEER_QDIM), lambda b, pt: (b % hps, b // hps, 0)),
                     pl.BlockSpec((n_route, 2, PEER_NKEYS, PEER_QDIM // 2), lambda b, pt: (b % hps, 0, 0, 0))]
        tab = pl.BlockSpec((n_route, PEER_NKEYS, ROUTE_TQ), lambda b, pt: (b % hps, 0, b // hps))
        out_specs += [tab] * 4
        out_shape += [jax.ShapeDtypeStruct((PEER_HEADS, PEER_NKEYS, t), dt) for dt in (F32, F32, BF16, BF16)]
        extra = [qh, sk_bf]
    grid_spec = pltpu.PrefetchScalarGridSpec(
        num_scalar_prefetch=1, grid=(nseq,), in_specs=in_specs, out_specs=out_specs,
        scratch_shapes=[pltpu.VMEM((n_pages, N_HEADS * q.shape[1], ck.shape[1]), F32)])
    return pl.pallas_call(
        functools.partial(_moba_sample_kernel, n_pages=n_pages, n_route=n_route),
        grid_spec=grid_spec,
        out_shape=out_shape,
        compiler_params=_cparams(1, 56),
        name="moba_sample",
    )(page_table, q, kn, vn, t_far, t_last, t_own, *([ck] * n_pages), *([cv] * n_pages), *extra)


HIST_ROWS = 32
HIST_OFF = HIST_ROWS - (CONV_K - 1)


def _conv_kernel(ga_ref, gg_ref, hist_ref, w_ref, b_ref, lg_ref, lb_ref, o_ref, nb_ref, ext_ref, y_ref,
                 *, tt, n_t, sb):
    t = pl.program_id(1)

    @pl.when(t == 0)
    def _():
        for s in range(sb):
            ext_ref[s, HIST_OFF:HIST_ROWS, :] = hist_ref[s]

    rt = min(tt, 128)
    for s in range(sb):
        rows = slice(s * tt, (s + 1) * tt)
        ext_ref[s, HIST_ROWS:HIST_ROWS + tt, :] = ga_ref[rows, :] * jax.nn.sigmoid(gg_ref[rows, :])
        for c in range(CONV_CH // 128):
            cs = slice(c * 128, (c + 1) * 128)
            for r0 in range(0, tt, rt):
                acc = jnp.zeros((rt, 128), F32)
                for res in range(8):
                    taps = [j for j in range(CONV_K) if (HIST_OFF + j) % 8 == res]
                    q0 = (HIST_OFF + taps[0]) // 8
                    q1 = (HIST_OFF + taps[-1]) // 8
                    if res:
                        win = ext_ref[s, r0 + 8 * q0:r0 + 8 * (q1 + 1) + rt, cs]
                        win = pltpu.roll(win, win.shape[0] - res, axis=0)
                    else:
                        win = ext_ref[s, r0 + 8 * q0:r0 + 8 * q1 + rt, cs]
                    for j in taps:
                        off = 8 * ((HIST_OFF + j) // 8 - q0)
                        acc = acc + w_ref[j:j + 1, cs] * win[off:off + rt]
                y_ref[s * tt + r0:s * tt + r0 + rt, cs] = acc + b_ref[:, cs]
    y = y_ref[...]
    mu = jnp.mean(y, axis=-1, keepdims=True)
    yc = y - mu
    var = jnp.mean(yc * yc, axis=-1, keepdims=True)
    yn = yc * lax.rsqrt(var + EPS) * lg_ref[...] + lb_ref[...]
    o_ref[...] = (yn * jax.nn.sigmoid(yn)).astype(o_ref.dtype)

    @pl.when(t == n_t - 1)
    def _():
        for s in range(sb):
            nb_ref[s] = ext_ref[s, tt + HIST_OFF:tt + HIST_ROWS, :]

    if n_t > 1:
        for s in range(sb):
            ext_ref[s, 0:HIST_ROWS, :] = ext_ref[s, tt:tt + HIST_ROWS, :]


def conformer_conv(ga, gg, hist, w_dw, b_dw, ln_g, ln_b, n_seq, seq, tt, sb):
    n_t = seq // tt
    assert sb == 1 or n_t == 1
    row = lambda a: a.reshape(1, CONV_CH)
    cvec = pl.BlockSpec((1, CONV_CH), lambda n, t: (0, 0))
    tile = pl.BlockSpec((sb * tt, CONV_CH), lambda n, t: (n * n_t + t, 0))
    hist_spec = pl.BlockSpec((sb, CONV_K - 1, CONV_CH), lambda n, t: (n, 0, 0))
    return pl.pallas_call(
        functools.partial(_conv_kernel, tt=tt, n_t=n_t, sb=sb),
        grid=(n_seq // sb, n_t),
        in_specs=[tile, tile, hist_spec, pl.BlockSpec((CONV_K, CONV_CH), lambda n, t: (0, 0)),
                  cvec, cvec, cvec],
        out_specs=[tile, hist_spec],
        out_shape=[jax.ShapeDtypeStruct((n_seq * seq, CONV_CH), BF16),
                   jax.ShapeDtypeStruct((n_seq, CONV_K - 1, CONV_CH), F32)],
        scratch_shapes=[pltpu.VMEM((sb, HIST_ROWS + tt, CONV_CH), F32), pltpu.VMEM((sb * tt, CONV_CH), F32)],
        compiler_params=_cparams(2, 32),
        name="conformer_conv",
    )(ga, gg, hist, w_dw, row(b_dw), row(ln_g), row(ln_b))


def _out_proj_kernel(x_ref, a_ref, c_ref, w_ref, o_ref):
    wa = a_ref.shape[1]
    y = jnp.dot(a_ref[...].astype(BF16), w_ref[0:wa, :], preferred_element_type=F32)
    y = y + jnp.dot(c_ref[...].astype(BF16), w_ref[wa:, :], preferred_element_type=F32)
    o_ref[...] = x_ref[...] + y


def out_proj(x, a, c, w_bf, tm):
    t, d = x.shape
    return pl.pallas_call(
        _out_proj_kernel,
        grid=(t // tm,),
        in_specs=[pl.BlockSpec((tm, d), lambda i: (i, 0)),
                  pl.BlockSpec((tm, a.shape[1]), lambda i: (i, 0)),
                  pl.BlockSpec((tm, c.shape[1]), lambda i: (i, 0)),
                  pl.BlockSpec(w_bf.shape, lambda i: (0, 0))],
        out_specs=pl.BlockSpec((tm, d), lambda i: (i, 0)),
        out_shape=jax.ShapeDtypeStruct((t, d), F32),
        compiler_params=_cparams(1, 48),
        name="out_proj",
    )(x, a, c, w_bf)


def _softmax_pv(s, mv_bf):
    m = jnp.max(s, axis=-1, keepdims=True)
    p = jnp.exp(s - m)
    o = jnp.dot(p.astype(BF16), mv_bf, preferred_element_type=F32)
    return o / jnp.sum(p, axis=-1, keepdims=True)


def _cross_kernel(h_ref, g_ref, wq_ref, wo_ref, mk_ref, mv_ref, o_ref, *, n_grp, rows):
    scale = HEAD_DIM ** -0.5
    h = h_ref[...]
    hn = _rms(h, g_ref[...]).astype(BF16)
    q = jnp.dot(hn, wq_ref[...], preferred_element_type=F32)
    outs = []
    for g in range(n_grp):
        mk = mk_ref[g].astype(BF16)
        mv = mv_ref[g].astype(BF16)
        qg = q[g * rows:(g + 1) * rows]
        if rows >= 128:
            heads = []
            for hd in range(X_HEADS):
                cs = slice(hd * HEAD_DIM, (hd + 1) * HEAD_DIM)
                s = lax.dot_general(qg[:, cs].astype(BF16), mk[:, cs], NT_DIMS, preferred_element_type=F32)
                heads.append(_softmax_pv(s * scale, mv[:, cs]))
            outs.append(jnp.concatenate(heads, axis=1))
        else:
            qx = _by_head_rows(qg).astype(BF16)
            s = lax.dot_general(qx, mk, NT_DIMS, preferred_element_type=F32) * scale
            rowh = lax.broadcasted_iota(jnp.int32, s.shape, 0) // rows
            colh = lax.broadcasted_iota(jnp.int32, s.shape, 1) % X_HEADS
            o = _softmax_pv(jnp.where(rowh == colh, s, NEG), mv)
            outs.append(_by_head_lanes(o, X_HEADS))
    o_all = outs[0] if n_grp == 1 else jnp.concatenate(outs, axis=0)
    o_ref[...] = h + jnp.dot(o_all.astype(BF16), wo_ref[...], preferred_element_type=F32)


def cross_block(h, g, wq_bf, wo_bf, mk, mv, n_grp, rows, tiles_per_seq):
    t, d = h.shape
    tm = n_grp * rows
    if n_grp == 1:
        mem_map = lambda i: (i // tiles_per_seq, 0, 0)
    else:
        mem_map = lambda i: (i, 0, 0)
    mem_spec = pl.BlockSpec((n_grp,) + mk.shape[1:], mem_map)
    return pl.pallas_call(
        functools.partial(_cross_kernel, n_grp=n_grp, rows=rows),
        grid=(t // tm,),
        in_specs=[pl.BlockSpec((tm, d), lambda i: (i, 0)),
                  pl.BlockSpec((1, d), lambda i: (0, 0)),
                  pl.BlockSpec(wq_bf.shape, lambda i: (0, 0)),
                  pl.BlockSpec(wo_bf.shape, lambda i: (0, 0)),
                  mem_spec, mem_spec],
        out_specs=pl.BlockSpec((tm, d), lambda i: (i, 0)),
        out_shape=jax.ShapeDtypeStruct((t, d), F32),
        compiler_params=_cparams(1, 48),
        name="cross_block",
    )(h, g.reshape(1, d), wq_bf, wo_bf, mk, mv)


def _top16(s, exact_ties):
    n, t = s.shape
    row = lax.broadcasted_iota(jnp.int32, (n, t), 0).astype(F32)
    row16 = lax.broadcasted_iota(jnp.int32, (PEER_TOPK, t), 0)
    rank = jnp.full((n, t), float(PEER_TOPK), F32)
    vals = jnp.zeros((PEER_TOPK, t), F32)
    for r in range(PEER_TOPK):
        m = jnp.max(s, axis=0, keepdims=True)
        if exact_ties:
            idx = jnp.min(jnp.where(s == m, row, float(n)), axis=0, keepdims=True)
            hit = row == idx
        else:
            hit = s == m
        rank = jnp.where(hit, float(r), rank)
        s = jnp.where(hit, -jnp.inf, s)
        vals = jnp.where(row16 == r, m, vals)
    return vals, rank


def _pair_pieces(v1, v2, e1, e2):
    t = v1.shape[1]
    sub = lax.broadcasted_iota(jnp.int32, (8, t), 0)
    subf = sub.astype(F32)
    pieces = []

    def col(b, a0, a_max):
        a = sub + a0
        pieces.append(dict(c=v1[a0:a0 + 8] + v2[b:b + 1], e=e1[a0:a0 + 8] * e2[b:b + 1],
                           f=(subf + a0) * 16.0 + b, ok=a <= a_max, a0=a0, row_a=None))

    def rowp(a, b0, b_min, b_max):
        b = sub + b0
        pieces.append(dict(c=v1[a:a + 1] + v2[b0:b0 + 8], e=e1[a:a + 1] * e2[b0:b0 + 8],
                           f=a * 16.0 + (subf + b0), ok=(b >= b_min) & (b <= b_max), a0=None, row_a=a))

    col(0, 0, 15), col(0, 8, 15), col(1, 0, 7), col(2, 0, 4), col(3, 0, 3)
    rowp(0, 8, 8, 15), rowp(0, 0, 4, 7), rowp(1, 0, 4, 7), rowp(2, 0, 4, 4)
    for p in pieces:
        p["c"] = jnp.where(p["ok"], p["c"], -jnp.inf)
        p["f"] = jnp.where(p["ok"], p["f"], -1.0)
    return pieces


def _route_head(s1, s2, exact_ties):
    tq = s1.shape[1]
    v1, rank1 = _top16(s1, exact_ties)
    v2, rank2 = _top16(s2, exact_ties)
    e1 = jnp.exp(v1 - v1[0:1])
    e2 = jnp.exp(v2 - v2[0:1])
    pieces = _pair_pieces(v1, v2, e1, e2)

    taken = [jnp.zeros((8, tq), F32) for _ in pieces]
    cs = [p["c"] for p in pieces]
    for _ in range(PEER_TOPK):
        m = cs[0]
        for c in cs[1:]:
            m = jnp.maximum(m, c)
        m = jnp.max(m, axis=0, keepdims=True)
        if exact_ties:
            fm = None
            for c, p in zip(cs, pieces):
                cand = jnp.where(c == m, p["f"], 1e9)
                fm = cand if fm is None else jnp.minimum(fm, cand)
            fm = jnp.min(fm, axis=0, keepdims=True)
        for i, p in enumerate(pieces):
            hit = (p["f"] == fm) if exact_ties else (cs[i] == m)
            taken[i] = jnp.where(hit, 1.0, taken[i])
            cs[i] = jnp.where(hit, -jnp.inf, cs[i])

    row16 = lax.broadcasted_iota(jnp.int32, (PEER_TOPK, tq), 0)
    n1 = jnp.zeros((PEER_TOPK, tq), F32)
    z = jnp.zeros((1, tq), F32)
    for tf, p in zip(taken, pieces):
        z = z + jnp.sum(tf * p["e"], axis=0, keepdims=True)
        if p["row_a"] is None:
            pad = jnp.zeros((8, tq), F32)
            n1 = n1 + (jnp.concatenate([tf, pad], axis=0) if p["a0"] == 0
                       else jnp.concatenate([pad, tf], axis=0))
        else:
            n1 = n1 + jnp.where(row16 == p["row_a"], jnp.sum(tf, axis=0, keepdims=True), 0.0)

    in1 = rank1 < float(PEER_TOPK)
    in2 = rank2 < float(PEER_TOPK)
    cnt1 = jnp.zeros(rank1.shape, F32)
    for r in range(PEER_TOPK):
        cnt1 = jnp.where(rank1 == float(r), n1[r:r + 1], cnt1)
    p1 = jnp.where(in1, jnp.exp(s1 - v1[0:1]) / z, 0.0)
    e2_dense = jnp.where(in2, jnp.exp(s2 - v2[0:1]), 0.0)
    winners = (jnp.sum(jnp.where(in1, 1.0, 0.0), axis=0, keepdims=True)
               + jnp.sum(jnp.where(in2, 1.0, 0.0), axis=0, keepdims=True)
               + jnp.sum(n1, axis=0, keepdims=True))
    return cnt1, p1, rank2, e2_dense, winners


def _route_unit(qh, sk, tabs, exact_ties):
    half = PEER_QDIM // 2
    s1 = lax.dot_general(sk[0], qh[:, :half], NT_DIMS, preferred_element_type=F32)
    s2 = lax.dot_general(sk[1], qh[:, half:], NT_DIMS, preferred_element_type=F32)
    cnt1, p1, rank2, e2_dense, winners = _route_head(s1, s2, exact_ties)
    cnt_ref, p1_ref, rk2_ref, e2_ref = tabs
    cnt_ref[...] = cnt1
    p1_ref[...] = p1
    rk2_ref[...] = rank2.astype(BF16)
    e2_ref[...] = e2_dense.astype(BF16)
    return winners


def _peer_query_kernel(h_ref, g_ref, wpq_ref, xnt_ref, qh_ref):
    hn = _rms(h_ref[...], g_ref[...])
    xnt_ref[...] = hn.T.astype(BF16)
    q = jnp.dot(hn.astype(BF16), wpq_ref[...], preferred_element_type=F32)
    for hh in range(PEER_HEADS):
        qh_ref[hh] = q[:, hh * PEER_QDIM:(hh + 1) * PEER_QDIM].astype(BF16)


def peer_query(h, g, wpq_bf):
    t, d = h.shape
    tq = ROUTE_TQ
    return pl.pallas_call(
        _peer_query_kernel,
        grid=(t // tq,),
        in_specs=[pl.BlockSpec((tq, d), lambda i: (i, 0)),
                  pl.BlockSpec((1, d), lambda i: (0, 0)),
                  pl.BlockSpec(wpq_bf.shape, lambda i: (0, 0), pipeline_mode=pl.Buffered(1))],
        out_specs=[pl.BlockSpec((d, tq), lambda i: (0, i)),
                   pl.BlockSpec((PEER_HEADS, tq, PEER_QDIM), lambda i: (0, i, 0))],
        out_shape=[jax.ShapeDtypeStruct((d, t), BF16), jax.ShapeDtypeStruct((PEER_HEADS, t, PEER_QDIM), BF16)],
        compiler_params=_cparams(1, 48),
        name="peer_query",
    )(h, g.reshape(1, d), wpq_bf)


def _peer_route_kernel(qh_ref, sk_ref, cnt_ref, p1_ref, rk2_ref, e2_ref, *, n_route):
    tabs = (cnt_ref, p1_ref, rk2_ref, e2_ref)
    winners = [_route_unit(qh_ref[u], sk_ref[u], [t.at[u] for t in tabs], False) for u in range(n_route)]
    for u in range(n_route):
        @pl.when(jnp.max(winners[u]) > float(3 * PEER_TOPK))
        def _(u=u):
            _route_unit(qh_ref[u], sk_ref[u], [t.at[u] for t in tabs], True)


def peer_route(qh, sk_bf, n_route=2):
    t = qh.shape[1]
    hps = PEER_HEADS // n_route
    tab = pl.BlockSpec((n_route, PEER_NKEYS, ROUTE_TQ), lambda i, j: (j, 0, i))
    return pl.pallas_call(
        functools.partial(_peer_route_kernel, n_route=n_route),
        grid=(t // ROUTE_TQ, hps),
        in_specs=[pl.BlockSpec((n_route, ROUTE_TQ, PEER_QDIM), lambda i, j: (j, i, 0)),
                  pl.BlockSpec((n_route, 2, PEER_NKEYS, PEER_QDIM // 2), lambda i, j: (j, 0, 0, 0))],
        out_specs=[tab] * 4,
        out_shape=[jax.ShapeDtypeStruct((PEER_HEADS, PEER_NKEYS, t), dt) for dt in (F32, F32, BF16, BF16)],
        compiler_params=_cparams(2, 32),
        name="peer_route",
    )(qh, sk_bf)


def _peer_dense_kernel(xnt_ref, u_ref, vt_ref, cnt_ref, p1_ref, rk2_ref, e2_ref, h_ref, fg_ref, y_ref, acc_ref,
                       *, te, n_e):
    j = pl.program_id(1)

    @pl.when(j == 0)
    def _():
        acc_ref[...] = jnp.zeros(acc_ref.shape, F32)

    tq = xnt_ref.shape[1]
    groups = te // PEER_NKEYS
    gpc = PEER_ROW_CHUNK // PEER_NKEYS
    parts = []
    for a in range(groups):
        if a % gpc == 0:
            rows = slice(a * PEER_NKEYS, (a + gpc) * PEER_NKEYS)
            act = jax.nn.gelu(jnp.dot(u_ref[rows, :], xnt_ref[...], preferred_element_type=F32))
        i1 = j * groups + a
        gate = jnp.zeros((PEER_NKEYS, tq), BF16)
        for hh in range(PEER_HEADS):
            cnt_row = cnt_ref[hh, pl.ds(i1, 1), :].astype(BF16)
            p1_row = p1_ref[hh, pl.ds(i1, 1), :].astype(BF16)
            gate = gate + jnp.where(rk2_ref[hh] < cnt_row, p1_row * e2_ref[hh], jnp.zeros((), BF16))
        lo = (a % gpc) * PEER_NKEYS
        parts.append(act[lo:lo + PEER_NKEYS].astype(BF16) * gate)
    hm_t = parts[0] if groups == 1 else jnp.concatenate(parts, axis=0)
    acc_ref[...] += jnp.dot(vt_ref[...], hm_t, preferred_element_type=F32)

    @pl.when(j == n_e - 1)
    def _():
        y_ref[...] = _rms(h_ref[...] + acc_ref[...].T, fg_ref[...])


def peer_dense(xnt, u_bf, vt_bf, tabs, h, final_g, tq, te):
    t, d = h.shape
    n_e = u_bf.shape[0] // te
    once = dict(pipeline_mode=pl.Buffered(1))
    tab = pl.BlockSpec((PEER_HEADS, PEER_NKEYS, tq), lambda i, j: (0, 0, i), **once)
    return pl.pallas_call(
        functools.partial(_peer_dense_kernel, te=te, n_e=n_e),
        grid=(t // tq, n_e),
        in_specs=[pl.BlockSpec((d, tq), lambda i, j: (0, i), **once),
                  pl.BlockSpec((te, d), lambda i, j: (j, 0)),
                  pl.BlockSpec((d, te), lambda i, j: (0, j)),
                  tab, tab, tab, tab,
                  pl.BlockSpec((tq, d), lambda i, j: (i, 0), **once),
                  pl.BlockSpec((1, d), lambda i, j: (0, 0))],
        out_specs=pl.BlockSpec((tq, d), lambda i, j: (i, 0)),
        out_shape=jax.ShapeDtypeStruct((t, d), F32),
        scratch_shapes=[pltpu.VMEM((d, tq), F32)],
        compiler_params=_cparams(2, 56),
        name="peer_dense",
    )(xnt, u_bf, vt_bf, *tabs, h, final_g.reshape(1, d))


def kernel(x_prompt, x_sample, mem_prompt, cache_k, cache_v, cache_conv, cache_mem_k, cache_mem_v, page_table,
           norm1_g, w_in, rel_bias, conv_w, conv_b, conv_ln_g, conv_ln_b, w_out, norm2_g, mem_norm_g, w_cq,
           w_mk, w_mv, w_co, norm3_g, w_pq, peer_sub_keys, peer_u, peer_v, final_g):
    depth = w_in.shape[0]
    assert depth == 1, "single-layer step"
    n_p, seq, d = x_prompt.shape
    n_s, t_s, _ = x_sample.shape
    l = 0
    bf = lambda a: a.astype(BF16)

    w_in_bf, w_out_bf = bf(w_in[l]), bf(w_out[l])
    w_cq_bf, w_co_bf, w_pq_bf = bf(w_cq[l]), bf(w_co[l]), bf(w_pq[l])
    w_mem_bf = bf(jnp.concatenate([w_mk[l], w_mv[l]], axis=1))
    sk_bf, u_bf, vt_bf = bf(peer_sub_keys[l]), bf(peer_u[l]), bf(peer_v[l].T)

    n_pool, n_pages = cache_k.shape[1], page_table.shape[1]
    bias_p, t_far, t_last, t_own = bias_tiles(rel_bias, n_pages * PAGE_SIZE)
    c31_h = rel_bias[NUM_BUCKETS - 1]

    def cross_and_query(h, mk, mv, n_grp, rows, tiles_per_seq):
        h = cross_block(h, norm2_g[l], w_cq_bf, w_co_bf, mk, mv, n_grp, rows, tiles_per_seq)
        return (h,) + tuple(peer_query(h, norm3_g[l], w_pq_bf))

    experts = lambda xnt, tabs, h: peer_dense(xnt, u_bf, vt_bf, tabs, h, final_g, tq=512, te=1024)

    xp = x_prompt.reshape(n_p * seq, d)
    mk_p, mv_p = rms_matmul(mem_prompt.reshape(n_p * N_MEM, d), mem_norm_g[l], w_mem_bf, 2, tm=512)
    q_p, k_p, v_p, ga_p, gg_p = rms_matmul(xp, norm1_g[l], w_in_bf, 5, tm=256)
    attn_p = moba_prompt(q_p, k_p, v_p, bias_p, c31_h, n_p, seq)
    zero_buf = jnp.zeros((n_p, CONV_K - 1, CONV_CH), F32)
    conv_p, buf_p = conformer_conv(ga_p, gg_p, zero_buf, conv_w[l], conv_b[l], conv_ln_g[l], conv_ln_b[l],
                                   n_p, seq, tt=256, sb=1)
    h_p = out_proj(xp, attn_p, conv_p, w_out_bf, tm=512)
    h_p, xnt_p, qh_p = cross_and_query(h_p, mk_p.reshape(n_p, N_MEM, X_WIDTH), mv_p.reshape(n_p, N_MEM, X_WIDTH),
                                       1, 256, seq // 256)

    xs = x_sample.reshape(n_s * t_s, d)
    q_s, k_s, v_s, ga_s, gg_s = rms_matmul(xs, norm1_g[l], w_in_bf, 5, tm=256)
    tok3 = lambda a: a.reshape(n_s, t_s, ATTN_WIDTH)
    units_p = PEER_HEADS * (n_p * seq // ROUTE_TQ)
    fuse = units_p % n_s == 0 and PEER_HEADS % (units_p // n_s) == 0
    attn_s, *tabs_p = moba_sample(tok3(q_s), tok3(k_s), tok3(v_s),
                                  cache_k[l].reshape(n_pool, PAGE_SIZE * N_HEADS, HEAD_DIM),
                                  cache_v[l].reshape(n_pool, PAGE_SIZE * N_HEADS, HEAD_DIM),
                                  page_table, t_far, t_last, t_own, route=(qh_p, sk_bf) if fuse else None)
    if not fuse:
        tabs_p = peer_route(qh_p, sk_bf)
    y_p = experts(xnt_p, tabs_p, h_p)
    conv_s, buf_s = conformer_conv(ga_s, gg_s, cache_conv[l], conv_w[l], conv_b[l], conv_ln_g[l], conv_ln_b[l],
                                   n_s, t_s, tt=t_s, sb=8)
    h_s = out_proj(xs, attn_s.reshape(n_s * t_s, ATTN_WIDTH), conv_s, w_out_bf, tm=512)
    h_s, xnt_s, qh_s = cross_and_query(h_s, cache_mem_k[l].reshape(n_s, N_MEM * X_HEADS, HEAD_DIM),
                                       cache_mem_v[l].reshape(n_s, N_MEM * X_HEADS, HEAD_DIM), 8, t_s, 1)
    y_s = experts(xnt_s, peer_route(qh_s, sk_bf), h_s)

    n_pg = seq // PAGE_SIZE
    return (y_p.reshape(n_p, seq, d), y_s.reshape(n_s, t_s, d),
            k_p.reshape(1, n_p, n_pg, PAGE_SIZE, N_HEADS, HEAD_DIM),
            v_p.reshape(1, n_p, n_pg, PAGE_SIZE, N_HEADS, HEAD_DIM),
            buf_p[None],
            mk_p.reshape(1, n_p, N_MEM, X_HEADS, HEAD_DIM), mv_p.reshape(1, n_p, N_MEM, X_HEADS, HEAD_DIM),
            k_s.reshape(1, n_s, t_s, N_HEADS, HEAD_DIM), v_s.reshape(1, n_s, t_s, N_HEADS, HEAD_DIM),
            buf_s[None])
```

```python
import functools
import math

import numpy as np
import jax
import jax.numpy as jnp
from jax import lax
from jax.experimental import pallas as pl
from jax.experimental.pallas import tpu as pltpu

F32 = jnp.float32
BF16 = jnp.bfloat16

D_MODEL = 2048
HEAD_DIM = 128
N_HEADS = 8
ATTN_WIDTH = N_HEADS * HEAD_DIM
CONV_CH = 1024
MOBA_BLOCK = 256
MOBA_TOPK = 3
PAGE_SIZE = 128
NUM_BUCKETS = 32
MAX_EXACT = 16
REL_MAX_DIST = 128
CONV_K = 31
N_MEM = 256
X_HEADS = 4
X_WIDTH = X_HEADS * HEAD_DIM
PEER_HEADS = 8
PEER_NKEYS = 128
PEER_TOPK = 16
PEER_QDIM = 256
PEER_ROW_CHUNK = 512
ROUTE_TQ = 256
QUERY_TQ = 256
ROUTE_STEPS_PER_PAGE = 2
EPS = 1e-6
NEG = -1e30

MIB = 1024 * 1024
NT_DIMS = (((1,), (1,)), ((), ()))


def _cparams(n_grid, vmem_mib):
    return pltpu.CompilerParams(dimension_semantics=("arbitrary",) * n_grid,
                                vmem_limit_bytes=vmem_mib * MIB)


def _rms(x, g):
    return x * lax.rsqrt(jnp.mean(x * x, axis=-1, keepdims=True) + EPS) * g


def _rms_matmul_kernel(x_ref, g_ref, w_ref, *outs):
    xn = _rms(x_ref[...], g_ref[...]).astype(BF16)
    tn = outs[0].shape[1]
    for s, o_ref in enumerate(outs):
        o_ref[...] = jnp.dot(xn, w_ref[:, s * tn:(s + 1) * tn], preferred_element_type=F32)


def rms_matmul(x, g, w_bf, n_out, tm):
    t, d = x.shape
    tn = w_bf.shape[1] // n_out
    return pl.pallas_call(
        _rms_matmul_kernel,
        grid=(t // tm,),
        in_specs=[pl.BlockSpec((tm, d), lambda i: (i, 0)),
                  pl.BlockSpec((1, d), lambda i: (0, 0)),
                  pl.BlockSpec(w_bf.shape, lambda i: (0, 0), pipeline_mode=pl.Buffered(1))],
        out_specs=[pl.BlockSpec((tm, tn), lambda i: (i, 0))] * n_out,
        out_shape=[jax.ShapeDtypeStruct((t, tn), F32)] * n_out,
        compiler_params=_cparams(1, 56),
        name="rms_matmul",
    )(x, g.reshape(1, d), w_bf)


def _bucket_np(rel):
    n = np.maximum(rel, 0)
    nf = np.maximum(n, 1).astype(np.float32)
    large = MAX_EXACT + (np.log(nf / MAX_EXACT) / np.float32(math.log(REL_MAX_DIST / MAX_EXACT))
                         * (NUM_BUCKETS - MAX_EXACT)).astype(np.int32)
    large = np.minimum(large, NUM_BUCKETS - 1)
    return np.where(n < MAX_EXACT, n, large).astype(np.int32)


def _bias_kernel(rb_ref, rbx_ref, bkp_ref, bkf_ref, bkl_ref, bko_ref, tp_ref, tf_ref, tl_ref, to_ref):
    def lookup(bk, table):
        acc = jnp.full(bk.shape, NEG, F32)
        for b in range(NUM_BUCKETS):
            acc = jnp.where(bk == b, table(b), acc)
        return acc

    for h in range(N_HEADS):
        for t in range(2):
            tp_ref[h, t] = lookup(bkp_ref[t], lambda b: rb_ref[b, h])
    by_row = lambda b: rbx_ref[b]
    tf_ref[...] = lookup(bkf_ref[...], by_row)
    tl_ref[...] = lookup(bkl_ref[...], by_row)
    to_ref[...] = lookup(bko_ref[...], by_row)


def bias_tiles(rel_bias, n_past):
    key = np.arange(MOBA_BLOCK)[:, None]
    qry = np.arange(MOBA_BLOCK)[None, :]
    bkp = np.stack([_bucket_np(qry - key), _bucket_np(MOBA_BLOCK + qry - key)])
    r = np.arange(N_HEADS * 8)[:, None]
    rh, rq = r // 8, r % 8
    c = np.arange(PAGE_SIZE * N_HEADS)[None, :]
    ct, ch = c // N_HEADS, c % N_HEADS
    bkf = np.where(rh == ch, NUM_BUCKETS - 1, -1)
    bkl = np.where(rh == ch, _bucket_np(n_past + rq - (n_past - PAGE_SIZE + ct)), -1)
    assert PAGE_SIZE >= REL_MAX_DIST
    co = np.arange(128)[None, :]
    coh, cot = co // 8, co % 8
    bko = np.where((rh == coh) & (cot <= rq), _bucket_np(rq - cot), -1)
    rbx = jnp.repeat(rel_bias, 8, axis=1)[:, :, None]
    vm = pl.BlockSpec(memory_space=pltpu.VMEM)
    i32 = lambda a: jnp.asarray(a.astype(np.int32))
    return pl.pallas_call(
        _bias_kernel,
        in_specs=[pl.BlockSpec(memory_space=pltpu.SMEM), vm, vm, vm, vm, vm],
        out_specs=[vm, vm, vm, vm],
        out_shape=[jax.ShapeDtypeStruct((N_HEADS, 2, MOBA_BLOCK, MOBA_BLOCK), F32),
                   jax.ShapeDtypeStruct(bkf.shape, F32),
                   jax.ShapeDtypeStruct(bkl.shape, F32),
                   jax.ShapeDtypeStruct(bko.shape, F32)],
        compiler_params=pltpu.CompilerParams(vmem_limit_bytes=32 * MIB),
        name="bias_tiles",
    )(rel_bias, rbx, i32(bkp), i32(bkf), i32(bkl), i32(bko))


def _split_bf16(x):
    hi = x.astype(BF16)
    lo = (x - hi.astype(F32)).astype(BF16)
    return hi, lo


def _moba_prompt_kernel(q_ref, k_ref, v_ref, bias_ref, c31_ref, o_ref, qbf, kbf, vt, pen_ref, s_ref):
    nb = kbf.shape[0] // MOBA_BLOCK
    scale = HEAD_DIM ** -0.5
    blk = lambda b: slice(b * MOBA_BLOCK, (b + 1) * MOBA_BLOCK)

    q = q_ref[...]
    q_hi, q_lo = _split_bf16(q)
    qbf[...] = q_hi
    kbf[...] = k_ref[...].astype(BF16)
    km = jnp.concatenate([jnp.mean(k_ref[blk(b), :], axis=0, keepdims=True) for b in range(nb)]
                         + [jnp.zeros((16 - nb, HEAD_DIM), F32)], axis=0)
    for b in range(nb):
        vt[b] = v_ref[blk(b), :].T.astype(BF16)
    km_hi, km_lo = _split_bf16(km)
    gate = (lax.dot_general(km_hi, q_hi, NT_DIMS, preferred_element_type=F32)
            + lax.dot_general(km_lo, q_hi, NT_DIMS, preferred_element_type=F32)
            + lax.dot_general(km_hi, q_lo, NT_DIMS, preferred_element_type=F32))

    row = lax.broadcasted_iota(jnp.int32, gate.shape, 0)
    own = lax.broadcasted_iota(jnp.int32, gate.shape, 1) // MOBA_BLOCK
    rank = jnp.zeros(gate.shape, F32)
    for b2 in range(nb):
        gb = gate[b2:b2 + 1, :]
        beats = ((gb > gate) | ((gb == gate) & (b2 < row))) & (b2 < own)
        rank = rank + jnp.where(beats, 1.0, 0.0)
    pen_ref[...] = jnp.where((row < own) & (rank < float(MOBA_TOPK)), 0.0, NEG)

    key = lax.broadcasted_iota(jnp.int32, (MOBA_BLOCK, MOBA_BLOCK), 0)
    qry = lax.broadcasted_iota(jnp.int32, (MOBA_BLOCK, MOBA_BLOCK), 1)
    c31 = c31_ref[pl.program_id(1)]
    for qi in range(nb):
        m = None
        for kb in range(qi + 1):
            s = lax.dot_general(kbf[blk(kb), :], qbf[blk(qi), :], NT_DIMS, preferred_element_type=F32) * scale
            if kb == qi:
                s = jnp.where(key <= qry, s + bias_ref[0, 0], NEG)
            else:
                s = s + (bias_ref[0, 1] if kb == qi - 1 else c31) + pen_ref[kb:kb + 1, blk(qi)]
            s_ref[kb] = s
            cm = jnp.max(s, axis=0, keepdims=True)
            m = cm if m is None else jnp.maximum(m, cm)
        lsum = jnp.zeros((1, MOBA_BLOCK), F32)
        acc = jnp.zeros((HEAD_DIM, MOBA_BLOCK), F32)
        for kb in range(qi + 1):
            p = jnp.exp(s_ref[kb] - m)
            lsum = lsum + jnp.sum(p, axis=0, keepdims=True)
            acc = acc + jnp.dot(vt[kb], p.astype(BF16), preferred_element_type=F32)
        o_ref[blk(qi), :] = (acc / lsum).T.astype(o_ref.dtype)


def moba_prompt(q, k, v, bias_p, c31_h, n_seq, seq):
    nq = seq // MOBA_BLOCK
    tok = pl.BlockSpec((seq, HEAD_DIM), lambda n, h: (n, h))
    return pl.pallas_call(
        _moba_prompt_kernel,
        grid=(n_seq, N_HEADS),
        in_specs=[tok, tok, tok,
                  pl.BlockSpec((1, 2, MOBA_BLOCK, MOBA_BLOCK), lambda n, h: (h, 0, 0, 0)),
                  pl.BlockSpec(memory_space=pltpu.SMEM)],
        out_specs=tok,
        out_shape=jax.ShapeDtypeStruct(q.shape, BF16),
        scratch_shapes=[pltpu.VMEM((seq, HEAD_DIM), BF16), pltpu.VMEM((seq, HEAD_DIM), BF16),
                        pltpu.VMEM((nq, HEAD_DIM, MOBA_BLOCK), BF16), pltpu.VMEM((16, seq), F32),
                        pltpu.VMEM((nq, MOBA_BLOCK, MOBA_BLOCK), F32)],
        compiler_params=_cparams(2, 48),
        name="moba_prompt",
    )(q, k, v, bias_p, c31_h)


def _by_head_rows(x):
    return jnp.concatenate([x[:, h * HEAD_DIM:(h + 1) * HEAD_DIM] for h in range(x.shape[1] // HEAD_DIM)], axis=0)


def _by_head_lanes(x, n_heads):
    t = x.shape[0] // n_heads
    return jnp.concatenate([x[h * t:(h + 1) * t] for h in range(n_heads)], axis=1)


def _moba_sample_kernel(pt_ref, q_ref, kn_ref, vn_ref, tf_ref, tl_ref, to_ref, *refs, n_pages, n_route):
    kp, vp = refs[:n_pages], refs[n_pages:2 * n_pages]
    refs = refs[2 * n_pages:]
    if n_route:
        qh_ref, sk_ref, o_ref, cnt_ref, p1_ref, rk2_ref, e2_ref, s_ref = refs
    else:
        o_ref, s_ref = refs
    ppb = MOBA_BLOCK // PAGE_SIZE
    nb = n_pages // ppb
    scale = HEAD_DIM ** -0.5
    t_new = q_ref.shape[1]
    nr = N_HEADS * t_new

    route = _RouteUnits(qh_ref, sk_ref, (cnt_ref, p1_ref, rk2_ref, e2_ref), n_route) if n_route else None
    interleave = (lambda: route.advance(ROUTE_STEPS_PER_PAGE)) if n_route else (lambda: None)

    q = _by_head_rows(q_ref[0])
    q_bf = q.astype(BF16)
    zpad = jnp.zeros((128 - nr, HEAD_DIM), F32)
    kn = jnp.concatenate([_by_head_rows(kn_ref[0]), zpad], axis=0).astype(BF16)
    vn = jnp.concatenate([_by_head_rows(vn_ref[0]), zpad], axis=0).astype(BF16)

    gates = []
    for b in range(nb):
        ksum = jnp.zeros((N_HEADS, HEAD_DIM), F32)
        for pg in range(ppb):
            ksum = ksum + jnp.sum(kp[b * ppb + pg][...].reshape(PAGE_SIZE, N_HEADS, HEAD_DIM), axis=0)
        kmean = ksum / float(MOBA_BLOCK)
        krep = jnp.concatenate([jnp.broadcast_to(kmean[h:h + 1, :], (t_new, HEAD_DIM)) for h in range(N_HEADS)],
                               axis=0)
        gates.append(jnp.sum(q * krep, axis=1, keepdims=True))

    penalty = []
    for b in range(nb):
        rank = jnp.zeros((nr, 1), F32)
        for b2 in range(nb):
            if b2 != b:
                beats = (gates[b2] > gates[b]) | ((gates[b2] == gates[b]) & (b2 < b))
                rank = rank + jnp.where(beats, 1.0, 0.0)
        penalty.append(jnp.where(rank < float(MOBA_TOPK), 0.0, NEG))

    s_own = lax.dot_general(q_bf, kn, NT_DIMS, preferred_element_type=F32) * scale + to_ref[...]
    m = jnp.max(s_own, axis=-1, keepdims=True)
    for pg in range(n_pages):
        s = lax.dot_general(q_bf, kp[pg][...].astype(BF16), NT_DIMS, preferred_element_type=F32)
        bias = tl_ref[...] if pg == n_pages - 1 else tf_ref[...]
        s = s * scale + bias + penalty[pg // ppb]
        s_ref[pg] = s
        m = jnp.maximum(m, jnp.max(s, axis=-1, keepdims=True))
        interleave()
    p = jnp.exp(s_own - m)
    lsum = jnp.sum(p, axis=-1, keepdims=True)
    acc = jnp.dot(p.astype(BF16), vn, preferred_element_type=F32)
    for pg in range(n_pages):
        p = jnp.exp(s_ref[pg] - m)
        lsum = lsum + jnp.sum(p, axis=-1, keepdims=True)
        acc = acc + jnp.dot(p.astype(BF16), vp[pg][...].astype(BF16), preferred_element_type=F32)
        interleave()
    o_ref[0] = _by_head_lanes(acc / lsum, N_HEADS)

    if n_route:
        route.finish()


def moba_sample(q, kn, vn, ck, cv, page_table, t_far, t_last, t_own, route=None):
    nseq, n_pages = page_table.shape
    tok = pl.BlockSpec((1,) + q.shape[1:], lambda b, pt: (b, 0, 0))

    def page_spec(p):
        return pl.BlockSpec((None,) + ck.shape[1:], lambda b, pt, p=p: (pt[b, p], 0, 0))

    const = lambda a: pl.BlockSpec(a.shape, lambda b, pt: (0, 0))
    in_specs = ([tok, tok, tok, const(t_far), const(t_last), const(t_own)]
                + [page_spec(p) for p in range(n_pages)] * 2)
    out_specs, out_shape, extra, n_route = [tok], [jax.ShapeDtypeStruct(q.shape, F32)], [], 0
    if route is not None:
        qh, sk_bf = route
        t = qh.shape[1]
        units = PEER_HEADS * (t // ROUTE_TQ)
        n_route = units // nseq
        assert n_route * nseq == units and PEER_HEADS % n_route == 0
        hps = PEER_HEADS // n_route
        in_specs += [pl.BlockSpec((n_route, ROUTE_TQ, PEER_QDIM), lambda b, pt: (b % hps, b // hps, 0)),
                     pl.BlockSpec((n_route, 2, PEER_NKEYS, PEER_QDIM // 2), lambda b, pt: (b % hps, 0, 0, 0))]
        tab = pl.BlockSpec((n_route, PEER_NKEYS, ROUTE_TQ), lambda b, pt: (b % hps, 0, b // hps))
        out_specs += [tab] * 4
        out_shape += [jax.ShapeDtypeStruct((PEER_HEADS, PEER_NKEYS, t), dt) for dt in (F32, F32, BF16, BF16)]
        extra = [qh, sk_bf]
    grid_spec = pltpu.PrefetchScalarGridSpec(
        num_scalar_prefetch=1, grid=(nseq,), in_specs=in_specs, out_specs=out_specs,
        scratch_shapes=[pltpu.VMEM((n_pages, N_HEADS * q.shape[1], ck.shape[1]), F32)])
    return pl.pallas_call(
        functools.partial(_moba_sample_kernel, n_pages=n_pages, n_route=n_route),
        grid_spec=grid_spec,
        out_shape=out_shape,
        compiler_params=_cparams(1, 56),
        name="moba_sample",
    )(page_table, q, kn, vn, t_far, t_last, t_own, *([ck] * n_pages), *([cv] * n_pages), *extra)


HIST_ROWS = 32
HIST_OFF = HIST_ROWS - (CONV_K - 1)


def _conv_kernel(ga_ref, gg_ref, hist_ref, w_ref, b_ref, lg_ref, lb_ref, o_ref, nb_ref, ext_ref, y_ref,
                 *, tt, n_t, sb):
    t = pl.program_id(1)

    @pl.when(t == 0)
    def _():
        for s in range(sb):
            ext_ref[s, HIST_OFF:HIST_ROWS, :] = hist_ref[s]

    rt = min(tt, 128)
    for s in range(sb):
        rows = slice(s * tt, (s + 1) * tt)
        ext_ref[s, HIST_ROWS:HIST_ROWS + tt, :] = ga_ref[rows, :] * jax.nn.sigmoid(gg_ref[rows, :])
        for c in range(CONV_CH // 128):
            cs = slice(c * 128, (c + 1) * 128)
            for r0 in range(0, tt, rt):
                acc = jnp.zeros((rt, 128), F32)
                for res in range(8):
                    taps = [j for j in range(CONV_K) if (HIST_OFF + j) % 8 == res]
                    q0 = (HIST_OFF + taps[0]) // 8
                    q1 = (HIST_OFF + taps[-1]) // 8
                    if res:
                        win = ext_ref[s, r0 + 8 * q0:r0 + 8 * (q1 + 1) + rt, cs]
                        win = pltpu.roll(win, win.shape[0] - res, axis=0)
                    else:
                        win = ext_ref[s, r0 + 8 * q0:r0 + 8 * q1 + rt, cs]
                    for j in taps:
                        off = 8 * ((HIST_OFF + j) // 8 - q0)
                        acc = acc + w_ref[j:j + 1, cs] * win[off:off + rt]
                y_ref[s * tt + r0:s * tt + r0 + rt, cs] = acc + b_ref[:, cs]
    y = y_ref[...]
    mu = jnp.mean(y, axis=-1, keepdims=True)
    yc = y - mu
    var = jnp.mean(yc * yc, axis=-1, keepdims=True)
    yn = yc * lax.rsqrt(var + EPS) * lg_ref[...] + lb_ref[...]
    o_ref[...] = (yn * jax.nn.sigmoid(yn)).astype(o_ref.dtype)

    @pl.when(t == n_t - 1)
    def _():
        for s in range(sb):
            nb_ref[s] = ext_ref[s, tt + HIST_OFF:tt + HIST_ROWS, :]

    if n_t > 1:
        for s in range(sb):
            ext_ref[s, 0:HIST_ROWS, :] = ext_ref[s, tt:tt + HIST_ROWS, :]


def conformer_conv(ga, gg, hist, w_dw, b_dw, ln_g, ln_b, n_seq, seq, tt, sb):
    n_t = seq // tt
    assert sb == 1 or n_t == 1
    row = lambda a: a.reshape(1, CONV_CH)
    cvec = pl.BlockSpec((1, CONV_CH), lambda n, t: (0, 0))
    tile = pl.BlockSpec((sb * tt, CONV_CH), lambda n, t: (n * n_t + t, 0))
    hist_spec = pl.BlockSpec((sb, CONV_K - 1, CONV_CH), lambda n, t: (n, 0, 0))
    return pl.pallas_call(
        functools.partial(_conv_kernel, tt=tt, n_t=n_t, sb=sb),
        grid=(n_seq // sb, n_t),
        in_specs=[tile, tile, hist_spec, pl.BlockSpec((CONV_K, CONV_CH), lambda n, t: (0, 0)),
                  cvec, cvec, cvec],
        out_specs=[tile, hist_spec],
        out_shape=[jax.ShapeDtypeStruct((n_seq * seq, CONV_CH), BF16),
                   jax.ShapeDtypeStruct((n_seq, CONV_K - 1, CONV_CH), F32)],
        scratch_shapes=[pltpu.VMEM((sb, HIST_ROWS + tt, CONV_CH), F32), pltpu.VMEM((sb * tt, CONV_CH), F32)],
        compiler_params=_cparams(2, 32),
        name="conformer_conv",
    )(ga, gg, hist, w_dw, row(b_dw), row(ln_g), row(ln_b))


def _out_proj_kernel(x_ref, a_ref, c_ref, w_ref, o_ref):
    wa = a_ref.shape[1]
    y = jnp.dot(a_ref[...].astype(BF16), w_ref[0:wa, :], preferred_element_type=F32)
    y = y + jnp.dot(c_ref[...].astype(BF16), w_ref[wa:, :], preferred_element_type=F32)
    o_ref[...] = x_ref[...] + y


def out_proj(x, a, c, w_bf, tm):
    t, d = x.shape
    return pl.pallas_call(
        _out_proj_kernel,
        grid=(t // tm,),
        in_specs=[pl.BlockSpec((tm, d), lambda i: (i, 0)),
                  pl.BlockSpec((tm, a.shape[1]), lambda i: (i, 0)),
                  pl.BlockSpec((tm, c.shape[1]), lambda i: (i, 0)),
                  pl.BlockSpec(w_bf.shape, lambda i: (0, 0))],
        out_specs=pl.BlockSpec((tm, d), lambda i: (i, 0)),
        out_shape=jax.ShapeDtypeStruct((t, d), F32),
        compiler_params=_cparams(1, 48),
        name="out_proj",
    )(x, a, c, w_bf)


def _softmax_pv(s, mv_bf):
    m = jnp.max(s, axis=-1, keepdims=True)
    p = jnp.exp(s - m)
    o = jnp.dot(p.astype(BF16), mv_bf, preferred_element_type=F32)
    return o / jnp.sum(p, axis=-1, keepdims=True)


def _cross_kernel(h_ref, g_ref, wq_ref, wo_ref, mk_ref, mv_ref, o_ref, *, n_grp, rows):
    scale = HEAD_DIM ** -0.5
    h = h_ref[...]
    hn = _rms(h, g_ref[...]).astype(BF16)
    q = jnp.dot(hn, wq_ref[...], preferred_element_type=F32)
    outs = []
    for g in range(n_grp):
        mk = mk_ref[g].astype(BF16)
        mv = mv_ref[g].astype(BF16)
        qg = q[g * rows:(g + 1) * rows]
        if rows >= 128:
            heads = []
            for hd in range(X_HEADS):
                cs = slice(hd * HEAD_DIM, (hd + 1) * HEAD_DIM)
                s = lax.dot_general(qg[:, cs].astype(BF16), mk[:, cs], NT_DIMS, preferred_element_type=F32)
                heads.append(_softmax_pv(s * scale, mv[:, cs]))
            outs.append(jnp.concatenate(heads, axis=1))
        else:
            qx = _by_head_rows(qg).astype(BF16)
            s = lax.dot_general(qx, mk, NT_DIMS, preferred_element_type=F32) * scale
            rowh = lax.broadcasted_iota(jnp.int32, s.shape, 0) // rows
            colh = lax.broadcasted_iota(jnp.int32, s.shape, 1) % X_HEADS
            o = _softmax_pv(jnp.where(rowh == colh, s, NEG), mv)
            outs.append(_by_head_lanes(o, X_HEADS))
    o_all = outs[0] if n_grp == 1 else jnp.concatenate(outs, axis=0)
    o_ref[...] = h + jnp.dot(o_all.astype(BF16), wo_ref[...], preferred_element_type=F32)


def cross_block(h, g, wq_bf, wo_bf, mk, mv, n_grp, rows, tiles_per_seq):
    t, d = h.shape
    tm = n_grp * rows
    if n_grp == 1:
        mem_map = lambda i: (i // tiles_per_seq, 0, 0)
    else:
        mem_map = lambda i: (i, 0, 0)
    mem_spec = pl.BlockSpec((n_grp,) + mk.shape[1:], mem_map)
    return pl.pallas_call(
        functools.partial(_cross_kernel, n_grp=n_grp, rows=rows),
        grid=(t // tm,),
        in_specs=[pl.BlockSpec((tm, d), lambda i: (i, 0)),
                  pl.BlockSpec((1, d), lambda i: (0, 0)),
                  pl.BlockSpec(wq_bf.shape, lambda i: (0, 0)),
                  pl.BlockSpec(wo_bf.shape, lambda i: (0, 0)),
                  mem_spec, mem_spec],
        out_specs=pl.BlockSpec((tm, d), lambda i: (i, 0)),
        out_shape=jax.ShapeDtypeStruct((t, d), F32),
        compiler_params=_cparams(1, 48),
        name="cross_block",
    )(h, g.reshape(1, d), wq_bf, wo_bf, mk, mv)


def _drain(steps):
    try:
        while True:
            next(steps)
    except StopIteration as done:
        return done.value


def _top16(s, exact_ties, want_rank=True):
    n, t = s.shape
    row = lax.broadcasted_iota(jnp.int32, (n, t), 0).astype(F32)
    row16 = lax.broadcasted_iota(jnp.int32, (PEER_TOPK, t), 0)
    rank = jnp.full((n, t), float(PEER_TOPK), F32)
    vals = jnp.zeros((PEER_TOPK, t), F32)
    for r in range(PEER_TOPK):
        m = jnp.max(s, axis=0, keepdims=True)
        if exact_ties:
            idx = jnp.min(jnp.where(s == m, row, float(n)), axis=0, keepdims=True)
            hit = row == idx
        else:
            hit = s == m
        if want_rank:
            rank = jnp.where(hit, float(r), rank)
        s = jnp.where(hit, -jnp.inf, s)
        vals = jnp.where(row16 == r, m, vals)
        yield
    return vals, rank


def _pair_pieces(v1, v2, e1, e2):
    t = v1.shape[1]
    sub = lax.broadcasted_iota(jnp.int32, (8, t), 0)
    subf = sub.astype(F32)
    pieces = []

    def col(b, a0, a_max):
        a = sub + a0
        pieces.append(dict(c=v1[a0:a0 + 8] + v2[b:b + 1], e=e1[a0:a0 + 8] * e2[b:b + 1],
                           f=(subf + a0) * 16.0 + b, ok=a <= a_max, a0=a0, row_a=None))

    def rowp(a, b0, b_min, b_max):
        b = sub + b0
        pieces.append(dict(c=v1[a:a + 1] + v2[b0:b0 + 8], e=e1[a:a + 1] * e2[b0:b0 + 8],
                           f=a * 16.0 + (subf + b0), ok=(b >= b_min) & (b <= b_max), a0=None, row_a=a))

    col(0, 0, 15), col(0, 8, 15), col(1, 0, 7), col(2, 0, 4), col(3, 0, 3)
    rowp(0, 8, 8, 15), rowp(0, 0, 4, 7), rowp(1, 0, 4, 7), rowp(2, 0, 4, 4)
    for p in pieces:
        p["c"] = jnp.where(p["ok"], p["c"], -jnp.inf)
        p["f"] = jnp.where(p["ok"], p["f"], -1.0)
    return pieces


def _route_head(s1, s2, exact_ties):
    tq = s1.shape[1]
    v1, rank1 = yield from _top16(s1, exact_ties, want_rank=exact_ties)
    v2, rank2 = yield from _top16(s2, exact_ties)
    e1 = jnp.exp(v1 - v1[0:1])
    e2 = jnp.exp(v2 - v2[0:1])
    pieces = _pair_pieces(v1, v2, e1, e2)

    taken = [jnp.zeros((8, tq), F32) for _ in pieces]
    cs = [p["c"] for p in pieces]
    for _ in range(PEER_TOPK):
        m = cs[0]
        for c in cs[1:]:
            m = jnp.maximum(m, c)
        m = jnp.max(m, axis=0, keepdims=True)
        if exact_ties:
            fm = None
            for c, p in zip(cs, pieces):
                cand = jnp.where(c == m, p["f"], 1e9)
                fm = cand if fm is None else jnp.minimum(fm, cand)
            fm = jnp.min(fm, axis=0, keepdims=True)
        for i, p in enumerate(pieces):
            hit = (p["f"] == fm) if exact_ties else (cs[i] == m)
            taken[i] = jnp.where(hit, 1.0, taken[i])
            cs[i] = jnp.where(hit, -jnp.inf, cs[i])
        yield

    row16 = lax.broadcasted_iota(jnp.int32, (PEER_TOPK, tq), 0)
    n1 = jnp.zeros((PEER_TOPK, tq), F32)
    z = jnp.zeros((1, tq), F32)
    for tf, p in zip(taken, pieces):
        z = z + jnp.sum(tf * p["e"], axis=0, keepdims=True)
        if p["row_a"] is None:
            pad = jnp.zeros((8, tq), F32)
            n1 = n1 + (jnp.concatenate([tf, pad], axis=0) if p["a0"] == 0
                       else jnp.concatenate([pad, tf], axis=0))
        else:
            n1 = n1 + jnp.where(row16 == p["row_a"], jnp.sum(tf, axis=0, keepdims=True), 0.0)

    in2 = rank2 < float(PEER_TOPK)
    cnt1 = jnp.zeros(s1.shape, F32)
    if exact_ties:
        in1 = rank1 < float(PEER_TOPK)
        for r in range(PEER_TOPK):
            cnt1 = jnp.where(rank1 == float(r), n1[r:r + 1], cnt1)
    else:
        in1 = s1 >= v1[PEER_TOPK - 1:PEER_TOPK]
        for r in range(PEER_TOPK):
            cnt1 = jnp.where(s1 == v1[r:r + 1], n1[r:r + 1], cnt1)
    p1 = jnp.where(in1, jnp.exp(s1 - v1[0:1]) / z, 0.0)
    e2_dense = jnp.where(in2, jnp.exp(s2 - v2[0:1]), 0.0)
    winners = (jnp.sum(jnp.where(in1, 1.0, 0.0), axis=0, keepdims=True)
               + jnp.sum(jnp.where(in2, 1.0, 0.0), axis=0, keepdims=True)
               + jnp.sum(n1, axis=0, keepdims=True))
    return cnt1, p1, rank2, e2_dense, winners


def _route_unit(qh, sk, tabs, exact_ties):
    half = PEER_QDIM // 2
    s1 = lax.dot_general(sk[0], qh[:, :half], NT_DIMS, preferred_element_type=F32)
    s2 = lax.dot_general(sk[1], qh[:, half:], NT_DIMS, preferred_element_type=F32)
    cnt1, p1, rank2, e2_dense, winners = yield from _route_head(s1, s2, exact_ties)
    cnt_ref, p1_ref, rk2_ref, e2_ref = tabs
    cnt_ref[...] = cnt1
    p1_ref[...] = p1
    rk2_ref[...] = rank2.astype(BF16)
    e2_ref[...] = e2_dense.astype(BF16)
    return winners


class _RouteUnits:
    def __init__(self, qh_ref, sk_ref, tabs, n):
        self.args = [(qh_ref, sk_ref, [t.at[u] for t in tabs], u) for u in range(n)]
        self.steps = [_route_unit(q[u], k[u], t, False) for q, k, t, u in self.args]
        self.winners = [None] * n

    def advance(self, k):
        for u, steps in enumerate(self.steps):
            if self.winners[u] is None:
                try:
                    for _ in range(k):
                        next(steps)
                except StopIteration as done:
                    self.winners[u] = done.value

    def finish(self):
        while any(w is None for w in self.winners):
            self.advance(1)
        for (q, k, t, u), w in zip(self.args, self.winners):
            @pl.when(jnp.max(w) > float(3 * PEER_TOPK))
            def _(q=q, k=k, t=t, u=u):
                _drain(_route_unit(q[u], k[u], t, True))


def _peer_query_kernel(h_ref, g_ref, wpq_ref, xnt_ref, qh_ref):
    hn = _rms(h_ref[...], g_ref[...])
    xnt_ref[...] = hn.T.astype(BF16)
    q = jnp.dot(hn.astype(BF16), wpq_ref[...], preferred_element_type=F32)
    for hh in range(PEER_HEADS):
        qh_ref[hh] = q[:, hh * PEER_QDIM:(hh + 1) * PEER_QDIM].astype(BF16)


def peer_query(h, g, wpq_bf):
    t, d = h.shape
    tq = QUERY_TQ
    return pl.pallas_call(
        _peer_query_kernel,
        grid=(t // tq,),
        in_specs=[pl.BlockSpec((tq, d), lambda i: (i, 0)),
                  pl.BlockSpec((1, d), lambda i: (0, 0)),
                  pl.BlockSpec(wpq_bf.shape, lambda i: (0, 0), pipeline_mode=pl.Buffered(1))],
        out_specs=[pl.BlockSpec((d, tq), lambda i: (0, i)),
                   pl.BlockSpec((PEER_HEADS, tq, PEER_QDIM), lambda i: (0, i, 0))],
        out_shape=[jax.ShapeDtypeStruct((d, t), BF16), jax.ShapeDtypeStruct((PEER_HEADS, t, PEER_QDIM), BF16)],
        compiler_params=_cparams(1, 48),
        name="peer_query",
    )(h, g.reshape(1, d), wpq_bf)


def _peer_route_kernel(qh_ref, sk_ref, cnt_ref, p1_ref, rk2_ref, e2_ref, *, n_route):
    _RouteUnits(qh_ref, sk_ref, (cnt_ref, p1_ref, rk2_ref, e2_ref), n_route).finish()


def peer_route(qh, sk_bf, n_route=2):
    t = qh.shape[1]
    hps = PEER_HEADS // n_route
    tab = pl.BlockSpec((n_route, PEER_NKEYS, ROUTE_TQ), lambda i, j: (j, 0, i))
    return pl.pallas_call(
        functools.partial(_peer_route_kernel, n_route=n_route),
        grid=(t // ROUTE_TQ, hps),
        in_specs=[pl.BlockSpec((n_route, ROUTE_TQ, PEER_QDIM), lambda i, j: (j, i, 0)),
                  pl.BlockSpec((n_route, 2, PEER_NKEYS, PEER_QDIM // 2), lambda i, j: (j, 0, 0, 0))],
        out_specs=[tab] * 4,
        out_shape=[jax.ShapeDtypeStruct((PEER_HEADS, PEER_NKEYS, t), dt) for dt in (F32, F32, BF16, BF16)],
        compiler_params=_cparams(2, 32),
        name="peer_route",
    )(qh, sk_bf)


def _peer_dense_kernel(xnt_ref, u_ref, vt_ref, cnt_ref, p1_ref, rk2_ref, e2_ref, h_ref, fg_ref, y_ref, acc_ref,
                       *, te, n_e):
    j = pl.program_id(1)

    @pl.when(j == 0)
    def _():
        acc_ref[...] = jnp.zeros(acc_ref.shape, F32)

    tq = xnt_ref.shape[1]
    groups = te // PEER_NKEYS
    gpc = PEER_ROW_CHUNK // PEER_NKEYS
    parts = []
    for a in range(groups):
        if a % gpc == 0:
            rows = slice(a * PEER_NKEYS, (a + gpc) * PEER_NKEYS)
            act = jax.nn.gelu(jnp.dot(u_ref[rows, :], xnt_ref[...], preferred_element_type=F32))
        i1 = j * groups + a
        gate = jnp.zeros((PEER_NKEYS, tq), BF16)
        for hh in range(PEER_HEADS):
            cnt_row = cnt_ref[hh, pl.ds(i1, 1), :].astype(BF16)
            p1_row = p1_ref[hh, pl.ds(i1, 1), :].astype(BF16)
            gate = gate + jnp.where(rk2_ref[hh] < cnt_row, p1_row * e2_ref[hh], jnp.zeros((), BF16))
        lo = (a % gpc) * PEER_NKEYS
        parts.append(act[lo:lo + PEER_NKEYS].astype(BF16) * gate)
    hm_t = parts[0] if groups == 1 else jnp.concatenate(parts, axis=0)
    acc_ref[...] += jnp.dot(vt_ref[...], hm_t, preferred_element_type=F32)

    @pl.when(j == n_e - 1)
    def _():
        y_ref[...] = _rms(h_ref[...] + acc_ref[...].T, fg_ref[...])


def peer_dense(xnt, u_bf, vt_bf, tabs, h, final_g, tq, te):
    t, d = h.shape
    n_e = u_bf.shape[0] // te
    once = dict(pipeline_mode=pl.Buffered(1))
    tab = pl.BlockSpec((PEER_HEADS, PEER_NKEYS, tq), lambda i, j: (0, 0, i), **once)
    return pl.pallas_call(
        functools.partial(_peer_dense_kernel, te=te, n_e=n_e),
        grid=(t // tq, n_e),
        in_specs=[pl.BlockSpec((d, tq), lambda i, j: (0, i), **once),
                  pl.BlockSpec((te, d), lambda i, j: (j, 0)),
                  pl.BlockSpec((d, te), lambda i, j: (0, j)),
                  tab, tab, tab, tab,
                  pl.BlockSpec((tq, d), lambda i, j: (i, 0), **once),
                  pl.BlockSpec((1, d), lambda i, j: (0, 0))],
        out_specs=pl.BlockSpec((tq, d), lambda i, j: (i, 0)),
        out_shape=jax.ShapeDtypeStruct((t, d), F32),
        scratch_shapes=[pltpu.VMEM((d, tq), F32)],
        compiler_params=_cparams(2, 56),
        name="peer_dense",
    )(xnt, u_bf, vt_bf, *tabs, h, final_g.reshape(1, d))


def kernel(x_prompt, x_sample, mem_prompt, cache_k, cache_v, cache_conv, cache_mem_k, cache_mem_v, page_table,
           norm1_g, w_in, rel_bias, conv_w, conv_b, conv_ln_g, conv_ln_b, w_out, norm2_g, mem_norm_g, w_cq,
           w_mk, w_mv, w_co, norm3_g, w_pq, peer_sub_keys, peer_u, peer_v, final_g):
    depth = w_in.shape[0]
    assert depth == 1, "single-layer step"
    n_p, seq, d = x_prompt.shape
    n_s, t_s, _ = x_sample.shape
    l = 0
    bf = lambda a: a.astype(BF16)

    w_in_bf, w_out_bf = bf(w_in[l]), bf(w_out[l])
    w_cq_bf, w_co_bf, w_pq_bf = bf(w_cq[l]), bf(w_co[l]), bf(w_pq[l])
    w_mem_bf = bf(jnp.concatenate([w_mk[l], w_mv[l]], axis=1))
    sk_bf, u_bf, vt_bf = bf(peer_sub_keys[l]), bf(peer_u[l]), bf(peer_v[l].T)

    n_pool, n_pages = cache_k.shape[1], page_table.shape[1]
    bias_p, t_far, t_last, t_own = bias_tiles(rel_bias, n_pages * PAGE_SIZE)
    c31_h = rel_bias[NUM_BUCKETS - 1]

    def cross_and_query(h, mk, mv, n_grp, rows, tiles_per_seq):
        h = cross_block(h, norm2_g[l], w_cq_bf, w_co_bf, mk, mv, n_grp, rows, tiles_per_seq)
        return (h,) + tuple(peer_query(h, norm3_g[l], w_pq_bf))

    experts = lambda xnt, tabs, h: peer_dense(xnt, u_bf, vt_bf, tabs, h, final_g, tq=512, te=1024)

    xp = x_prompt.reshape(n_p * seq, d)
    mk_p, mv_p = rms_matmul(mem_prompt.reshape(n_p * N_MEM, d), mem_norm_g[l], w_mem_bf, 2, tm=512)
    q_p, k_p, v_p, ga_p, gg_p = rms_matmul(xp, norm1_g[l], w_in_bf, 5, tm=256)
    attn_p = moba_prompt(q_p, k_p, v_p, bias_p, c31_h, n_p, seq)
    zero_buf = jnp.zeros((n_p, CONV_K - 1, CONV_CH), F32)
    conv_p, buf_p = conformer_conv(ga_p, gg_p, zero_buf, conv_w[l], conv_b[l], conv_ln_g[l], conv_ln_b[l],
                                   n_p, seq, tt=256, sb=1)
    h_p = out_proj(xp, attn_p, conv_p, w_out_bf, tm=512)
    h_p, xnt_p, qh_p = cross_and_query(h_p, mk_p.reshape(n_p, N_MEM, X_WIDTH), mv_p.reshape(n_p, N_MEM, X_WIDTH),
                                       1, 256, seq // 256)

    xs = x_sample.reshape(n_s * t_s, d)
    q_s, k_s, v_s, ga_s, gg_s = rms_matmul(xs, norm1_g[l], w_in_bf, 5, tm=256)
    tok3 = lambda a: a.reshape(n_s, t_s, ATTN_WIDTH)
    units_p = PEER_HEADS * (n_p * seq // ROUTE_TQ)
    fuse = units_p % n_s == 0 and PEER_HEADS % (units_p // n_s) == 0
    attn_s, *tabs_p = moba_sample(tok3(q_s), tok3(k_s), tok3(v_s),
                                  cache_k[l].reshape(n_pool, PAGE_SIZE * N_HEADS, HEAD_DIM),
                                  cache_v[l].reshape(n_pool, PAGE_SIZE * N_HEADS, HEAD_DIM),
                                  page_table, t_far, t_last, t_own, route=(qh_p, sk_bf) if fuse else None)
    if not fuse:
        tabs_p = peer_route(qh_p, sk_bf)
    y_p = experts(xnt_p, tabs_p, h_p)
    conv_s, buf_s = conformer_conv(ga_s, gg_s, cache_conv[l], conv_w[l], conv_b[l], conv_ln_g[l], conv_ln_b[l],
                                   n_s, t_s, tt=t_s, sb=8)
    h_s = out_proj(xs, attn_s.reshape(n_s * t_s, ATTN_WIDTH), conv_s, w_out_bf, tm=512)
    h_s, xnt_s, qh_s = cross_and_query(h_s, cache_mem_k[l].reshape(n_s, N_MEM * X_HEADS, HEAD_DIM),
                                       cache_mem_v[l].reshape(n_s, N_MEM * X_HEADS, HEAD_DIM), 8, t_s, 1)
    y_s = experts(xnt_s, peer_route(qh_s, sk_bf), h_s)

    n_pg = seq // PAGE_SIZE
    return (y_p.reshape(n_p, seq, d), y_s.reshape(n_s, t_s, d),
            k_p.reshape(1, n_p, n_pg, PAGE_SIZE, N_HEADS, HEAD_DIM),
            v_p.reshape(1, n_p, n_pg, PAGE_SIZE, N_HEADS, HEAD_DIM),
            buf_p[None],
            mk_p.reshape(1, n_p, N_MEM, X_HEADS, HEAD_DIM), mv_p.reshape(1, n_p, N_MEM, X_HEADS, HEAD_DIM),
            k_s.reshape(1, n_s, t_s, N_HEADS, HEAD_DIM), v_s.reshape(1, n_s, t_s, N_HEADS, HEAD_DIM),
            buf_s[None])
```

```python
import functools
import math

import numpy as np
import jax
import jax.numpy as jnp
from jax import lax
from jax.experimental import pallas as pl
from jax.experimental.pallas import tpu as pltpu

F32 = jnp.float32
BF16 = jnp.bfloat16

D_MODEL = 2048
HEAD_DIM = 128
N_HEADS = 8
ATTN_WIDTH = N_HEADS * HEAD_DIM
CONV_CH = 1024
MOBA_BLOCK = 256
MOBA_TOPK = 3
PAGE_SIZE = 128
NUM_BUCKETS = 32
MAX_EXACT = 16
REL_MAX_DIST = 128
CONV_K = 31
N_MEM = 256
X_HEADS = 4
X_WIDTH = X_HEADS * HEAD_DIM
PEER_HEADS = 8
PEER_NKEYS = 128
PEER_TOPK = 16
PEER_QDIM = 256
PEER_ROW_CHUNK = 512
ROUTE_TQ = 256
QUERY_TQ = 256
ROUTE_STEPS_PER_PAGE = 2
EPS = 1e-6
NEG = -1e30

IN_PROJ_ROWS = 256
MEM_PROJ_ROWS = 512
CONV_ROWS = 256
SHORT_SEQS_PER_STEP = 8
CROSS_ROWS = 256
PEER_TOKEN_TILE = 512
PEER_EXPERT_TILE = 1024

MIB = 1024 * 1024
NT_DIMS = (((1,), (1,)), ((), ()))


def _cparams(n_grid, vmem_mib):
    return pltpu.CompilerParams(dimension_semantics=("arbitrary",) * n_grid,
                                vmem_limit_bytes=vmem_mib * MIB)


def _rms(x, g):
    return x * lax.rsqrt(jnp.mean(x * x, axis=-1, keepdims=True) + EPS) * g


def _rms_matmul_kernel(x_ref, g_ref, w_ref, *outs):
    xn = _rms(x_ref[...], g_ref[...]).astype(BF16)
    tn = outs[0].shape[1]
    for s, o_ref in enumerate(outs):
        o_ref[...] = jnp.dot(xn, w_ref[:, s * tn:(s + 1) * tn], preferred_element_type=F32)


def rms_matmul(x, g, w_bf, n_out, tm):
    t, d = x.shape
    tn = w_bf.shape[1] // n_out
    return pl.pallas_call(
        _rms_matmul_kernel,
        grid=(t // tm,),
        in_specs=[pl.BlockSpec((tm, d), lambda i: (i, 0)),
                  pl.BlockSpec((1, d), lambda i: (0, 0)),
                  pl.BlockSpec(w_bf.shape, lambda i: (0, 0), pipeline_mode=pl.Buffered(1))],
        out_specs=[pl.BlockSpec((tm, tn), lambda i: (i, 0))] * n_out,
        out_shape=[jax.ShapeDtypeStruct((t, tn), F32)] * n_out,
        compiler_params=_cparams(1, 56),
        name="rms_matmul",
    )(x, g.reshape(1, d), w_bf)


def _bucket_np(rel):
    n = np.maximum(rel, 0)
    nf = np.maximum(n, 1).astype(np.float32)
    large = MAX_EXACT + (np.log(nf / MAX_EXACT) / np.float32(math.log(REL_MAX_DIST / MAX_EXACT))
                         * (NUM_BUCKETS - MAX_EXACT)).astype(np.int32)
    large = np.minimum(large, NUM_BUCKETS - 1)
    return np.where(n < MAX_EXACT, n, large).astype(np.int32)


def _bias_kernel(rb_ref, rbx_ref, bkp_ref, bkf_ref, bkl_ref, bko_ref, tp_ref, tf_ref, tl_ref, to_ref):
    def lookup(bk, table):
        acc = jnp.full(bk.shape, NEG, F32)
        for b in range(NUM_BUCKETS):
            acc = jnp.where(bk == b, table(b), acc)
        return acc

    for h in range(N_HEADS):
        for t in range(2):
            tp_ref[h, t] = lookup(bkp_ref[t], lambda b: rb_ref[b, h])
    by_row = lambda b: rbx_ref[b]
    tf_ref[...] = lookup(bkf_ref[...], by_row)
    tl_ref[...] = lookup(bkl_ref[...], by_row)
    to_ref[...] = lookup(bko_ref[...], by_row)


def bias_tiles(rel_bias, n_past):
    key = np.arange(MOBA_BLOCK)[:, None]
    qry = np.arange(MOBA_BLOCK)[None, :]
    bkp = np.stack([_bucket_np(qry - key), _bucket_np(MOBA_BLOCK + qry - key)])
    r = np.arange(N_HEADS * 8)[:, None]
    rh, rq = r // 8, r % 8
    c = np.arange(PAGE_SIZE * N_HEADS)[None, :]
    ct, ch = c // N_HEADS, c % N_HEADS
    bkf = np.where(rh == ch, NUM_BUCKETS - 1, -1)
    bkl = np.where(rh == ch, _bucket_np(n_past + rq - (n_past - PAGE_SIZE + ct)), -1)
    assert PAGE_SIZE >= REL_MAX_DIST
    co = np.arange(128)[None, :]
    coh, cot = co // 8, co % 8
    bko = np.where((rh == coh) & (cot <= rq), _bucket_np(rq - cot), -1)
    rbx = jnp.repeat(rel_bias, 8, axis=1)[:, :, None]
    vm = pl.BlockSpec(memory_space=pltpu.VMEM)
    i32 = lambda a: jnp.asarray(a.astype(np.int32))
    return pl.pallas_call(
        _bias_kernel,
        in_specs=[pl.BlockSpec(memory_space=pltpu.SMEM), vm, vm, vm, vm, vm],
        out_specs=[vm, vm, vm, vm],
        out_shape=[jax.ShapeDtypeStruct((N_HEADS, 2, MOBA_BLOCK, MOBA_BLOCK), F32),
                   jax.ShapeDtypeStruct(bkf.shape, F32),
                   jax.ShapeDtypeStruct(bkl.shape, F32),
                   jax.ShapeDtypeStruct(bko.shape, F32)],
        compiler_params=pltpu.CompilerParams(vmem_limit_bytes=32 * MIB),
        name="bias_tiles",
    )(rel_bias, rbx, i32(bkp), i32(bkf), i32(bkl), i32(bko))


def _split_bf16(x):
    hi = x.astype(BF16)
    lo = (x - hi.astype(F32)).astype(BF16)
    return hi, lo


def _moba_prompt_kernel(q_ref, k_ref, v_ref, bias_ref, c31_ref, o_ref, qbf, kbf, vt, pen_ref, s_ref):
    nb = kbf.shape[0] // MOBA_BLOCK
    scale = HEAD_DIM ** -0.5
    blk = lambda b: slice(b * MOBA_BLOCK, (b + 1) * MOBA_BLOCK)

    q = q_ref[...]
    q_hi, q_lo = _split_bf16(q)
    qbf[...] = q_hi
    kbf[...] = k_ref[...].astype(BF16)
    km = jnp.concatenate([jnp.mean(k_ref[blk(b), :], axis=0, keepdims=True) for b in range(nb)]
                         + [jnp.zeros((16 - nb, HEAD_DIM), F32)], axis=0)
    for b in range(nb):
        vt[b] = v_ref[blk(b), :].T.astype(BF16)
    km_hi, km_lo = _split_bf16(km)
    gate = (lax.dot_general(km_hi, q_hi, NT_DIMS, preferred_element_type=F32)
            + lax.dot_general(km_lo, q_hi, NT_DIMS, preferred_element_type=F32)
            + lax.dot_general(km_hi, q_lo, NT_DIMS, preferred_element_type=F32))

    row = lax.broadcasted_iota(jnp.int32, gate.shape, 0)
    own = lax.broadcasted_iota(jnp.int32, gate.shape, 1) // MOBA_BLOCK
    rank = jnp.zeros(gate.shape, F32)
    for b2 in range(nb):
        gb = gate[b2:b2 + 1, :]
        beats = ((gb > gate) | ((gb == gate) & (b2 < row))) & (b2 < own)
        rank = rank + jnp.where(beats, 1.0, 0.0)
    pen_ref[...] = jnp.where((row < own) & (rank < float(MOBA_TOPK)), 0.0, NEG)

    key = lax.broadcasted_iota(jnp.int32, (MOBA_BLOCK, MOBA_BLOCK), 0)
    qry = lax.broadcasted_iota(jnp.int32, (MOBA_BLOCK, MOBA_BLOCK), 1)
    c31 = c31_ref[pl.program_id(1)]
    for qi in range(nb):
        m = None
        for kb in range(qi + 1):
            s = lax.dot_general(kbf[blk(kb), :], qbf[blk(qi), :], NT_DIMS, preferred_element_type=F32) * scale
            if kb == qi:
                s = jnp.where(key <= qry, s + bias_ref[0, 0], NEG)
            else:
                s = s + (bias_ref[0, 1] if kb == qi - 1 else c31) + pen_ref[kb:kb + 1, blk(qi)]
            s_ref[kb] = s
            cm = jnp.max(s, axis=0, keepdims=True)
            m = cm if m is None else jnp.maximum(m, cm)
        lsum = jnp.zeros((1, MOBA_BLOCK), F32)
        acc = jnp.zeros((HEAD_DIM, MOBA_BLOCK), F32)
        for kb in range(qi + 1):
            p = jnp.exp(s_ref[kb] - m)
            lsum = lsum + jnp.sum(p, axis=0, keepdims=True)
            acc = acc + jnp.dot(vt[kb], p.astype(BF16), preferred_element_type=F32)
        o_ref[blk(qi), :] = (acc / lsum).T.astype(o_ref.dtype)


def moba_prompt(q, k, v, bias_p, c31_h, n_seq, seq):
    nq = seq // MOBA_BLOCK
    tok = pl.BlockSpec((seq, HEAD_DIM), lambda n, h: (n, h))
    return pl.pallas_call(
        _moba_prompt_kernel,
        grid=(n_seq, N_HEADS),
        in_specs=[tok, tok, tok,
                  pl.BlockSpec((1, 2, MOBA_BLOCK, MOBA_BLOCK), lambda n, h: (h, 0, 0, 0)),
                  pl.BlockSpec(memory_space=pltpu.SMEM)],
        out_specs=tok,
        out_shape=jax.ShapeDtypeStruct(q.shape, BF16),
        scratch_shapes=[pltpu.VMEM((seq, HEAD_DIM), BF16), pltpu.VMEM((seq, HEAD_DIM), BF16),
                        pltpu.VMEM((nq, HEAD_DIM, MOBA_BLOCK), BF16), pltpu.VMEM((16, seq), F32),
                        pltpu.VMEM((nq, MOBA_BLOCK, MOBA_BLOCK), F32)],
        compiler_params=_cparams(2, 48),
        name="moba_prompt",
    )(q, k, v, bias_p, c31_h)


def _by_head_rows(x):
    return jnp.concatenate([x[:, h * HEAD_DIM:(h + 1) * HEAD_DIM] for h in range(x.shape[1] // HEAD_DIM)], axis=0)


def _by_head_lanes(x, n_heads):
    t = x.shape[0] // n_heads
    return jnp.concatenate([x[h * t:(h + 1) * t] for h in range(n_heads)], axis=1)


def _moba_sample_kernel(pt_ref, q_ref, kn_ref, vn_ref, tf_ref, tl_ref, to_ref, *refs, n_pages, n_route):
    kp, vp = refs[:n_pages], refs[n_pages:2 * n_pages]
    refs = refs[2 * n_pages:]
    if n_route:
        qh_ref, sk_ref, o_ref, cnt_ref, p1_ref, rk2_ref, e2_ref, s_ref = refs
    else:
        o_ref, s_ref = refs
    ppb = MOBA_BLOCK // PAGE_SIZE
    nb = n_pages // ppb
    scale = HEAD_DIM ** -0.5
    t_new = q_ref.shape[1]
    nr = N_HEADS * t_new

    route = _RouteUnits(qh_ref, sk_ref, (cnt_ref, p1_ref, rk2_ref, e2_ref), n_route) if n_route else None
    interleave = (lambda: route.advance(ROUTE_STEPS_PER_PAGE)) if n_route else (lambda: None)

    q = _by_head_rows(q_ref[0])
    q_bf = q.astype(BF16)
    zpad = jnp.zeros((128 - nr, HEAD_DIM), F32)
    kn = jnp.concatenate([_by_head_rows(kn_ref[0]), zpad], axis=0).astype(BF16)
    vn = jnp.concatenate([_by_head_rows(vn_ref[0]), zpad], axis=0).astype(BF16)

    gates = []
    for b in range(nb):
        ksum = jnp.zeros((N_HEADS, HEAD_DIM), F32)
        for pg in range(ppb):
            ksum = ksum + jnp.sum(kp[b * ppb + pg][...].reshape(PAGE_SIZE, N_HEADS, HEAD_DIM), axis=0)
        kmean = ksum / float(MOBA_BLOCK)
        krep = jnp.concatenate([jnp.broadcast_to(kmean[h:h + 1, :], (t_new, HEAD_DIM)) for h in range(N_HEADS)],
                               axis=0)
        gates.append(jnp.sum(q * krep, axis=1, keepdims=True))

    penalty = []
    for b in range(nb):
        rank = jnp.zeros((nr, 1), F32)
        for b2 in range(nb):
            if b2 != b:
                beats = (gates[b2] > gates[b]) | ((gates[b2] == gates[b]) & (b2 < b))
                rank = rank + jnp.where(beats, 1.0, 0.0)
        penalty.append(jnp.where(rank < float(MOBA_TOPK), 0.0, NEG))

    s_own = lax.dot_general(q_bf, kn, NT_DIMS, preferred_element_type=F32) * scale + to_ref[...]
    m = jnp.max(s_own, axis=-1, keepdims=True)
    for pg in range(n_pages):
        s = lax.dot_general(q_bf, kp[pg][...].astype(BF16), NT_DIMS, preferred_element_type=F32)
        bias = tl_ref[...] if pg == n_pages - 1 else tf_ref[...]
        s = s * scale + bias + penalty[pg // ppb]
        s_ref[pg] = s
        m = jnp.maximum(m, jnp.max(s, axis=-1, keepdims=True))
        interleave()
    p = jnp.exp(s_own - m)
    lsum = jnp.sum(p, axis=-1, keepdims=True)
    acc = jnp.dot(p.astype(BF16), vn, preferred_element_type=F32)
    for pg in range(n_pages):
        p = jnp.exp(s_ref[pg] - m)
        lsum = lsum + jnp.sum(p, axis=-1, keepdims=True)
        acc = acc + jnp.dot(p.astype(BF16), vp[pg][...].astype(BF16), preferred_element_type=F32)
        interleave()
    o_ref[0] = _by_head_lanes(acc / lsum, N_HEADS)

    if n_route:
        route.finish()


def moba_sample(q, kn, vn, ck, cv, page_table, t_far, t_last, t_own, route=None):
    nseq, n_pages = page_table.shape
    tok = pl.BlockSpec((1,) + q.shape[1:], lambda b, pt: (b, 0, 0))

    def page_spec(p):
        return pl.BlockSpec((None,) + ck.shape[1:], lambda b, pt, p=p: (pt[b, p], 0, 0))

    const = lambda a: pl.BlockSpec(a.shape, lambda b, pt: (0, 0))
    in_specs = ([tok, tok, tok, const(t_far), const(t_last), const(t_own)]
                + [page_spec(p) for p in range(n_pages)] * 2)
    out_specs, out_shape, extra, n_route = [tok], [jax.ShapeDtypeStruct(q.shape, F32)], [], 0
    if route is not None:
        qh, sk_bf = route
        t = qh.shape[1]
        units = PEER_HEADS * (t // ROUTE_TQ)
        n_route = units // nseq
        assert n_route * nseq == units and PEER_HEADS % n_route == 0
        hps = PEER_HEADS // n_route
        in_specs += [pl.BlockSpec((n_route, ROUTE_TQ, PEER_QDIM), lambda b, pt: (b % hps, b // hps, 0)),
                     pl.BlockSpec((n_route, 2, PEER_NKEYS, PEER_QDIM // 2), lambda b, pt: (b % hps, 0, 0, 0))]
        tab = pl.BlockSpec((n_route, PEER_NKEYS, ROUTE_TQ), lambda b, pt: (b % hps, 0, b // hps))
        out_specs += [tab] * 4
        out_shape += [jax.ShapeDtypeStruct((PEER_HEADS, PEER_NKEYS, t), dt) for dt in (F32, F32, BF16, BF16)]
        extra = [qh, sk_bf]
    grid_spec = pltpu.PrefetchScalarGridSpec(
        num_scalar_prefetch=1, grid=(nseq,), in_specs=in_specs, out_specs=out_specs,
        scratch_shapes=[pltpu.VMEM((n_pages, N_HEADS * q.shape[1], ck.shape[1]), F32)])
    return pl.pallas_call(
        functools.partial(_moba_sample_kernel, n_pages=n_pages, n_route=n_route),
        grid_spec=grid_spec,
        out_shape=out_shape,
        compiler_params=_cparams(1, 56),
        name="moba_sample",
    )(page_table, q, kn, vn, t_far, t_last, t_own, *([ck] * n_pages), *([cv] * n_pages), *extra)


HIST_ROWS = 32
HIST_OFF = HIST_ROWS - (CONV_K - 1)


def _conv_kernel(ga_ref, gg_ref, hist_ref, w_ref, b_ref, lg_ref, lb_ref, o_ref, nb_ref, ext_ref, y_ref,
                 *, tt, n_t, sb):
    t = pl.program_id(1)

    @pl.when(t == 0)
    def _():
        for s in range(sb):
            ext_ref[s, HIST_OFF:HIST_ROWS, :] = hist_ref[s]

    rt = min(tt, 128)
    for s in range(sb):
        rows = slice(s * tt, (s + 1) * tt)
        ext_ref[s, HIST_ROWS:HIST_ROWS + tt, :] = ga_ref[rows, :] * jax.nn.sigmoid(gg_ref[rows, :])
        for c in range(CONV_CH // 128):
            cs = slice(c * 128, (c + 1) * 128)
            for r0 in range(0, tt, rt):
                acc = jnp.zeros((rt, 128), F32)
                for res in range(8):
                    taps = [j for j in range(CONV_K) if (HIST_OFF + j) % 8 == res]
                    q0 = (HIST_OFF + taps[0]) // 8
                    q1 = (HIST_OFF + taps[-1]) // 8
                    if res:
                        win = ext_ref[s, r0 + 8 * q0:r0 + 8 * (q1 + 1) + rt, cs]
                        win = pltpu.roll(win, win.shape[0] - res, axis=0)
                    else:
                        win = ext_ref[s, r0 + 8 * q0:r0 + 8 * q1 + rt, cs]
                    for j in taps:
                        off = 8 * ((HIST_OFF + j) // 8 - q0)
                        acc = acc + w_ref[j:j + 1, cs] * win[off:off + rt]
                y_ref[s * tt + r0:s * tt + r0 + rt, cs] = acc + b_ref[:, cs]
    y = y_ref[...]
    mu = jnp.mean(y, axis=-1, keepdims=True)
    yc = y - mu
    var = jnp.mean(yc * yc, axis=-1, keepdims=True)
    yn = yc * lax.rsqrt(var + EPS) * lg_ref[...] + lb_ref[...]
    o_ref[...] = (yn * jax.nn.sigmoid(yn)).astype(o_ref.dtype)

    @pl.when(t == n_t - 1)
    def _():
        for s in range(sb):
            nb_ref[s] = ext_ref[s, tt + HIST_OFF:tt + HIST_ROWS, :]

    if n_t > 1:
        for s in range(sb):
            ext_ref[s, 0:HIST_ROWS, :] = ext_ref[s, tt:tt + HIST_ROWS, :]


def conformer_conv(ga, gg, hist, w_dw, b_dw, ln_g, ln_b, n_seq, seq, tt, sb):
    n_t = seq // tt
    assert sb == 1 or n_t == 1
    row = lambda a: a.reshape(1, CONV_CH)
    cvec = pl.BlockSpec((1, CONV_CH), lambda n, t: (0, 0))
    tile = pl.BlockSpec((sb * tt, CONV_CH), lambda n, t: (n * n_t + t, 0))
    hist_spec = pl.BlockSpec((sb, CONV_K - 1, CONV_CH), lambda n, t: (n, 0, 0))
    return pl.pallas_call(
        functools.partial(_conv_kernel, tt=tt, n_t=n_t, sb=sb),
        grid=(n_seq // sb, n_t),
        in_specs=[tile, tile, hist_spec, pl.BlockSpec((CONV_K, CONV_CH), lambda n, t: (0, 0)),
                  cvec, cvec, cvec],
        out_specs=[tile, hist_spec],
        out_shape=[jax.ShapeDtypeStruct((n_seq * seq, CONV_CH), BF16),
                   jax.ShapeDtypeStruct((n_seq, CONV_K - 1, CONV_CH), F32)],
        scratch_shapes=[pltpu.VMEM((sb, HIST_ROWS + tt, CONV_CH), F32), pltpu.VMEM((sb * tt, CONV_CH), F32)],
        compiler_params=_cparams(2, 32),
        name="conformer_conv",
    )(ga, gg, hist, w_dw, row(b_dw), row(ln_g), row(ln_b))


def _softmax_pv(s, mv_bf):
    m = jnp.max(s, axis=-1, keepdims=True)
    p = jnp.exp(s - m)
    o = jnp.dot(p.astype(BF16), mv_bf, preferred_element_type=F32)
    return o / jnp.sum(p, axis=-1, keepdims=True)


def _cross_kernel(x_ref, a_ref, c_ref, wout_ref, g_ref, wq_ref, wo_ref, mk_ref, mv_ref, o_ref, *, n_grp, rows):
    scale = HEAD_DIM ** -0.5
    wa = a_ref.shape[1]
    h = x_ref[...] + jnp.dot(a_ref[...].astype(BF16), wout_ref[0:wa, :], preferred_element_type=F32)
    h = h + jnp.dot(c_ref[...].astype(BF16), wout_ref[wa:, :], preferred_element_type=F32)
    hn = _rms(h, g_ref[...]).astype(BF16)
    q = jnp.dot(hn, wq_ref[...], preferred_element_type=F32)
    outs = []
    for g in range(n_grp):
        mk = mk_ref[g].astype(BF16)
        mv = mv_ref[g].astype(BF16)
        qg = q[g * rows:(g + 1) * rows]
        if rows >= 128:
            heads = []
            for hd in range(X_HEADS):
                cs = slice(hd * HEAD_DIM, (hd + 1) * HEAD_DIM)
                s = lax.dot_general(qg[:, cs].astype(BF16), mk[:, cs], NT_DIMS, preferred_element_type=F32)
                heads.append(_softmax_pv(s * scale, mv[:, cs]))
            outs.append(jnp.concatenate(heads, axis=1))
        else:
            qx = _by_head_rows(qg).astype(BF16)
            s = lax.dot_general(qx, mk, NT_DIMS, preferred_element_type=F32) * scale
            rowh = lax.broadcasted_iota(jnp.int32, s.shape, 0) // rows
            colh = lax.broadcasted_iota(jnp.int32, s.shape, 1) % X_HEADS
            o = _softmax_pv(jnp.where(rowh == colh, s, NEG), mv)
            outs.append(_by_head_lanes(o, X_HEADS))
    o_all = outs[0] if n_grp == 1 else jnp.concatenate(outs, axis=0)
    o_ref[...] = h + jnp.dot(o_all.astype(BF16), wo_ref[...], preferred_element_type=F32)


def cross_block(x, a, c, wout_bf, g, wq_bf, wo_bf, mk, mv, n_grp, rows, tiles_per_seq):
    t, d = x.shape
    tm = n_grp * rows
    resident = lambda w: pl.BlockSpec(w.shape, lambda i: (0, 0), pipeline_mode=pl.Buffered(1))
    rows_of = lambda arr: pl.BlockSpec((tm, arr.shape[1]), lambda i: (i, 0))
    if n_grp == 1:
        mem_map = lambda i: (i // tiles_per_seq, 0, 0)
    else:
        mem_map = lambda i: (i, 0, 0)
    mem_spec = pl.BlockSpec((n_grp,) + mk.shape[1:], mem_map)
    return pl.pallas_call(
        functools.partial(_cross_kernel, n_grp=n_grp, rows=rows),
        grid=(t // tm,),
        in_specs=[rows_of(x), rows_of(a), rows_of(c), resident(wout_bf),
                  pl.BlockSpec((1, d), lambda i: (0, 0)), resident(wq_bf), resident(wo_bf),
                  mem_spec, mem_spec],
        out_specs=rows_of(x),
        out_shape=jax.ShapeDtypeStruct((t, d), F32),
        compiler_params=_cparams(1, 48),
        name="cross_block",
    )(x, a, c, wout_bf, g.reshape(1, d), wq_bf, wo_bf, mk, mv)


def _drain(steps):
    try:
        while True:
            next(steps)
    except StopIteration as done:
        return done.value


def _top16(s, exact_ties, want_rank=True):
    n, t = s.shape
    row = lax.broadcasted_iota(jnp.int32, (n, t), 0).astype(F32)
    row16 = lax.broadcasted_iota(jnp.int32, (PEER_TOPK, t), 0)
    rank = jnp.full((n, t), float(PEER_TOPK), F32)
    vals = jnp.zeros((PEER_TOPK, t), F32)
    for r in range(PEER_TOPK):
        m = jnp.max(s, axis=0, keepdims=True)
        if exact_ties:
            idx = jnp.min(jnp.where(s == m, row, float(n)), axis=0, keepdims=True)
            hit = row == idx
        else:
            hit = s == m
        if want_rank:
            rank = jnp.where(hit, float(r), rank)
        s = jnp.where(hit, -jnp.inf, s)
        vals = jnp.where(row16 == r, m, vals)
        yield
    return vals, rank


def _pair_pieces(v1, v2, e1, e2):
    t = v1.shape[1]
    sub = lax.broadcasted_iota(jnp.int32, (8, t), 0)
    subf = sub.astype(F32)
    pieces = []

    def col(b, a0, a_max):
        a = sub + a0
        pieces.append(dict(c=v1[a0:a0 + 8] + v2[b:b + 1], e=e1[a0:a0 + 8] * e2[b:b + 1],
                           f=(subf + a0) * 16.0 + b, ok=a <= a_max, a0=a0, row_a=None))

    def rowp(a, b0, b_min, b_max):
        b = sub + b0
        pieces.append(dict(c=v1[a:a + 1] + v2[b0:b0 + 8], e=e1[a:a + 1] * e2[b0:b0 + 8],
                           f=a * 16.0 + (subf + b0), ok=(b >= b_min) & (b <= b_max), a0=None, row_a=a))

    col(0, 0, 15), col(0, 8, 15), col(1, 0, 7), col(2, 0, 4), col(3, 0, 3)
    rowp(0, 8, 8, 15), rowp(0, 0, 4, 7), rowp(1, 0, 4, 7), rowp(2, 0, 4, 4)
    for p in pieces:
        p["c"] = jnp.where(p["ok"], p["c"], -jnp.inf)
        p["f"] = jnp.where(p["ok"], p["f"], -1.0)
    return pieces


def _route_head(s1, s2, exact_ties):
    tq = s1.shape[1]
    v1, rank1 = yield from _top16(s1, exact_ties, want_rank=exact_ties)
    v2, rank2 = yield from _top16(s2, exact_ties)
    e1 = jnp.exp(v1 - v1[0:1])
    e2 = jnp.exp(v2 - v2[0:1])
    pieces = _pair_pieces(v1, v2, e1, e2)

    taken = [jnp.zeros((8, tq), F32) for _ in pieces]
    cs = [p["c"] for p in pieces]
    for _ in range(PEER_TOPK):
        m = cs[0]
        for c in cs[1:]:
            m = jnp.maximum(m, c)
        m = jnp.max(m, axis=0, keepdims=True)
        if exact_ties:
            fm = None
            for c, p in zip(cs, pieces):
                cand = jnp.where(c == m, p["f"], 1e9)
                fm = cand if fm is None else jnp.minimum(fm, cand)
            fm = jnp.min(fm, axis=0, keepdims=True)
        for i, p in enumerate(pieces):
            hit = (p["f"] == fm) if exact_ties else (cs[i] == m)
            taken[i] = jnp.where(hit, 1.0, taken[i])
            cs[i] = jnp.where(hit, -jnp.inf, cs[i])
        yield

    row16 = lax.broadcasted_iota(jnp.int32, (PEER_TOPK, tq), 0)
    n1 = jnp.zeros((PEER_TOPK, tq), F32)
    z = jnp.zeros((1, tq), F32)
    for tf, p in zip(taken, pieces):
        z = z + jnp.sum(tf * p["e"], axis=0, keepdims=True)
        if p["row_a"] is None:
            pad = jnp.zeros((8, tq), F32)
            n1 = n1 + (jnp.concatenate([tf, pad], axis=0) if p["a0"] == 0
                       else jnp.concatenate([pad, tf], axis=0))
        else:
            n1 = n1 + jnp.where(row16 == p["row_a"], jnp.sum(tf, axis=0, keepdims=True), 0.0)

    in2 = rank2 < float(PEER_TOPK)
    cnt1 = jnp.zeros(s1.shape, F32)
    if exact_ties:
        in1 = rank1 < float(PEER_TOPK)
        for r in range(PEER_TOPK):
            cnt1 = jnp.where(rank1 == float(r), n1[r:r + 1], cnt1)
    else:
        in1 = s1 >= v1[PEER_TOPK - 1:PEER_TOPK]
        for r in range(PEER_TOPK):
            cnt1 = jnp.where(s1 == v1[r:r + 1], n1[r:r + 1], cnt1)
    p1 = jnp.where(in1, jnp.exp(s1 - v1[0:1]) / z, 0.0)
    e2_dense = jnp.where(in2, jnp.exp(s2 - v2[0:1]), 0.0)
    winners = (jnp.sum(jnp.where(in1, 1.0, 0.0), axis=0, keepdims=True)
               + jnp.sum(jnp.where(in2, 1.0, 0.0), axis=0, keepdims=True)
               + jnp.sum(n1, axis=0, keepdims=True))
    return cnt1, p1, rank2, e2_dense, winners


def _route_unit(qh, sk, tabs, exact_ties):
    half = PEER_QDIM // 2
    s1 = lax.dot_general(sk[0], qh[:, :half], NT_DIMS, preferred_element_type=F32)
    s2 = lax.dot_general(sk[1], qh[:, half:], NT_DIMS, preferred_element_type=F32)
    cnt1, p1, rank2, e2_dense, winners = yield from _route_head(s1, s2, exact_ties)
    cnt_ref, p1_ref, rk2_ref, e2_ref = tabs
    cnt_ref[...] = cnt1
    p1_ref[...] = p1
    rk2_ref[...] = rank2.astype(BF16)
    e2_ref[...] = e2_dense.astype(BF16)
    return winners


class _RouteUnits:
    def __init__(self, qh_ref, sk_ref, tabs, n):
        self.args = [(qh_ref, sk_ref, [t.at[u] for t in tabs], u) for u in range(n)]
        self.steps = [_route_unit(q[u], k[u], t, False) for q, k, t, u in self.args]
        self.winners = [None] * n

    def advance(self, k):
        for u, steps in enumerate(self.steps):
            if self.winners[u] is None:
                try:
                    for _ in range(k):
                        next(steps)
                except StopIteration as done:
                    self.winners[u] = done.value

    def finish(self):
        while any(w is None for w in self.winners):
            self.advance(1)
        for (q, k, t, u), w in zip(self.args, self.winners):
            @pl.when(jnp.max(w) > float(3 * PEER_TOPK))
            def _(q=q, k=k, t=t, u=u):
                _drain(_route_unit(q[u], k[u], t, True))


def _peer_query_kernel(h_ref, g_ref, wpq_ref, xnt_ref, qh_ref):
    hn = _rms(h_ref[...], g_ref[...])
    xnt_ref[...] = hn.T.astype(BF16)
    q = jnp.dot(hn.astype(BF16), wpq_ref[...], preferred_element_type=F32)
    for hh in range(PEER_HEADS):
        qh_ref[hh] = q[:, hh * PEER_QDIM:(hh + 1) * PEER_QDIM].astype(BF16)


def peer_query(h, g, wpq_bf):
    t, d = h.shape
    tq = QUERY_TQ
    return pl.pallas_call(
        _peer_query_kernel,
        grid=(t // tq,),
        in_specs=[pl.BlockSpec((tq, d), lambda i: (i, 0)),
                  pl.BlockSpec((1, d), lambda i: (0, 0)),
                  pl.BlockSpec(wpq_bf.shape, lambda i: (0, 0), pipeline_mode=pl.Buffered(1))],
        out_specs=[pl.BlockSpec((d, tq), lambda i: (0, i)),
                   pl.BlockSpec((PEER_HEADS, tq, PEER_QDIM), lambda i: (0, i, 0))],
        out_shape=[jax.ShapeDtypeStruct((d, t), BF16), jax.ShapeDtypeStruct((PEER_HEADS, t, PEER_QDIM), BF16)],
        compiler_params=_cparams(1, 48),
        name="peer_query",
    )(h, g.reshape(1, d), wpq_bf)


def _peer_route_kernel(qh_ref, sk_ref, cnt_ref, p1_ref, rk2_ref, e2_ref, *, n_route):
    _RouteUnits(qh_ref, sk_ref, (cnt_ref, p1_ref, rk2_ref, e2_ref), n_route).finish()


def peer_route(qh, sk_bf, n_route=2):
    t = qh.shape[1]
    hps = PEER_HEADS // n_route
    tab = pl.BlockSpec((n_route, PEER_NKEYS, ROUTE_TQ), lambda i, j: (j, 0, i))
    return pl.pallas_call(
        functools.partial(_peer_route_kernel, n_route=n_route),
        grid=(t // ROUTE_TQ, hps),
        in_specs=[pl.BlockSpec((n_route, ROUTE_TQ, PEER_QDIM), lambda i, j: (j, i, 0)),
                  pl.BlockSpec((n_route, 2, PEER_NKEYS, PEER_QDIM // 2), lambda i, j: (j, 0, 0, 0))],
        out_specs=[tab] * 4,
        out_shape=[jax.ShapeDtypeStruct((PEER_HEADS, PEER_NKEYS, t), dt) for dt in (F32, F32, BF16, BF16)],
        compiler_params=_cparams(2, 32),
        name="peer_route",
    )(qh, sk_bf)


def _peer_dense_kernel(xnt_ref, u_ref, vt_ref, cnt_ref, p1_ref, rk2_ref, e2_ref, h_ref, fg_ref, y_ref, acc_ref,
                       *, te, n_e):
    j = pl.program_id(1)

    @pl.when(j == 0)
    def _():
        acc_ref[...] = jnp.zeros(acc_ref.shape, F32)

    tq = xnt_ref.shape[1]
    groups = te // PEER_NKEYS
    gpc = PEER_ROW_CHUNK // PEER_NKEYS
    parts = []
    for a in range(groups):
        if a % gpc == 0:
            rows = slice(a * PEER_NKEYS, (a + gpc) * PEER_NKEYS)
            act = jax.nn.gelu(jnp.dot(u_ref[rows, :], xnt_ref[...], preferred_element_type=F32))
        i1 = j * groups + a
        gate = jnp.zeros((PEER_NKEYS, tq), BF16)
        for hh in range(PEER_HEADS):
            cnt_row = cnt_ref[hh, pl.ds(i1, 1), :].astype(BF16)
            p1_row = p1_ref[hh, pl.ds(i1, 1), :].astype(BF16)
            gate = gate + jnp.where(rk2_ref[hh] < cnt_row, p1_row * e2_ref[hh], jnp.zeros((), BF16))
        lo = (a % gpc) * PEER_NKEYS
        parts.append(act[lo:lo + PEER_NKEYS].astype(BF16) * gate)
    hm_t = parts[0] if groups == 1 else jnp.concatenate(parts, axis=0)
    acc_ref[...] += jnp.dot(vt_ref[...], hm_t, preferred_element_type=F32)

    @pl.when(j == n_e - 1)
    def _():
        y_ref[...] = _rms(h_ref[...] + acc_ref[...].T, fg_ref[...])


def peer_dense(xnt, u_bf, vt_bf, tabs, h, final_g, tq, te):
    t, d = h.shape
    n_e = u_bf.shape[0] // te
    once = dict(pipeline_mode=pl.Buffered(1))
    tab = pl.BlockSpec((PEER_HEADS, PEER_NKEYS, tq), lambda i, j: (0, 0, i), **once)
    return pl.pallas_call(
        functools.partial(_peer_dense_kernel, te=te, n_e=n_e),
        grid=(t // tq, n_e),
        in_specs=[pl.BlockSpec((d, tq), lambda i, j: (0, i), **once),
                  pl.BlockSpec((te, d), lambda i, j: (j, 0)),
                  pl.BlockSpec((d, te), lambda i, j: (0, j)),
                  tab, tab, tab, tab,
                  pl.BlockSpec((tq, d), lambda i, j: (i, 0), **once),
                  pl.BlockSpec((1, d), lambda i, j: (0, 0))],
        out_specs=pl.BlockSpec((tq, d), lambda i, j: (i, 0)),
        out_shape=jax.ShapeDtypeStruct((t, d), F32),
        scratch_shapes=[pltpu.VMEM((d, tq), F32)],
        compiler_params=_cparams(2, 56),
        name="peer_dense",
    )(xnt, u_bf, vt_bf, *tabs, h, final_g.reshape(1, d))


def kernel(x_prompt, x_sample, mem_prompt, cache_k, cache_v, cache_conv, cache_mem_k, cache_mem_v, page_table,
           norm1_g, w_in, rel_bias, conv_w, conv_b, conv_ln_g, conv_ln_b, w_out, norm2_g, mem_norm_g, w_cq,
           w_mk, w_mv, w_co, norm3_g, w_pq, peer_sub_keys, peer_u, peer_v, final_g):
    depth = w_in.shape[0]
    assert depth == 1, "single-layer step"
    n_p, seq, d = x_prompt.shape
    n_s, t_s, _ = x_sample.shape
    l = 0
    bf = lambda a: a.astype(BF16)

    w_in_bf, w_out_bf = bf(w_in[l]), bf(w_out[l])
    w_cq_bf, w_co_bf, w_pq_bf = bf(w_cq[l]), bf(w_co[l]), bf(w_pq[l])
    w_mem_bf = bf(jnp.concatenate([w_mk[l], w_mv[l]], axis=1))
    sk_bf, u_bf, vt_bf = bf(peer_sub_keys[l]), bf(peer_u[l]), bf(peer_v[l].T)

    n_pool, n_pages = cache_k.shape[1], page_table.shape[1]
    bias_p, t_far, t_last, t_own = bias_tiles(rel_bias, n_pages * PAGE_SIZE)
    c31_h = rel_bias[NUM_BUCKETS - 1]

    def cross_and_query(x, attn, conv, mk, mv, n_grp, rows, tiles_per_seq):
        h = cross_block(x, attn, conv, w_out_bf, norm2_g[l], w_cq_bf, w_co_bf, mk, mv, n_grp, rows, tiles_per_seq)
        return (h,) + tuple(peer_query(h, norm3_g[l], w_pq_bf))

    experts = lambda xnt, tabs, h: peer_dense(xnt, u_bf, vt_bf, tabs, h, final_g,
                                              tq=PEER_TOKEN_TILE, te=PEER_EXPERT_TILE)

    xp = x_prompt.reshape(n_p * seq, d)
    mk_p, mv_p = rms_matmul(mem_prompt.reshape(n_p * N_MEM, d), mem_norm_g[l], w_mem_bf, 2, tm=MEM_PROJ_ROWS)
    q_p, k_p, v_p, ga_p, gg_p = rms_matmul(xp, norm1_g[l], w_in_bf, 5, tm=IN_PROJ_ROWS)
    attn_p = moba_prompt(q_p, k_p, v_p, bias_p, c31_h, n_p, seq)
    zero_buf = jnp.zeros((n_p, CONV_K - 1, CONV_CH), F32)
    conv_p, buf_p = conformer_conv(ga_p, gg_p, zero_buf, conv_w[l], conv_b[l], conv_ln_g[l], conv_ln_b[l],
                                   n_p, seq, tt=CONV_ROWS, sb=1)
    h_p, xnt_p, qh_p = cross_and_query(xp, attn_p, conv_p, mk_p.reshape(n_p, N_MEM, X_WIDTH),
                                       mv_p.reshape(n_p, N_MEM, X_WIDTH), 1, CROSS_ROWS, seq // CROSS_ROWS)

    xs = x_sample.reshape(n_s * t_s, d)
    q_s, k_s, v_s, ga_s, gg_s = rms_matmul(xs, norm1_g[l], w_in_bf, 5, tm=IN_PROJ_ROWS)
    tok3 = lambda a: a.reshape(n_s, t_s, ATTN_WIDTH)
    units_p = PEER_HEADS * (n_p * seq // ROUTE_TQ)
    fuse = units_p % n_s == 0 and PEER_HEADS % (units_p // n_s) == 0
    attn_s, *tabs_p = moba_sample(tok3(q_s), tok3(k_s), tok3(v_s),
                                  cache_k[l].reshape(n_pool, PAGE_SIZE * N_HEADS, HEAD_DIM),
                                  cache_v[l].reshape(n_pool, PAGE_SIZE * N_HEADS, HEAD_DIM),
                                  page_table, t_far, t_last, t_own, route=(qh_p, sk_bf) if fuse else None)
    if not fuse:
        tabs_p = peer_route(qh_p, sk_bf)
    y_p = experts(xnt_p, tabs_p, h_p)
    conv_s, buf_s = conformer_conv(ga_s, gg_s, cache_conv[l], conv_w[l], conv_b[l], conv_ln_g[l], conv_ln_b[l],
                                   n_s, t_s, tt=t_s, sb=SHORT_SEQS_PER_STEP)
    h_s, xnt_s, qh_s = cross_and_query(xs, attn_s.reshape(n_s * t_s, ATTN_WIDTH), conv_s,
                                       cache_mem_k[l].reshape(n_s, N_MEM * X_HEADS, HEAD_DIM),
                                       cache_mem_v[l].reshape(n_s, N_MEM * X_HEADS, HEAD_DIM),
                                       SHORT_SEQS_PER_STEP, t_s, 1)
    y_s = experts(xnt_s, peer_route(qh_s, sk_bf), h_s)

    n_pg = seq // PAGE_SIZE
    return (y_p.reshape(n_p, seq, d), y_s.reshape(n_s, t_s, d),
            k_p.reshape(1, n_p, n_pg, PAGE_SIZE, N_HEADS, HEAD_DIM),
            v_p.reshape(1, n_p, n_pg, PAGE_SIZE, N_HEADS, HEAD_DIM),
            buf_p[None],
            mk_p.reshape(1, n_p, N_MEM, X_HEADS, HEAD_DIM), mv_p.reshape(1, n_p, N_MEM, X_HEADS, HEAD_DIM),
            k_s.reshape(1, n_s, t_s, N_HEADS, HEAD_DIM), v_s.reshape(1, n_s, t_s, N_HEADS, HEAD_DIM),
            buf_s[None])
```

```python
import functools
import math

import numpy as np
import jax
import jax.numpy as jnp
from jax import lax
from jax.experimental import pallas as pl
from jax.experimental.pallas import tpu as pltpu

F32 = jnp.float32
BF16 = jnp.bfloat16

D_MODEL = 2048
HEAD_DIM = 128
N_HEADS = 8
ATTN_WIDTH = N_HEADS * HEAD_DIM
CONV_CH = 1024
MOBA_BLOCK = 256
MOBA_TOPK = 3
PAGE_SIZE = 128
NUM_BUCKETS = 32
MAX_EXACT = 16
REL_MAX_DIST = 128
CONV_K = 31
N_MEM = 256
X_HEADS = 4
X_WIDTH = X_HEADS * HEAD_DIM
PEER_HEADS = 8
PEER_NKEYS = 128
PEER_TOPK = 16
PEER_QDIM = 256
PEER_ROW_CHUNK = 512
ROUTE_TQ = 256
QUERY_TQ = 256
ROUTE_STEPS_PER_PAGE = 2
EPS = 1e-6
NEG = -1e30

IN_PROJ_ROWS = 256
MEM_PROJ_ROWS = 512
OUT_PROJ_ROWS = 512
CONV_ROWS = 256
SHORT_SEQS_PER_STEP = 8
CROSS_ROWS = 256
PEER_TOKEN_TILE = 512
PEER_EXPERT_TILE = 1024

MIB = 1024 * 1024
NT_DIMS = (((1,), (1,)), ((), ()))


def _cparams(n_grid, vmem_mib):
    return pltpu.CompilerParams(dimension_semantics=("arbitrary",) * n_grid,
                                vmem_limit_bytes=vmem_mib * MIB)


def _rms(x, g):
    return x * lax.rsqrt(jnp.mean(x * x, axis=-1, keepdims=True) + EPS) * g


def _rms_matmul_kernel(x_ref, g_ref, w_ref, *outs):
    xn = _rms(x_ref[...], g_ref[...]).astype(BF16)
    tn = outs[0].shape[1]
    for s, o_ref in enumerate(outs):
        o_ref[...] = jnp.dot(xn, w_ref[:, s * tn:(s + 1) * tn], preferred_element_type=F32)


def rms_matmul(x, g, w_bf, n_out, tm):
    t, d = x.shape
    tn = w_bf.shape[1] // n_out
    return pl.pallas_call(
        _rms_matmul_kernel,
        grid=(t // tm,),
        in_specs=[pl.BlockSpec((tm, d), lambda i: (i, 0)),
                  pl.BlockSpec((1, d), lambda i: (0, 0)),
                  pl.BlockSpec(w_bf.shape, lambda i: (0, 0), pipeline_mode=pl.Buffered(1))],
        out_specs=[pl.BlockSpec((tm, tn), lambda i: (i, 0))] * n_out,
        out_shape=[jax.ShapeDtypeStruct((t, tn), F32)] * n_out,
        compiler_params=_cparams(1, 56),
        name="rms_matmul",
    )(x, g.reshape(1, d), w_bf)


def _bucket_np(rel):
    n = np.maximum(rel, 0)
    nf = np.maximum(n, 1).astype(np.float32)
    large = MAX_EXACT + (np.log(nf / MAX_EXACT) / np.float32(math.log(REL_MAX_DIST / MAX_EXACT))
                         * (NUM_BUCKETS - MAX_EXACT)).astype(np.int32)
    large = np.minimum(large, NUM_BUCKETS - 1)
    return np.where(n < MAX_EXACT, n, large).astype(np.int32)


def _bias_kernel(rb_ref, rbx_ref, bkp_ref, bkf_ref, bkl_ref, bko_ref, tp_ref, tf_ref, tl_ref, to_ref):
    def lookup(bk, table):
        acc = jnp.full(bk.shape, NEG, F32)
        for b in range(NUM_BUCKETS):
            acc = jnp.where(bk == b, table(b), acc)
        return acc

    for h in range(N_HEADS):
        for t in range(2):
            tp_ref[h, t] = lookup(bkp_ref[t], lambda b: rb_ref[b, h])
    by_row = lambda b: rbx_ref[b]
    tf_ref[...] = lookup(bkf_ref[...], by_row)
    tl_ref[...] = lookup(bkl_ref[...], by_row)
    to_ref[...] = lookup(bko_ref[...], by_row)


def bias_tiles(rel_bias, n_past):
    key = np.arange(MOBA_BLOCK)[:, None]
    qry = np.arange(MOBA_BLOCK)[None, :]
    bkp = np.stack([_bucket_np(qry - key), _bucket_np(MOBA_BLOCK + qry - key)])
    r = np.arange(N_HEADS * 8)[:, None]
    rh, rq = r // 8, r % 8
    c = np.arange(PAGE_SIZE * N_HEADS)[None, :]
    ct, ch = c // N_HEADS, c % N_HEADS
    bkf = np.where(rh == ch, NUM_BUCKETS - 1, -1)
    bkl = np.where(rh == ch, _bucket_np(n_past + rq - (n_past - PAGE_SIZE + ct)), -1)
    assert PAGE_SIZE >= REL_MAX_DIST
    co = np.arange(128)[None, :]
    coh, cot = co // 8, co % 8
    bko = np.where((rh == coh) & (cot <= rq), _bucket_np(rq - cot), -1)
    rbx = jnp.repeat(rel_bias, 8, axis=1)[:, :, None]
    vm = pl.BlockSpec(memory_space=pltpu.VMEM)
    i32 = lambda a: jnp.asarray(a.astype(np.int32))
    return pl.pallas_call(
        _bias_kernel,
        in_specs=[pl.BlockSpec(memory_space=pltpu.SMEM), vm, vm, vm, vm, vm],
        out_specs=[vm, vm, vm, vm],
        out_shape=[jax.ShapeDtypeStruct((N_HEADS, 2, MOBA_BLOCK, MOBA_BLOCK), F32),
                   jax.ShapeDtypeStruct(bkf.shape, F32),
                   jax.ShapeDtypeStruct(bkl.shape, F32),
                   jax.ShapeDtypeStruct(bko.shape, F32)],
        compiler_params=pltpu.CompilerParams(vmem_limit_bytes=32 * MIB),
        name="bias_tiles",
    )(rel_bias, rbx, i32(bkp), i32(bkf), i32(bkl), i32(bko))


def _split_bf16(x):
    hi = x.astype(BF16)
    lo = (x - hi.astype(F32)).astype(BF16)
    return hi, lo


def _moba_prompt_kernel(q_ref, k_ref, v_ref, bias_ref, c31_ref, o_ref, qbf, kbf, vt, pen_ref, s_ref):
    nb = kbf.shape[0] // MOBA_BLOCK
    scale = HEAD_DIM ** -0.5
    blk = lambda b: slice(b * MOBA_BLOCK, (b + 1) * MOBA_BLOCK)

    q = q_ref[...]
    q_hi, q_lo = _split_bf16(q)
    qbf[...] = q_hi
    kbf[...] = k_ref[...].astype(BF16)
    km = jnp.concatenate([jnp.mean(k_ref[blk(b), :], axis=0, keepdims=True) for b in range(nb)]
                         + [jnp.zeros((16 - nb, HEAD_DIM), F32)], axis=0)
    for b in range(nb):
        vt[b] = v_ref[blk(b), :].T.astype(BF16)
    km_hi, km_lo = _split_bf16(km)
    gate = (lax.dot_general(km_hi, q_hi, NT_DIMS, preferred_element_type=F32)
            + lax.dot_general(km_lo, q_hi, NT_DIMS, preferred_element_type=F32)
            + lax.dot_general(km_hi, q_lo, NT_DIMS, preferred_element_type=F32))

    row = lax.broadcasted_iota(jnp.int32, gate.shape, 0)
    own = lax.broadcasted_iota(jnp.int32, gate.shape, 1) // MOBA_BLOCK
    rank = jnp.zeros(gate.shape, F32)
    for b2 in range(nb):
        gb = gate[b2:b2 + 1, :]
        beats = ((gb > gate) | ((gb == gate) & (b2 < row))) & (b2 < own)
        rank = rank + jnp.where(beats, 1.0, 0.0)
    pen_ref[...] = jnp.where((row < own) & (rank < float(MOBA_TOPK)), 0.0, NEG)

    key = lax.broadcasted_iota(jnp.int32, (MOBA_BLOCK, MOBA_BLOCK), 0)
    qry = lax.broadcasted_iota(jnp.int32, (MOBA_BLOCK, MOBA_BLOCK), 1)
    c31 = c31_ref[pl.program_id(1)]
    for qi in range(nb):
        m = None
        for kb in range(qi + 1):
            s = lax.dot_general(kbf[blk(kb), :], qbf[blk(qi), :], NT_DIMS, preferred_element_type=F32) * scale
            if kb == qi:
                s = jnp.where(key <= qry, s + bias_ref[0, 0], NEG)
            else:
                s = s + (bias_ref[0, 1] if kb == qi - 1 else c31) + pen_ref[kb:kb + 1, blk(qi)]
            s_ref[kb] = s
            cm = jnp.max(s, axis=0, keepdims=True)
            m = cm if m is None else jnp.maximum(m, cm)
        lsum = jnp.zeros((1, MOBA_BLOCK), F32)
        acc = jnp.zeros((HEAD_DIM, MOBA_BLOCK), F32)
        for kb in range(qi + 1):
            p = jnp.exp(s_ref[kb] - m)
            lsum = lsum + jnp.sum(p, axis=0, keepdims=True)
            acc = acc + jnp.dot(vt[kb], p.astype(BF16), preferred_element_type=F32)
        o_ref[blk(qi), :] = (acc / lsum).T.astype(o_ref.dtype)


def moba_prompt(q, k, v, bias_p, c31_h, n_seq, seq):
    nq = seq // MOBA_BLOCK
    tok = pl.BlockSpec((seq, HEAD_DIM), lambda n, h: (n, h))
    return pl.pallas_call(
        _moba_prompt_kernel,
        grid=(n_seq, N_HEADS),
        in_specs=[tok, tok, tok,
                  pl.BlockSpec((1, 2, MOBA_BLOCK, MOBA_BLOCK), lambda n, h: (h, 0, 0, 0)),
                  pl.BlockSpec(memory_space=pltpu.SMEM)],
        out_specs=tok,
        out_shape=jax.ShapeDtypeStruct(q.shape, BF16),
        scratch_shapes=[pltpu.VMEM((seq, HEAD_DIM), BF16), pltpu.VMEM((seq, HEAD_DIM), BF16),
                        pltpu.VMEM((nq, HEAD_DIM, MOBA_BLOCK), BF16), pltpu.VMEM((16, seq), F32),
                        pltpu.VMEM((nq, MOBA_BLOCK, MOBA_BLOCK), F32)],
        compiler_params=_cparams(2, 48),
        name="moba_prompt",
    )(q, k, v, bias_p, c31_h)


def _by_head_rows(x):
    return jnp.concatenate([x[:, h * HEAD_DIM:(h + 1) * HEAD_DIM] for h in range(x.shape[1] // HEAD_DIM)], axis=0)


def _by_head_lanes(x, n_heads):
    t = x.shape[0] // n_heads
    return jnp.concatenate([x[h * t:(h + 1) * t] for h in range(n_heads)], axis=1)


def _moba_sample_kernel(pt_ref, q_ref, kn_ref, vn_ref, tf_ref, tl_ref, to_ref, *refs, n_pages, n_route):
    kp, vp = refs[:n_pages], refs[n_pages:2 * n_pages]
    refs = refs[2 * n_pages:]
    if n_route:
        qh_ref, sk_ref, o_ref, cnt_ref, p1_ref, rk2_ref, e2_ref, s_ref = refs
    else:
        o_ref, s_ref = refs
    ppb = MOBA_BLOCK // PAGE_SIZE
    nb = n_pages // ppb
    scale = HEAD_DIM ** -0.5
    t_new = q_ref.shape[1]
    nr = N_HEADS * t_new

    route = _RouteUnits(qh_ref, sk_ref, (cnt_ref, p1_ref, rk2_ref, e2_ref), n_route) if n_route else None
    interleave = (lambda: route.advance(ROUTE_STEPS_PER_PAGE)) if n_route else (lambda: None)

    q = _by_head_rows(q_ref[0])
    q_bf = q.astype(BF16)
    zpad = jnp.zeros((128 - nr, HEAD_DIM), F32)
    kn = jnp.concatenate([_by_head_rows(kn_ref[0]), zpad], axis=0).astype(BF16)
    vn = jnp.concatenate([_by_head_rows(vn_ref[0]), zpad], axis=0).astype(BF16)

    gates = []
    for b in range(nb):
        ksum = jnp.zeros((N_HEADS, HEAD_DIM), F32)
        for pg in range(ppb):
            ksum = ksum + jnp.sum(kp[b * ppb + pg][...].reshape(PAGE_SIZE, N_HEADS, HEAD_DIM), axis=0)
        kmean = ksum / float(MOBA_BLOCK)
        krep = jnp.concatenate([jnp.broadcast_to(kmean[h:h + 1, :], (t_new, HEAD_DIM)) for h in range(N_HEADS)],
                               axis=0)
        gates.append(jnp.sum(q * krep, axis=1, keepdims=True))

    penalty = []
    for b in range(nb):
        rank = jnp.zeros((nr, 1), F32)
        for b2 in range(nb):
            if b2 != b:
                beats = (gates[b2] > gates[b]) | ((gates[b2] == gates[b]) & (b2 < b))
                rank = rank + jnp.where(beats, 1.0, 0.0)
        penalty.append(jnp.where(rank < float(MOBA_TOPK), 0.0, NEG))

    s_own = lax.dot_general(q_bf, kn, NT_DIMS, preferred_element_type=F32) * scale + to_ref[...]
    m = jnp.max(s_own, axis=-1, keepdims=True)
    for pg in range(n_pages):
        s = lax.dot_general(q_bf, kp[pg][...].astype(BF16), NT_DIMS, preferred_element_type=F32)
        bias = tl_ref[...] if pg == n_pages - 1 else tf_ref[...]
        s = s * scale + bias + penalty[pg // ppb]
        s_ref[pg] = s
        m = jnp.maximum(m, jnp.max(s, axis=-1, keepdims=True))
        interleave()
    p = jnp.exp(s_own - m)
    lsum = jnp.sum(p, axis=-1, keepdims=True)
    acc = jnp.dot(p.astype(BF16), vn, preferred_element_type=F32)
    for pg in range(n_pages):
        p = jnp.exp(s_ref[pg] - m)
        lsum = lsum + jnp.sum(p, axis=-1, keepdims=True)
        acc = acc + jnp.dot(p.astype(BF16), vp[pg][...].astype(BF16), preferred_element_type=F32)
        interleave()
    o_ref[0] = _by_head_lanes(acc / lsum, N_HEADS)

    if n_route:
        route.finish()


def moba_sample(q, kn, vn, ck, cv, page_table, t_far, t_last, t_own, route=None):
    nseq, n_pages = page_table.shape
    tok = pl.BlockSpec((1,) + q.shape[1:], lambda b, pt: (b, 0, 0))

    def page_spec(p):
        return pl.BlockSpec((None,) + ck.shape[1:], lambda b, pt, p=p: (pt[b, p], 0, 0))

    const = lambda a: pl.BlockSpec(a.shape, lambda b, pt: (0, 0))
    in_specs = ([tok, tok, tok, const(t_far), const(t_last), const(t_own)]
                + [page_spec(p) for p in range(n_pages)] * 2)
    out_specs, out_shape, extra, n_route = [tok], [jax.ShapeDtypeStruct(q.shape, F32)], [], 0
    if route is not None:
        qh, sk_bf = route
        t = qh.shape[1]
        units = PEER_HEADS * (t // ROUTE_TQ)
        n_route = units // nseq
        assert n_route * nseq == units and PEER_HEADS % n_route == 0
        hps = PEER_HEADS // n_route
        in_specs += [pl.BlockSpec((n_route, ROUTE_TQ, PEER_QDIM), lambda b, pt: (b % hps, b // hps, 0)),
                     pl.BlockSpec((n_route, 2, PEER_NKEYS, PEER_QDIM // 2), lambda b, pt: (b % hps, 0, 0, 0))]
        tab = pl.BlockSpec((n_route, PEER_NKEYS, ROUTE_TQ), lambda b, pt: (b % hps, 0, b // hps))
        out_specs += [tab] * 4
        out_shape += [jax.ShapeDtypeStruct((PEER_HEADS, PEER_NKEYS, t), dt) for dt in (F32, F32, BF16, BF16)]
        extra = [qh, sk_bf]
    grid_spec = pltpu.PrefetchScalarGridSpec(
        num_scalar_prefetch=1, grid=(nseq,), in_specs=in_specs, out_specs=out_specs,
        scratch_shapes=[pltpu.VMEM((n_pages, N_HEADS * q.shape[1], ck.shape[1]), F32)])
    return pl.pallas_call(
        functools.partial(_moba_sample_kernel, n_pages=n_pages, n_route=n_route),
        grid_spec=grid_spec,
        out_shape=out_shape,
        compiler_params=_cparams(1, 56),
        name="moba_sample",
    )(page_table, q, kn, vn, t_far, t_last, t_own, *([ck] * n_pages), *([cv] * n_pages), *extra)


HIST_ROWS = 32
HIST_OFF = HIST_ROWS - (CONV_K - 1)


def _conv_kernel(ga_ref, gg_ref, hist_ref, w_ref, b_ref, lg_ref, lb_ref, o_ref, nb_ref, ext_ref, y_ref,
                 *, tt, n_t, sb):
    t = pl.program_id(1)

    @pl.when(t == 0)
    def _():
        for s in range(sb):
            ext_ref[s, HIST_OFF:HIST_ROWS, :] = hist_ref[s]

    rt = min(tt, 128)
    for s in range(sb):
        rows = slice(s * tt, (s + 1) * tt)
        ext_ref[s, HIST_ROWS:HIST_ROWS + tt, :] = ga_ref[rows, :] * jax.nn.sigmoid(gg_ref[rows, :])
        for c in range(CONV_CH // 128):
            cs = slice(c * 128, (c + 1) * 128)
            for r0 in range(0, tt, rt):
                acc = jnp.zeros((rt, 128), F32)
                for res in range(8):
                    taps = [j for j in range(CONV_K) if (HIST_OFF + j) % 8 == res]
                    q0 = (HIST_OFF + taps[0]) // 8
                    q1 = (HIST_OFF + taps[-1]) // 8
                    if res:
                        win = ext_ref[s, r0 + 8 * q0:r0 + 8 * (q1 + 1) + rt, cs]
                        win = pltpu.roll(win, win.shape[0] - res, axis=0)
                    else:
                        win = ext_ref[s, r0 + 8 * q0:r0 + 8 * q1 + rt, cs]
                    for j in taps:
                        off = 8 * ((HIST_OFF + j) // 8 - q0)
                        acc = acc + w_ref[j:j + 1, cs] * win[off:off + rt]
                y_ref[s * tt + r0:s * tt + r0 + rt, cs] = acc + b_ref[:, cs]
    y = y_ref[...]
    mu = jnp.mean(y, axis=-1, keepdims=True)
    yc = y - mu
    var = jnp.mean(yc * yc, axis=-1, keepdims=True)
    yn = yc * lax.rsqrt(var + EPS) * lg_ref[...] + lb_ref[...]
    o_ref[...] = (yn * jax.nn.sigmoid(yn)).astype(o_ref.dtype)

    @pl.when(t == n_t - 1)
    def _():
        for s in range(sb):
            nb_ref[s] = ext_ref[s, tt + HIST_OFF:tt + HIST_ROWS, :]

    if n_t > 1:
        for s in range(sb):
            ext_ref[s, 0:HIST_ROWS, :] = ext_ref[s, tt:tt + HIST_ROWS, :]


def conformer_conv(ga, gg, hist, w_dw, b_dw, ln_g, ln_b, n_seq, seq, tt, sb):
    n_t = seq // tt
    assert sb == 1 or n_t == 1
    row = lambda a: a.reshape(1, CONV_CH)
    cvec = pl.BlockSpec((1, CONV_CH), lambda n, t: (0, 0))
    tile = pl.BlockSpec((sb * tt, CONV_CH), lambda n, t: (n * n_t + t, 0))
    hist_spec = pl.BlockSpec((sb, CONV_K - 1, CONV_CH), lambda n, t: (n, 0, 0))
    return pl.pallas_call(
        functools.partial(_conv_kernel, tt=tt, n_t=n_t, sb=sb),
        grid=(n_seq // sb, n_t),
        in_specs=[tile, tile, hist_spec, pl.BlockSpec((CONV_K, CONV_CH), lambda n, t: (0, 0)),
                  cvec, cvec, cvec],
        out_specs=[tile, hist_spec],
        out_shape=[jax.ShapeDtypeStruct((n_seq * seq, CONV_CH), BF16),
                   jax.ShapeDtypeStruct((n_seq, CONV_K - 1, CONV_CH), F32)],
        scratch_shapes=[pltpu.VMEM((sb, HIST_ROWS + tt, CONV_CH), F32), pltpu.VMEM((sb * tt, CONV_CH), F32)],
        compiler_params=_cparams(2, 32),
        name="conformer_conv",
    )(ga, gg, hist, w_dw, row(b_dw), row(ln_g), row(ln_b))


def _out_proj_kernel(x_ref, a_ref, c_ref, w_ref, o_ref):
    wa = a_ref.shape[1]
    y = jnp.dot(a_ref[...].astype(BF16), w_ref[0:wa, :], preferred_element_type=F32)
    y = y + jnp.dot(c_ref[...].astype(BF16), w_ref[wa:, :], preferred_element_type=F32)
    o_ref[...] = x_ref[...] + y


def out_proj(x, a, c, w_bf, tm):
    t, d = x.shape
    return pl.pallas_call(
        _out_proj_kernel,
        grid=(t // tm,),
        in_specs=[pl.BlockSpec((tm, d), lambda i: (i, 0)),
                  pl.BlockSpec((tm, a.shape[1]), lambda i: (i, 0)),
                  pl.BlockSpec((tm, c.shape[1]), lambda i: (i, 0)),
                  pl.BlockSpec(w_bf.shape, lambda i: (0, 0))],
        out_specs=pl.BlockSpec((tm, d), lambda i: (i, 0)),
        out_shape=jax.ShapeDtypeStruct((t, d), F32),
        compiler_params=_cparams(1, 48),
        name="out_proj",
    )(x, a, c, w_bf)


def _softmax_pv(s, mv_bf):
    m = jnp.max(s, axis=-1, keepdims=True)
    p = jnp.exp(s - m)
    o = jnp.dot(p.astype(BF16), mv_bf, preferred_element_type=F32)
    return o / jnp.sum(p, axis=-1, keepdims=True)


def _cross_kernel(h_ref, g_ref, wq_ref, wo_ref, mk_ref, mv_ref, o_ref, *, n_grp, rows):
    scale = HEAD_DIM ** -0.5
    h = h_ref[...]
    hn = _rms(h, g_ref[...]).astype(BF16)
    q = jnp.dot(hn, wq_ref[...], preferred_element_type=F32)
    outs = []
    for g in range(n_grp):
        mk = mk_ref[g].astype(BF16)
        mv = mv_ref[g].astype(BF16)
        qg = q[g * rows:(g + 1) * rows]
        if rows >= 128:
            heads = []
            for hd in range(X_HEADS):
                cs = slice(hd * HEAD_DIM, (hd + 1) * HEAD_DIM)
                s = lax.dot_general(qg[:, cs].astype(BF16), mk[:, cs], NT_DIMS, preferred_element_type=F32)
                heads.append(_softmax_pv(s * scale, mv[:, cs]))
            outs.append(jnp.concatenate(heads, axis=1))
        else:
            qx = _by_head_rows(qg).astype(BF16)
            s = lax.dot_general(qx, mk, NT_DIMS, preferred_element_type=F32) * scale
            rowh = lax.broadcasted_iota(jnp.int32, s.shape, 0) // rows
            colh = lax.broadcasted_iota(jnp.int32, s.shape, 1) % X_HEADS
            o = _softmax_pv(jnp.where(rowh == colh, s, NEG), mv)
            outs.append(_by_head_lanes(o, X_HEADS))
    o_all = outs[0] if n_grp == 1 else jnp.concatenate(outs, axis=0)
    o_ref[...] = h + jnp.dot(o_all.astype(BF16), wo_ref[...], preferred_element_type=F32)


def cross_block(h, g, wq_bf, wo_bf, mk, mv, n_grp, rows, tiles_per_seq):
    t, d = h.shape
    tm = n_grp * rows
    if n_grp == 1:
        mem_map = lambda i: (i // tiles_per_seq, 0, 0)
    else:
        mem_map = lambda i: (i, 0, 0)
    mem_spec = pl.BlockSpec((n_grp,) + mk.shape[1:], mem_map)
    return pl.pallas_call(
        functools.partial(_cross_kernel, n_grp=n_grp, rows=rows),
        grid=(t // tm,),
        in_specs=[pl.BlockSpec((tm, d), lambda i: (i, 0)),
                  pl.BlockSpec((1, d), lambda i: (0, 0)),
                  pl.BlockSpec(wq_bf.shape, lambda i: (0, 0)),
                  pl.BlockSpec(wo_bf.shape, lambda i: (0, 0)),
                  mem_spec, mem_spec],
        out_specs=pl.BlockSpec((tm, d), lambda i: (i, 0)),
        out_shape=jax.ShapeDtypeStruct((t, d), F32),
        compiler_params=_cparams(1, 48),
        name="cross_block",
    )(h, g.reshape(1, d), wq_bf, wo_bf, mk, mv)


def _drain(steps):
    try:
        while True:
            next(steps)
    except StopIteration as done:
        return done.value


def _top16(s, exact_ties, want_rank=True):
    n, t = s.shape
    row = lax.broadcasted_iota(jnp.int32, (n, t), 0).astype(F32)
    row16 = lax.broadcasted_iota(jnp.int32, (PEER_TOPK, t), 0)
    rank = jnp.full((n, t), float(PEER_TOPK), F32)
    vals = jnp.zeros((PEER_TOPK, t), F32)
    for r in range(PEER_TOPK):
        m = jnp.max(s, axis=0, keepdims=True)
        if exact_ties:
            idx = jnp.min(jnp.where(s == m, row, float(n)), axis=0, keepdims=True)
            hit = row == idx
        else:
            hit = s == m
        if want_rank:
            rank = jnp.where(hit, float(r), rank)
        s = jnp.where(hit, -jnp.inf, s)
        vals = jnp.where(row16 == r, m, vals)
        yield
    return vals, rank


def _pair_pieces(v1, v2, e1, e2):
    t = v1.shape[1]
    sub = lax.broadcasted_iota(jnp.int32, (8, t), 0)
    subf = sub.astype(F32)
    pieces = []

    def col(b, a0, a_max):
        a = sub + a0
        pieces.append(dict(c=v1[a0:a0 + 8] + v2[b:b + 1], e=e1[a0:a0 + 8] * e2[b:b + 1],
                           f=(subf + a0) * 16.0 + b, ok=a <= a_max, a0=a0, row_a=None))

    def rowp(a, b0, b_min, b_max):
        b = sub + b0
        pieces.append(dict(c=v1[a:a + 1] + v2[b0:b0 + 8], e=e1[a:a + 1] * e2[b0:b0 + 8],
                           f=a * 16.0 + (subf + b0), ok=(b >= b_min) & (b <= b_max), a0=None, row_a=a))

    col(0, 0, 15), col(0, 8, 15), col(1, 0, 7), col(2, 0, 4), col(3, 0, 3)
    rowp(0, 8, 8, 15), rowp(0, 0, 4, 7), rowp(1, 0, 4, 7), rowp(2, 0, 4, 4)
    for p in pieces:
        p["c"] = jnp.where(p["ok"], p["c"], -jnp.inf)
        p["f"] = jnp.where(p["ok"], p["f"], -1.0)
    return pieces


def _route_head(s1, s2, exact_ties):
    tq = s1.shape[1]
    v1, rank1 = yield from _top16(s1, exact_ties, want_rank=exact_ties)
    v2, rank2 = yield from _top16(s2, exact_ties)
    e1 = jnp.exp(v1 - v1[0:1])
    e2 = jnp.exp(v2 - v2[0:1])
    pieces = _pair_pieces(v1, v2, e1, e2)

    taken = [jnp.zeros((8, tq), F32) for _ in pieces]
    cs = [p["c"] for p in pieces]
    for _ in range(PEER_TOPK):
        m = cs[0]
        for c in cs[1:]:
            m = jnp.maximum(m, c)
        m = jnp.max(m, axis=0, keepdims=True)
        if exact_ties:
            fm = None
            for c, p in zip(cs, pieces):
                cand = jnp.where(c == m, p["f"], 1e9)
                fm = cand if fm is None else jnp.minimum(fm, cand)
            fm = jnp.min(fm, axis=0, keepdims=True)
        for i, p in enumerate(pieces):
            hit = (p["f"] == fm) if exact_ties else (cs[i] == m)
            taken[i] = jnp.where(hit, 1.0, taken[i])
            cs[i] = jnp.where(hit, -jnp.inf, cs[i])
        yield

    row16 = lax.broadcasted_iota(jnp.int32, (PEER_TOPK, tq), 0)
    n1 = jnp.zeros((PEER_TOPK, tq), F32)
    z = jnp.zeros((1, tq), F32)
    for tf, p in zip(taken, pieces):
        z = z + jnp.sum(tf * p["e"], axis=0, keepdims=True)
        if p["row_a"] is None:
            pad = jnp.zeros((8, tq), F32)
            n1 = n1 + (jnp.concatenate([tf, pad], axis=0) if p["a0"] == 0
                       else jnp.concatenate([pad, tf], axis=0))
        else:
            n1 = n1 + jnp.where(row16 == p["row_a"], jnp.sum(tf, axis=0, keepdims=True), 0.0)

    in2 = rank2 < float(PEER_TOPK)
    cnt1 = jnp.zeros(s1.shape, F32)
    if exact_ties:
        in1 = rank1 < float(PEER_TOPK)
        for r in range(PEER_TOPK):
            cnt1 = jnp.where(rank1 == float(r), n1[r:r + 1], cnt1)
    else:
        in1 = s1 >= v1[PEER_TOPK - 1:PEER_TOPK]
        for r in range(PEER_TOPK):
            cnt1 = jnp.where(s1 == v1[r:r + 1], n1[r:r + 1], cnt1)
    p1 = jnp.where(in1, jnp.exp(s1 - v1[0:1]) / z, 0.0)
    e2_dense = jnp.where(in2, jnp.exp(s2 - v2[0:1]), 0.0)
    winners = (jnp.sum(jnp.where(in1, 1.0, 0.0), axis=0, keepdims=True)
               + jnp.sum(jnp.where(in2, 1.0, 0.0), axis=0, keepdims=True)
               + jnp.sum(n1, axis=0, keepdims=True))
    return cnt1, p1, rank2, e2_dense, winners


def _route_unit(qh, sk, tabs, exact_ties):
    half = PEER_QDIM // 2
    s1 = lax.dot_general(sk[0], qh[:, :half], NT_DIMS, preferred_element_type=F32)
    s2 = lax.dot_general(sk[1], qh[:, half:], NT_DIMS, preferred_element_type=F32)
    cnt1, p1, rank2, e2_dense, winners = yield from _route_head(s1, s2, exact_ties)
    cnt_ref, p1_ref, rk2_ref, e2_ref = tabs
    cnt_ref[...] = cnt1
    p1_ref[...] = p1
    rk2_ref[...] = rank2.astype(BF16)
    e2_ref[...] = e2_dense.astype(BF16)
    return winners


class _RouteUnits:
    def __init__(self, qh_ref, sk_ref, tabs, n):
        self.args = [(qh_ref, sk_ref, [t.at[u] for t in tabs], u) for u in range(n)]
        self.steps = [_route_unit(q[u], k[u], t, False) for q, k, t, u in self.args]
        self.winners = [None] * n

    def advance(self, k):
        for u, steps in enumerate(self.steps):
            if self.winners[u] is None:
                try:
                    for _ in range(k):
                        next(steps)
                except StopIteration as done:
                    self.winners[u] = done.value

    def finish(self):
        while any(w is None for w in self.winners):
            self.advance(1)
        for (q, k, t, u), w in zip(self.args, self.winners):
            @pl.when(jnp.max(w) > float(3 * PEER_TOPK))
            def _(q=q, k=k, t=t, u=u):
                _drain(_route_unit(q[u], k[u], t, True))


def _peer_query_kernel(h_ref, g_ref, wpq_ref, xnt_ref, qh_ref):
    hn = _rms(h_ref[...], g_ref[...])
    xnt_ref[...] = hn.T.astype(BF16)
    q = jnp.dot(hn.astype(BF16), wpq_ref[...], preferred_element_type=F32)
    for hh in range(PEER_HEADS):
        qh_ref[hh] = q[:, hh * PEER_QDIM:(hh + 1) * PEER_QDIM].astype(BF16)


def peer_query(h, g, wpq_bf):
    t, d = h.shape
    tq = QUERY_TQ
    return pl.pallas_call(
        _peer_query_kernel,
        grid=(t // tq,),
        in_specs=[pl.BlockSpec((tq, d), lambda i: (i, 0)),
                  pl.BlockSpec((1, d), lambda i: (0, 0)),
                  pl.BlockSpec(wpq_bf.shape, lambda i: (0, 0), pipeline_mode=pl.Buffered(1))],
        out_specs=[pl.BlockSpec((d, tq), lambda i: (0, i)),
                   pl.BlockSpec((PEER_HEADS, tq, PEER_QDIM), lambda i: (0, i, 0))],
        out_shape=[jax.ShapeDtypeStruct((d, t), BF16), jax.ShapeDtypeStruct((PEER_HEADS, t, PEER_QDIM), BF16)],
        compiler_params=_cparams(1, 48),
        name="peer_query",
    )(h, g.reshape(1, d), wpq_bf)


def _peer_route_kernel(qh_ref, sk_ref, cnt_ref, p1_ref, rk2_ref, e2_ref, *, n_route):
    _RouteUnits(qh_ref, sk_ref, (cnt_ref, p1_ref, rk2_ref, e2_ref), n_route).finish()


def peer_route(qh, sk_bf, n_route=2):
    t = qh.shape[1]
    hps = PEER_HEADS // n_route
    tab = pl.BlockSpec((n_route, PEER_NKEYS, ROUTE_TQ), lambda i, j: (j, 0, i))
    return pl.pallas_call(
        functools.partial(_peer_route_kernel, n_route=n_route),
        grid=(t // ROUTE_TQ, hps),
        in_specs=[pl.BlockSpec((n_route, ROUTE_TQ, PEER_QDIM), lambda i, j: (j, i, 0)),
                  pl.BlockSpec((n_route, 2, PEER_NKEYS, PEER_QDIM // 2), lambda i, j: (j, 0, 0, 0))],
        out_specs=[tab] * 4,
        out_shape=[jax.ShapeDtypeStruct((PEER_HEADS, PEER_NKEYS, t), dt) for dt in (F32, F32, BF16, BF16)],
        compiler_params=_cparams(2, 32),
        name="peer_route",
    )(qh, sk_bf)


def _peer_dense_kernel(xnt_ref, u_ref, vt_ref, cnt_ref, p1_ref, rk2_ref, e2_ref, h_ref, fg_ref, y_ref, acc_ref,
                       *, te, n_e):
    j = pl.program_id(1)

    @pl.when(j == 0)
    def _():
        acc_ref[...] = jnp.zeros(acc_ref.shape, F32)

    tq = xnt_ref.shape[1]
    groups = te // PEER_NKEYS
    gpc = PEER_ROW_CHUNK // PEER_NKEYS
    parts = []
    for a in range(groups):
        if a % gpc == 0:
            rows = slice(a * PEER_NKEYS, (a + gpc) * PEER_NKEYS)
            act = jax.nn.gelu(jnp.dot(u_ref[rows, :], xnt_ref[...], preferred_element_type=F32).astype(BF16))
        i1 = j * groups + a
        gate = jnp.zeros((PEER_NKEYS, tq), BF16)
        for hh in range(PEER_HEADS):
            cnt_row = cnt_ref[hh, pl.ds(i1, 1), :].astype(BF16)
            p1_row = p1_ref[hh, pl.ds(i1, 1), :].astype(BF16)
            gate = gate + jnp.where(rk2_ref[hh] < cnt_row, p1_row * e2_ref[hh], jnp.zeros((), BF16))
        lo = (a % gpc) * PEER_NKEYS
        parts.append(act[lo:lo + PEER_NKEYS] * gate)
    hm_t = parts[0] if groups == 1 else jnp.concatenate(parts, axis=0)
    acc_ref[...] += jnp.dot(vt_ref[...], hm_t, preferred_element_type=F32)

    @pl.when(j == n_e - 1)
    def _():
        y_ref[...] = _rms(h_ref[...] + acc_ref[...].T, fg_ref[...])


def peer_dense(xnt, u_bf, vt_bf, tabs, h, final_g, tq, te):
    t, d = h.shape
    n_e = u_bf.shape[0] // te
    once = dict(pipeline_mode=pl.Buffered(1))
    tab = pl.BlockSpec((PEER_HEADS, PEER_NKEYS, tq), lambda i, j: (0, 0, i), **once)
    return pl.pallas_call(
        functools.partial(_peer_dense_kernel, te=te, n_e=n_e),
        grid=(t // tq, n_e),
        in_specs=[pl.BlockSpec((d, tq), lambda i, j: (0, i), **once),
                  pl.BlockSpec((te, d), lambda i, j: (j, 0)),
                  pl.BlockSpec((d, te), lambda i, j: (0, j)),
                  tab, tab, tab, tab,
                  pl.BlockSpec((tq, d), lambda i, j: (i, 0), **once),
                  pl.BlockSpec((1, d), lambda i, j: (0, 0))],
        out_specs=pl.BlockSpec((tq, d), lambda i, j: (i, 0)),
        out_shape=jax.ShapeDtypeStruct((t, d), F32),
        scratch_shapes=[pltpu.VMEM((d, tq), F32)],
        compiler_params=_cparams(2, 56),
        name="peer_dense",
    )(xnt, u_bf, vt_bf, *tabs, h, final_g.reshape(1, d))


def kernel(x_prompt, x_sample, mem_prompt, cache_k, cache_v, cache_conv, cache_mem_k, cache_mem_v, page_table,
           norm1_g, w_in, rel_bias, conv_w, conv_b, conv_ln_g, conv_ln_b, w_out, norm2_g, mem_norm_g, w_cq,
           w_mk, w_mv, w_co, norm3_g, w_pq, peer_sub_keys, peer_u, peer_v, final_g):
    depth = w_in.shape[0]
    assert depth == 1, "single-layer step"
    n_p, seq, d = x_prompt.shape
    n_s, t_s, _ = x_sample.shape
    l = 0
    bf = lambda a: a.astype(BF16)

    w_in_bf, w_out_bf = bf(w_in[l]), bf(w_out[l])
    w_cq_bf, w_co_bf, w_pq_bf = bf(w_cq[l]), bf(w_co[l]), bf(w_pq[l])
    w_mem_bf = bf(jnp.concatenate([w_mk[l], w_mv[l]], axis=1))
    sk_bf, u_bf, vt_bf = bf(peer_sub_keys[l]), bf(peer_u[l]), bf(peer_v[l].T)

    n_pool, n_pages = cache_k.shape[1], page_table.shape[1]
    bias_p, t_far, t_last, t_own = bias_tiles(rel_bias, n_pages * PAGE_SIZE)
    c31_h = rel_bias[NUM_BUCKETS - 1]

    def cross_and_query(x, attn, conv, mk, mv, n_grp, rows, tiles_per_seq):
        h = out_proj(x, attn, conv, w_out_bf, tm=OUT_PROJ_ROWS)
        h = cross_block(h, norm2_g[l], w_cq_bf, w_co_bf, mk, mv, n_grp, rows, tiles_per_seq)
        return (h,) + tuple(peer_query(h, norm3_g[l], w_pq_bf))

    experts = lambda xnt, tabs, h: peer_dense(xnt, u_bf, vt_bf, tabs, h, final_g,
                                              tq=PEER_TOKEN_TILE, te=PEER_EXPERT_TILE)

    xp = x_prompt.reshape(n_p * seq, d)
    mk_p, mv_p = rms_matmul(mem_prompt.reshape(n_p * N_MEM, d), mem_norm_g[l], w_mem_bf, 2, tm=MEM_PROJ_ROWS)
    q_p, k_p, v_p, ga_p, gg_p = rms_matmul(xp, norm1_g[l], w_in_bf, 5, tm=IN_PROJ_ROWS)
    attn_p = moba_prompt(q_p, k_p, v_p, bias_p, c31_h, n_p, seq)
    zero_buf = jnp.zeros((n_p, CONV_K - 1, CONV_CH), F32)
    conv_p, buf_p = conformer_conv(ga_p, gg_p, zero_buf, conv_w[l], conv_b[l], conv_ln_g[l], conv_ln_b[l],
                                   n_p, seq, tt=CONV_ROWS, sb=1)
    h_p, xnt_p, qh_p = cross_and_query(xp, attn_p, conv_p, mk_p.reshape(n_p, N_MEM, X_WIDTH),
                                       mv_p.reshape(n_p, N_MEM, X_WIDTH), 1, CROSS_ROWS, seq // CROSS_ROWS)

    xs = x_sample.reshape(n_s * t_s, d)
    q_s, k_s, v_s, ga_s, gg_s = rms_matmul(xs, norm1_g[l], w_in_bf, 5, tm=IN_PROJ_ROWS)
    tok3 = lambda a: a.reshape(n_s, t_s, ATTN_WIDTH)
    units_p = PEER_HEADS * (n_p * seq // ROUTE_TQ)
    fuse = units_p % n_s == 0 and PEER_HEADS % (units_p // n_s) == 0
    attn_s, *tabs_p = moba_sample(tok3(q_s), tok3(k_s), tok3(v_s),
                                  cache_k[l].reshape(n_pool, PAGE_SIZE * N_HEADS, HEAD_DIM),
                                  cache_v[l].reshape(n_pool, PAGE_SIZE * N_HEADS, HEAD_DIM),
                                  page_table, t_far, t_last, t_own, route=(qh_p, sk_bf) if fuse else None)
    if not fuse:
        tabs_p = peer_route(qh_p, sk_bf)
    y_p = experts(xnt_p, tabs_p, h_p)
    conv_s, buf_s = conformer_conv(ga_s, gg_s, cache_conv[l], conv_w[l], conv_b[l], conv_ln_g[l], conv_ln_b[l],
                                   n_s, t_s, tt=t_s, sb=SHORT_SEQS_PER_STEP)
    h_s, xnt_s, qh_s = cross_and_query(xs, attn_s.reshape(n_s * t_s, ATTN_WIDTH), conv_s,
                                       cache_mem_k[l].reshape(n_s, N_MEM * X_HEADS, HEAD_DIM),
                                       cache_mem_v[l].reshape(n_s, N_MEM * X_HEADS, HEAD_DIM),
                                       SHORT_SEQS_PER_STEP, t_s, 1)
    y_s = experts(xnt_s, peer_route(qh_s, sk_bf), h_s)

    n_pg = seq // PAGE_SIZE
    return (y_p.reshape(n_p, seq, d), y_s.reshape(n_s, t_s, d),
            k_p.reshape(1, n_p, n_pg, PAGE_SIZE, N_HEADS, HEAD_DIM),
            v_p.reshape(1, n_p, n_pg, PAGE_SIZE, N_HEADS, HEAD_DIM),
            buf_p[None],
            mk_p.reshape(1, n_p, N_MEM, X_HEADS, HEAD_DIM), mv_p.reshape(1, n_p, N_MEM, X_HEADS, HEAD_DIM),
            k_s.reshape(1, n_s, t_s, N_HEADS, HEAD_DIM), v_s.reshape(1, n_s, t_s, N_HEADS, HEAD_DIM),
            buf_s[None])
```

```python
import functools
import math

import numpy as np
import jax
import jax.numpy as jnp
from jax import lax
from jax.experimental import pallas as pl
from jax.experimental.pallas import tpu as pltpu

F32 = jnp.float32
BF16 = jnp.bfloat16

D_MODEL = 2048
HEAD_DIM = 128
N_HEADS = 8
ATTN_WIDTH = N_HEADS * HEAD_DIM
CONV_CH = 1024
MOBA_BLOCK = 256
MOBA_TOPK = 3
PAGE_SIZE = 128
NUM_BUCKETS = 32
MAX_EXACT = 16
REL_MAX_DIST = 128
CONV_K = 31
N_MEM = 256
X_HEADS = 4
X_WIDTH = X_HEADS * HEAD_DIM
PEER_HEADS = 8
PEER_NKEYS = 128
PEER_TOPK = 16
PEER_QDIM = 256
PEER_ROW_CHUNK = 512
ROUTE_TQ = 256
QUERY_TQ = 256
ROUTE_STEPS_PER_PAGE = 2
EPS = 1e-6
NEG = -1e30

IN_PROJ_ROWS = 256
MEM_PROJ_ROWS = 512
OUT_PROJ_ROWS = 512
CONV_ROWS = 256
SHORT_SEQS_PER_STEP = 8
CROSS_ROWS = 256
PEER_TOKEN_TILE = 512
PEER_EXPERT_TILE = 1024

MIB = 1024 * 1024
NT_DIMS = (((1,), (1,)), ((), ()))


def _cparams(n_grid, vmem_mib):
    return pltpu.CompilerParams(dimension_semantics=("arbitrary",) * n_grid,
                                vmem_limit_bytes=vmem_mib * MIB)


def _rms(x, g):
    return x * lax.rsqrt(jnp.mean(x * x, axis=-1, keepdims=True) + EPS) * g


def _rms_matmul_kernel(x_ref, g_ref, w_ref, *outs):
    xn = _rms(x_ref[...], g_ref[...]).astype(BF16)
    tn = outs[0].shape[1]
    for s, o_ref in enumerate(outs):
        o_ref[...] = jnp.dot(xn, w_ref[:, s * tn:(s + 1) * tn], preferred_element_type=F32)


def rms_matmul(x, g, w_bf, n_out, tm):
    t, d = x.shape
    tn = w_bf.shape[1] // n_out
    return pl.pallas_call(
        _rms_matmul_kernel,
        grid=(t // tm,),
        in_specs=[pl.BlockSpec((tm, d), lambda i: (i, 0)),
                  pl.BlockSpec((1, d), lambda i: (0, 0)),
                  pl.BlockSpec(w_bf.shape, lambda i: (0, 0), pipeline_mode=pl.Buffered(1))],
        out_specs=[pl.BlockSpec((tm, tn), lambda i: (i, 0))] * n_out,
        out_shape=[jax.ShapeDtypeStruct((t, tn), F32)] * n_out,
        compiler_params=_cparams(1, 56),
        name="rms_matmul",
    )(x, g.reshape(1, d), w_bf)


def _bucket_np(rel):
    n = np.maximum(rel, 0)
    nf = np.maximum(n, 1).astype(np.float32)
    large = MAX_EXACT + (np.log(nf / MAX_EXACT) / np.float32(math.log(REL_MAX_DIST / MAX_EXACT))
                         * (NUM_BUCKETS - MAX_EXACT)).astype(np.int32)
    large = np.minimum(large, NUM_BUCKETS - 1)
    return np.where(n < MAX_EXACT, n, large).astype(np.int32)


def _bias_kernel(rb_ref, rbx_ref, bkp_ref, bkf_ref, bkl_ref, bko_ref, tp_ref, tf_ref, tl_ref, to_ref):
    def lookup(bk, table):
        acc = jnp.full(bk.shape, NEG, F32)
        for b in range(NUM_BUCKETS):
            acc = jnp.where(bk == b, table(b), acc)
        return acc

    for h in range(N_HEADS):
        for t in range(2):
            tp_ref[h, t] = lookup(bkp_ref[t], lambda b: rb_ref[b, h])
    by_row = lambda b: rbx_ref[b]
    tf_ref[...] = lookup(bkf_ref[...], by_row)
    tl_ref[...] = lookup(bkl_ref[...], by_row)
    to_ref[...] = lookup(bko_ref[...], by_row)


def bias_tiles(rel_bias, n_past):
    key = np.arange(MOBA_BLOCK)[:, None]
    qry = np.arange(MOBA_BLOCK)[None, :]
    bkp = np.stack([_bucket_np(qry - key), _bucket_np(MOBA_BLOCK + qry - key)])
    r = np.arange(N_HEADS * 8)[:, None]
    rh, rq = r // 8, r % 8
    c = np.arange(PAGE_SIZE * N_HEADS)[None, :]
    ct, ch = c // N_HEADS, c % N_HEADS
    bkf = np.where(rh == ch, NUM_BUCKETS - 1, -1)
    bkl = np.where(rh == ch, _bucket_np(n_past + rq - (n_past - PAGE_SIZE + ct)), -1)
    assert PAGE_SIZE >= REL_MAX_DIST
    co = np.arange(128)[None, :]
    coh, cot = co // 8, co % 8
    bko = np.where((rh == coh) & (cot <= rq), _bucket_np(rq - cot), -1)
    rbx = jnp.repeat(rel_bias, 8, axis=1)[:, :, None]
    vm = pl.BlockSpec(memory_space=pltpu.VMEM)
    i32 = lambda a: jnp.asarray(a.astype(np.int32))
    return pl.pallas_call(
        _bias_kernel,
        in_specs=[pl.BlockSpec(memory_space=pltpu.SMEM), vm, vm, vm, vm, vm],
        out_specs=[vm, vm, vm, vm],
        out_shape=[jax.ShapeDtypeStruct((N_HEADS, 2, MOBA_BLOCK, MOBA_BLOCK), F32),
                   jax.ShapeDtypeStruct(bkf.shape, F32),
                   jax.ShapeDtypeStruct(bkl.shape, F32),
                   jax.ShapeDtypeStruct(bko.shape, F32)],
        compiler_params=pltpu.CompilerParams(vmem_limit_bytes=32 * MIB),
        name="bias_tiles",
    )(rel_bias, rbx, i32(bkp), i32(bkf), i32(bkl), i32(bko))


def _split_bf16(x):
    hi = x.astype(BF16)
    lo = (x - hi.astype(F32)).astype(BF16)
    return hi, lo


def _moba_prompt_kernel(q_ref, k_ref, v_ref, bias_ref, c31_ref, o_ref, qbf, kbf, vt, pen_ref, s_ref):
    nb = kbf.shape[0] // MOBA_BLOCK
    scale = HEAD_DIM ** -0.5
    blk = lambda b: slice(b * MOBA_BLOCK, (b + 1) * MOBA_BLOCK)

    q = q_ref[...]
    q_hi, q_lo = _split_bf16(q)
    qbf[...] = q_hi
    kbf[...] = k_ref[...].astype(BF16)
    km = jnp.concatenate([jnp.mean(k_ref[blk(b), :], axis=0, keepdims=True) for b in range(nb)]
                         + [jnp.zeros((16 - nb, HEAD_DIM), F32)], axis=0)
    for b in range(nb):
        vt[b] = v_ref[blk(b), :].T.astype(BF16)
    km_hi, km_lo = _split_bf16(km)
    gate = (lax.dot_general(km_hi, q_hi, NT_DIMS, preferred_element_type=F32)
            + lax.dot_general(km_lo, q_hi, NT_DIMS, preferred_element_type=F32)
            + lax.dot_general(km_hi, q_lo, NT_DIMS, preferred_element_type=F32))

    row = lax.broadcasted_iota(jnp.int32, gate.shape, 0)
    own = lax.broadcasted_iota(jnp.int32, gate.shape, 1) // MOBA_BLOCK
    rank = jnp.zeros(gate.shape, F32)
    for b2 in range(nb):
        gb = gate[b2:b2 + 1, :]
        beats = ((gb > gate) | ((gb == gate) & (b2 < row))) & (b2 < own)
        rank = rank + jnp.where(beats, 1.0, 0.0)
    pen_ref[...] = jnp.where((row < own) & (rank < float(MOBA_TOPK)), 0.0, NEG)

    key = lax.broadcasted_iota(jnp.int32, (MOBA_BLOCK, MOBA_BLOCK), 0)
    qry = lax.broadcasted_iota(jnp.int32, (MOBA_BLOCK, MOBA_BLOCK), 1)
    c31 = c31_ref[pl.program_id(1)]
    for qi in range(nb):
        m = None
        for kb in range(qi + 1):
            s = lax.dot_general(kbf[blk(kb), :], qbf[blk(qi), :], NT_DIMS, preferred_element_type=F32) * scale
            if kb == qi:
                s = jnp.where(key <= qry, s + bias_ref[0, 0], NEG)
            else:
                s = s + (bias_ref[0, 1] if kb == qi - 1 else c31) + pen_ref[kb:kb + 1, blk(qi)]
            s_ref[kb] = s
            cm = jnp.max(s, axis=0, keepdims=True)
            m = cm if m is None else jnp.maximum(m, cm)
        lsum = jnp.zeros((1, MOBA_BLOCK), F32)
        acc = jnp.zeros((HEAD_DIM, MOBA_BLOCK), F32)
        for kb in range(qi + 1):
            p = jnp.exp(s_ref[kb] - m)
            lsum = lsum + jnp.sum(p, axis=0, keepdims=True)
            acc = acc + jnp.dot(vt[kb], p.astype(BF16), preferred_element_type=F32)
        o_ref[blk(qi), :] = (acc / lsum).T.astype(o_ref.dtype)


def moba_prompt(q, k, v, bias_p, c31_h, n_seq, seq):
    nq = seq // MOBA_BLOCK
    tok = pl.BlockSpec((seq, HEAD_DIM), lambda n, h: (n, h))
    return pl.pallas_call(
        _moba_prompt_kernel,
        grid=(n_seq, N_HEADS),
        in_specs=[tok, tok, tok,
                  pl.BlockSpec((1, 2, MOBA_BLOCK, MOBA_BLOCK), lambda n, h: (h, 0, 0, 0)),
                  pl.BlockSpec(memory_space=pltpu.SMEM)],
        out_specs=tok,
        out_shape=jax.ShapeDtypeStruct(q.shape, BF16),
        scratch_shapes=[pltpu.VMEM((seq, HEAD_DIM), BF16), pltpu.VMEM((seq, HEAD_DIM), BF16),
                        pltpu.VMEM((nq, HEAD_DIM, MOBA_BLOCK), BF16), pltpu.VMEM((16, seq), F32),
                        pltpu.VMEM((nq, MOBA_BLOCK, MOBA_BLOCK), F32)],
        compiler_params=_cparams(2, 48),
        name="moba_prompt",
    )(q, k, v, bias_p, c31_h)


def _by_head_rows(x):
    return jnp.concatenate([x[:, h * HEAD_DIM:(h + 1) * HEAD_DIM] for h in range(x.shape[1] // HEAD_DIM)], axis=0)


def _by_head_lanes(x, n_heads):
    t = x.shape[0] // n_heads
    return jnp.concatenate([x[h * t:(h + 1) * t] for h in range(n_heads)], axis=1)


def _moba_sample_kernel(pt_ref, q_ref, kn_ref, vn_ref, tf_ref, tl_ref, to_ref, *refs, n_pages, n_route):
    kp, vp = refs[:n_pages], refs[n_pages:2 * n_pages]
    refs = refs[2 * n_pages:]
    if n_route:
        qh_ref, sk_ref, o_ref, cnt_ref, p1_ref, rk2_ref, e2_ref, s_ref = refs
    else:
        o_ref, s_ref = refs
    ppb = MOBA_BLOCK // PAGE_SIZE
    nb = n_pages // ppb
    scale = HEAD_DIM ** -0.5
    t_new = q_ref.shape[1]
    nr = N_HEADS * t_new

    route = _RouteUnits(qh_ref, sk_ref, (cnt_ref, p1_ref, rk2_ref, e2_ref), n_route) if n_route else None
    interleave = (lambda: route.advance(ROUTE_STEPS_PER_PAGE)) if n_route else (lambda: None)

    q = _by_head_rows(q_ref[0])
    q_bf = (q * scale).astype(BF16)
    zpad = jnp.zeros((128 - nr, HEAD_DIM), F32)
    kn = jnp.concatenate([_by_head_rows(kn_ref[0]), zpad], axis=0).astype(BF16)
    vn = jnp.concatenate([_by_head_rows(vn_ref[0]), zpad], axis=0).astype(BF16)

    s_own = lax.dot_general(q_bf, kn, NT_DIMS, preferred_element_type=F32) + to_ref[...]
    blk_max, gates = [], []
    for b in range(nb):
        bm = None
        ksum = jnp.zeros((N_HEADS, HEAD_DIM), F32)
        for pg in range(b * ppb, (b + 1) * ppb):
            kpage = kp[pg][...]
            ksum = ksum + jnp.sum(kpage.reshape(PAGE_SIZE, N_HEADS, HEAD_DIM), axis=0)
            s = lax.dot_general(q_bf, kpage.astype(BF16), NT_DIMS, preferred_element_type=F32)
            s = s + (tl_ref[...] if pg == n_pages - 1 else tf_ref[...])
            s_ref[pg] = s
            pm = jnp.max(s, axis=-1, keepdims=True)
            bm = pm if bm is None else jnp.maximum(bm, pm)
            interleave()
        blk_max.append(bm)
        kmean = ksum / float(MOBA_BLOCK)
        krep = jnp.concatenate([jnp.broadcast_to(kmean[h:h + 1, :], (t_new, HEAD_DIM)) for h in range(N_HEADS)],
                               axis=0)
        gates.append(jnp.sum(q * krep, axis=1, keepdims=True))

    penalty = []
    for b in range(nb):
        rank = jnp.zeros((nr, 1), F32)
        for b2 in range(nb):
            if b2 != b:
                beats = (gates[b2] > gates[b]) | ((gates[b2] == gates[b]) & (b2 < b))
                rank = rank + jnp.where(beats, 1.0, 0.0)
        penalty.append(jnp.where(rank < float(MOBA_TOPK), 0.0, NEG))

    m = jnp.max(s_own, axis=-1, keepdims=True)
    for b in range(nb):
        m = jnp.maximum(m, blk_max[b] + penalty[b])

    p = jnp.exp(s_own - m)
    lsum = jnp.sum(p, axis=-1, keepdims=True)
    acc = jnp.dot(p.astype(BF16), vn, preferred_element_type=F32)
    for pg in range(n_pages):
        p = jnp.exp(s_ref[pg] + (penalty[pg // ppb] - m))
        lsum = lsum + jnp.sum(p, axis=-1, keepdims=True)
        acc = acc + jnp.dot(p.astype(BF16), vp[pg][...].astype(BF16), preferred_element_type=F32)
        interleave()
    o_ref[0] = _by_head_lanes(acc / lsum, N_HEADS)

    if n_route:
        route.finish()


def moba_sample(q, kn, vn, ck, cv, page_table, t_far, t_last, t_own, route=None):
    nseq, n_pages = page_table.shape
    tok = pl.BlockSpec((1,) + q.shape[1:], lambda b, pt: (b, 0, 0))

    def page_spec(p):
        return pl.BlockSpec((None,) + ck.shape[1:], lambda b, pt, p=p: (pt[b, p], 0, 0))

    const = lambda a: pl.BlockSpec(a.shape, lambda b, pt: (0, 0))
    in_specs = ([tok, tok, tok, const(t_far), const(t_last), const(t_own)]
                + [page_spec(p) for p in range(n_pages)] * 2)
    out_specs, out_shape, extra, n_route = [tok], [jax.ShapeDtypeStruct(q.shape, F32)], [], 0
    if route is not None:
        qh, sk_bf = route
        t = qh.shape[1]
        units = PEER_HEADS * (t // ROUTE_TQ)
        n_route = units // nseq
        assert n_route * nseq == units and PEER_HEADS % n_route == 0
        hps = PEER_HEADS // n_route
        in_specs += [pl.BlockSpec((n_route, ROUTE_TQ, PEER_QDIM), lambda b, pt: (b % hps, b // hps, 0)),
                     pl.BlockSpec((n_route, 2, PEER_NKEYS, PEER_QDIM // 2), lambda b, pt: (b % hps, 0, 0, 0))]
        tab = pl.BlockSpec((n_route, PEER_NKEYS, ROUTE_TQ), lambda b, pt: (b % hps, 0, b // hps))
        out_specs += [tab] * 4
        out_shape += [jax.ShapeDtypeStruct((PEER_HEADS, PEER_NKEYS, t), dt) for dt in (F32, F32, BF16, BF16)]
        extra = [qh, sk_bf]
    grid_spec = pltpu.PrefetchScalarGridSpec(
        num_scalar_prefetch=1, grid=(nseq,), in_specs=in_specs, out_specs=out_specs,
        scratch_shapes=[pltpu.VMEM((n_pages, N_HEADS * q.shape[1], ck.shape[1]), F32)])
    return pl.pallas_call(
        functools.partial(_moba_sample_kernel, n_pages=n_pages, n_route=n_route),
        grid_spec=grid_spec,
        out_shape=out_shape,
        compiler_params=_cparams(1, 56),
        name="moba_sample",
    )(page_table, q, kn, vn, t_far, t_last, t_own, *([ck] * n_pages), *([cv] * n_pages), *extra)


HIST_ROWS = 32
HIST_OFF = HIST_ROWS - (CONV_K - 1)


def _conv_kernel(ga_ref, gg_ref, hist_ref, w_ref, b_ref, lg_ref, lb_ref, o_ref, nb_ref, ext_ref, y_ref,
                 *, tt, n_t, sb):
    t = pl.program_id(1)

    @pl.when(t == 0)
    def _():
        for s in range(sb):
            ext_ref[s, HIST_OFF:HIST_ROWS, :] = hist_ref[s]

    rt = min(tt, 128)
    for s in range(sb):
        rows = slice(s * tt, (s + 1) * tt)
        ext_ref[s, HIST_ROWS:HIST_ROWS + tt, :] = ga_ref[rows, :] * jax.nn.sigmoid(gg_ref[rows, :])
        for c in range(CONV_CH // 128):
            cs = slice(c * 128, (c + 1) * 128)
            for r0 in range(0, tt, rt):
                acc = jnp.zeros((rt, 128), F32)
                for res in range(8):
                    taps = [j for j in range(CONV_K) if (HIST_OFF + j) % 8 == res]
                    q0 = (HIST_OFF + taps[0]) // 8
                    q1 = (HIST_OFF + taps[-1]) // 8
                    if res:
                        win = ext_ref[s, r0 + 8 * q0:r0 + 8 * (q1 + 1) + rt, cs]
                        win = pltpu.roll(win, win.shape[0] - res, axis=0)
                    else:
                        win = ext_ref[s, r0 + 8 * q0:r0 + 8 * q1 + rt, cs]
                    for j in taps:
                        off = 8 * ((HIST_OFF + j) // 8 - q0)
                        acc = acc + w_ref[j:j + 1, cs] * win[off:off + rt]
                y_ref[s * tt + r0:s * tt + r0 + rt, cs] = acc + b_ref[:, cs]
    y = y_ref[...]
    mu = jnp.mean(y, axis=-1, keepdims=True)
    yc = y - mu
    var = jnp.mean(yc * yc, axis=-1, keepdims=True)
    yn = yc * lax.rsqrt(var + EPS) * lg_ref[...] + lb_ref[...]
    o_ref[...] = (yn * jax.nn.sigmoid(yn)).astype(o_ref.dtype)

    @pl.when(t == n_t - 1)
    def _():
        for s in range(sb):
            nb_ref[s] = ext_ref[s, tt + HIST_OFF:tt + HIST_ROWS, :]

    if n_t > 1:
        for s in range(sb):
            ext_ref[s, 0:HIST_ROWS, :] = ext_ref[s, tt:tt + HIST_ROWS, :]


def conformer_conv(ga, gg, hist, w_dw, b_dw, ln_g, ln_b, n_seq, seq, tt, sb):
    n_t = seq // tt
    assert sb == 1 or n_t == 1
    row = lambda a: a.reshape(1, CONV_CH)
    cvec = pl.BlockSpec((1, CONV_CH), lambda n, t: (0, 0))
    tile = pl.BlockSpec((sb * tt, CONV_CH), lambda n, t: (n * n_t + t, 0))
    hist_spec = pl.BlockSpec((sb, CONV_K - 1, CONV_CH), lambda n, t: (n, 0, 0))
    return pl.pallas_call(
        functools.partial(_conv_kernel, tt=tt, n_t=n_t, sb=sb),
        grid=(n_seq // sb, n_t),
        in_specs=[tile, tile, hist_spec, pl.BlockSpec((CONV_K, CONV_CH), lambda n, t: (0, 0)),
                  cvec, cvec, cvec],
        out_specs=[tile, hist_spec],
        out_shape=[jax.ShapeDtypeStruct((n_seq * seq, CONV_CH), BF16),
                   jax.ShapeDtypeStruct((n_seq, CONV_K - 1, CONV_CH), F32)],
        scratch_shapes=[pltpu.VMEM((sb, HIST_ROWS + tt, CONV_CH), F32), pltpu.VMEM((sb * tt, CONV_CH), F32)],
        compiler_params=_cparams(2, 32),
        name="conformer_conv",
    )(ga, gg, hist, w_dw, row(b_dw), row(ln_g), row(ln_b))


def _out_proj_kernel(x_ref, a_ref, c_ref, w_ref, o_ref):
    wa = a_ref.shape[1]
    y = jnp.dot(a_ref[...].astype(BF16), w_ref[0:wa, :], preferred_element_type=F32)
    y = y + jnp.dot(c_ref[...].astype(BF16), w_ref[wa:, :], preferred_element_type=F32)
    o_ref[...] = x_ref[...] + y


def out_proj(x, a, c, w_bf, tm):
    t, d = x.shape
    return pl.pallas_call(
        _out_proj_kernel,
        grid=(t // tm,),
        in_specs=[pl.BlockSpec((tm, d), lambda i: (i, 0)),
                  pl.BlockSpec((tm, a.shape[1]), lambda i: (i, 0)),
                  pl.BlockSpec((tm, c.shape[1]), lambda i: (i, 0)),
                  pl.BlockSpec(w_bf.shape, lambda i: (0, 0))],
        out_specs=pl.BlockSpec((tm, d), lambda i: (i, 0)),
        out_shape=jax.ShapeDtypeStruct((t, d), F32),
        compiler_params=_cparams(1, 48),
        name="out_proj",
    )(x, a, c, w_bf)


def _softmax_pv(s, mv_bf):
    m = jnp.max(s, axis=-1, keepdims=True)
    p = jnp.exp(s - m)
    o = jnp.dot(p.astype(BF16), mv_bf, preferred_element_type=F32)
    return o / jnp.sum(p, axis=-1, keepdims=True)


def _cross_kernel(h_ref, g_ref, wq_ref, wo_ref, mk_ref, mv_ref, o_ref, *, n_grp, rows):
    scale = HEAD_DIM ** -0.5
    h = h_ref[...]
    hn = _rms(h, g_ref[...]).astype(BF16)
    q = jnp.dot(hn, wq_ref[...], preferred_element_type=F32)
    outs = []
    for g in range(n_grp):
        mk = mk_ref[g].astype(BF16)
        mv = mv_ref[g].astype(BF16)
        qg = q[g * rows:(g + 1) * rows]
        if rows >= 128:
            heads = []
            for hd in range(X_HEADS):
                cs = slice(hd * HEAD_DIM, (hd + 1) * HEAD_DIM)
                s = lax.dot_general(qg[:, cs].astype(BF16), mk[:, cs], NT_DIMS, preferred_element_type=F32)
                heads.append(_softmax_pv(s * scale, mv[:, cs]))
            outs.append(jnp.concatenate(heads, axis=1))
        else:
            qx = _by_head_rows(qg).astype(BF16)
            s = lax.dot_general(qx, mk, NT_DIMS, preferred_element_type=F32) * scale
            rowh = lax.broadcasted_iota(jnp.int32, s.shape, 0) // rows
            colh = lax.broadcasted_iota(jnp.int32, s.shape, 1) % X_HEADS
            o = _softmax_pv(jnp.where(rowh == colh, s, NEG), mv)
            outs.append(_by_head_lanes(o, X_HEADS))
    o_all = outs[0] if n_grp == 1 else jnp.concatenate(outs, axis=0)
    o_ref[...] = h + jnp.dot(o_all.astype(BF16), wo_ref[...], preferred_element_type=F32)


def cross_block(h, g, wq_bf, wo_bf, mk, mv, n_grp, rows, tiles_per_seq):
    t, d = h.shape
    tm = n_grp * rows
    if n_grp == 1:
        mem_map = lambda i: (i // tiles_per_seq, 0, 0)
    else:
        mem_map = lambda i: (i, 0, 0)
    mem_spec = pl.BlockSpec((n_grp,) + mk.shape[1:], mem_map)
    return pl.pallas_call(
        functools.partial(_cross_kernel, n_grp=n_grp, rows=rows),
        grid=(t // tm,),
        in_specs=[pl.BlockSpec((tm, d), lambda i: (i, 0)),
                  pl.BlockSpec((1, d), lambda i: (0, 0)),
                  pl.BlockSpec(wq_bf.shape, lambda i: (0, 0)),
                  pl.BlockSpec(wo_bf.shape, lambda i: (0, 0)),
                  mem_spec, mem_spec],
        out_specs=pl.BlockSpec((tm, d), lambda i: (i, 0)),
        out_shape=jax.ShapeDtypeStruct((t, d), F32),
        compiler_params=_cparams(1, 48),
        name="cross_block",
    )(h, g.reshape(1, d), wq_bf, wo_bf, mk, mv)


def _drain(steps):
    try:
        while True:
            next(steps)
    except StopIteration as done:
        return done.value


def _top16(s, exact_ties, want_rank=True):
    n, t = s.shape
    row = lax.broadcasted_iota(jnp.int32, (n, t), 0).astype(F32)
    row16 = lax.broadcasted_iota(jnp.int32, (PEER_TOPK, t), 0)
    rank = jnp.full((n, t), float(PEER_TOPK), F32)
    vals = jnp.zeros((PEER_TOPK, t), F32)
    for r in range(PEER_TOPK):
        m = jnp.max(s, axis=0, keepdims=True)
        if exact_ties:
            idx = jnp.min(jnp.where(s == m, row, float(n)), axis=0, keepdims=True)
            hit = row == idx
        else:
            hit = s == m
        if want_rank:
            rank = jnp.where(hit, float(r), rank)
        s = jnp.where(hit, -jnp.inf, s)
        vals = jnp.where(row16 == r, m, vals)
        yield
    return vals, rank


def _pair_pieces(v1, v2, e1, e2):
    t = v1.shape[1]
    sub = lax.broadcasted_iota(jnp.int32, (8, t), 0)
    subf = sub.astype(F32)
    pieces = []

    def col(b, a0, a_max):
        a = sub + a0
        pieces.append(dict(c=v1[a0:a0 + 8] + v2[b:b + 1], e=e1[a0:a0 + 8] * e2[b:b + 1],
                           f=(subf + a0) * 16.0 + b, ok=a <= a_max, a0=a0, row_a=None))

    def rowp(a, b0, b_min, b_max):
        b = sub + b0
        pieces.append(dict(c=v1[a:a + 1] + v2[b0:b0 + 8], e=e1[a:a + 1] * e2[b0:b0 + 8],
                           f=a * 16.0 + (subf + b0), ok=(b >= b_min) & (b <= b_max), a0=None, row_a=a))

    col(0, 0, 15), col(0, 8, 15), col(1, 0, 7), col(2, 0, 4), col(3, 0, 3)
    rowp(0, 8, 8, 15), rowp(0, 0, 4, 7), rowp(1, 0, 4, 7), rowp(2, 0, 4, 4)
    for p in pieces:
        p["c"] = jnp.where(p["ok"], p["c"], -jnp.inf)
        p["f"] = jnp.where(p["ok"], p["f"], -1.0)
    return pieces


def _route_head(s1, s2, exact_ties):
    tq = s1.shape[1]
    v1, rank1 = yield from _top16(s1, exact_ties, want_rank=exact_ties)
    v2, rank2 = yield from _top16(s2, exact_ties)
    e1 = jnp.exp(v1 - v1[0:1])
    e2 = jnp.exp(v2 - v2[0:1])
    pieces = _pair_pieces(v1, v2, e1, e2)

    taken = [jnp.zeros((8, tq), F32) for _ in pieces]
    cs = [p["c"] for p in pieces]
    for _ in range(PEER_TOPK):
        m = cs[0]
        for c in cs[1:]:
            m = jnp.maximum(m, c)
        m = jnp.max(m, axis=0, keepdims=True)
        if exact_ties:
            fm = None
            for c, p in zip(cs, pieces):
                cand = jnp.where(c == m, p["f"], 1e9)
                fm = cand if fm is None else jnp.minimum(fm, cand)
            fm = jnp.min(fm, axis=0, keepdims=True)
        for i, p in enumerate(pieces):
            hit = (p["f"] == fm) if exact_ties else (cs[i] == m)
            taken[i] = jnp.where(hit, 1.0, taken[i])
            cs[i] = jnp.where(hit, -jnp.inf, cs[i])
        yield

    row16 = lax.broadcasted_iota(jnp.int32, (PEER_TOPK, tq), 0)
    n1 = jnp.zeros((PEER_TOPK, tq), F32)
    z = jnp.zeros((1, tq), F32)
    for tf, p in zip(taken, pieces):
        z = z + jnp.sum(tf * p["e"], axis=0, keepdims=True)
        if p["row_a"] is None:
            pad = jnp.zeros((8, tq), F32)
            n1 = n1 + (jnp.concatenate([tf, pad], axis=0) if p["a0"] == 0
                       else jnp.concatenate([pad, tf], axis=0))
        else:
            n1 = n1 + jnp.where(row16 == p["row_a"], jnp.sum(tf, axis=0, keepdims=True), 0.0)

    in2 = rank2 < float(PEER_TOPK)
    cnt1 = jnp.zeros(s1.shape, F32)
    if exact_ties:
        in1 = rank1 < float(PEER_TOPK)
        for r in range(PEER_TOPK):
            cnt1 = jnp.where(rank1 == float(r), n1[r:r + 1], cnt1)
    else:
        in1 = s1 >= v1[PEER_TOPK - 1:PEER_TOPK]
        for r in range(PEER_TOPK):
            cnt1 = jnp.where(s1 == v1[r:r + 1], n1[r:r + 1], cnt1)
    p1 = jnp.where(in1, jnp.exp(s1 - v1[0:1]) / z, 0.0)
    e2_dense = jnp.where(in2, jnp.exp(s2 - v2[0:1]), 0.0)
    winners = (jnp.sum(jnp.where(in1, 1.0, 0.0), axis=0, keepdims=True)
               + jnp.sum(jnp.where(in2, 1.0, 0.0), axis=0, keepdims=True)
               + jnp.sum(n1, axis=0, keepdims=True))
    return cnt1, p1, rank2, e2_dense, winners


def _route_unit(qh, sk, tabs, exact_ties):
    half = PEER_QDIM // 2
    s1 = lax.dot_general(sk[0], qh[:, :half], NT_DIMS, preferred_element_type=F32)
    s2 = lax.dot_general(sk[1], qh[:, half:], NT_DIMS, preferred_element_type=F32)
    cnt1, p1, rank2, e2_dense, winners = yield from _route_head(s1, s2, exact_ties)
    cnt_ref, p1_ref, rk2_ref, e2_ref = tabs
    cnt_ref[...] = cnt1
    p1_ref[...] = p1
    rk2_ref[...] = rank2.astype(BF16)
    e2_ref[...] = e2_dense.astype(BF16)
    return winners


class _RouteUnits:
    def __init__(self, qh_ref, sk_ref, tabs, n):
        self.args = [(qh_ref, sk_ref, [t.at[u] for t in tabs], u) for u in range(n)]
        self.steps = [_route_unit(q[u], k[u], t, False) for q, k, t, u in self.args]
        self.winners = [None] * n

    def advance(self, k):
        for u, steps in enumerate(self.steps):
            if self.winners[u] is None:
                try:
                    for _ in range(k):
                        next(steps)
                except StopIteration as done:
                    self.winners[u] = done.value

    def finish(self):
        while any(w is None for w in self.winners):
            self.advance(1)
        for (q, k, t, u), w in zip(self.args, self.winners):
            @pl.when(jnp.max(w) > float(3 * PEER_TOPK))
            def _(q=q, k=k, t=t, u=u):
                _drain(_route_unit(q[u], k[u], t, True))


def _peer_query_kernel(h_ref, g_ref, wpq_ref, xnt_ref, qh_ref):
    hn = _rms(h_ref[...], g_ref[...])
    xnt_ref[...] = hn.T.astype(BF16)
    q = jnp.dot(hn.astype(BF16), wpq_ref[...], preferred_element_type=F32)
    for hh in range(PEER_HEADS):
        qh_ref[hh] = q[:, hh * PEER_QDIM:(hh + 1) * PEER_QDIM].astype(BF16)


def peer_query(h, g, wpq_bf):
    t, d = h.shape
    tq = QUERY_TQ
    return pl.pallas_call(
        _peer_query_kernel,
        grid=(t // tq,),
        in_specs=[pl.BlockSpec((tq, d), lambda i: (i, 0)),
                  pl.BlockSpec((1, d), lambda i: (0, 0)),
                  pl.BlockSpec(wpq_bf.shape, lambda i: (0, 0), pipeline_mode=pl.Buffered(1))],
        out_specs=[pl.BlockSpec((d, tq), lambda i: (0, i)),
                   pl.BlockSpec((PEER_HEADS, tq, PEER_QDIM), lambda i: (0, i, 0))],
        out_shape=[jax.ShapeDtypeStruct((d, t), BF16), jax.ShapeDtypeStruct((PEER_HEADS, t, PEER_QDIM), BF16)],
        compiler_params=_cparams(1, 48),
        name="peer_query",
    )(h, g.reshape(1, d), wpq_bf)


def _peer_route_kernel(qh_ref, sk_ref, cnt_ref, p1_ref, rk2_ref, e2_ref, *, n_route):
    _RouteUnits(qh_ref, sk_ref, (cnt_ref, p1_ref, rk2_ref, e2_ref), n_route).finish()


def peer_route(qh, sk_bf, n_route=2):
    t = qh.shape[1]
    hps = PEER_HEADS // n_route
    tab = pl.BlockSpec((n_route, PEER_NKEYS, ROUTE_TQ), lambda i, j: (j, 0, i))
    return pl.pallas_call(
        functools.partial(_peer_route_kernel, n_route=n_route),
        grid=(t // ROUTE_TQ, hps),
        in_specs=[pl.BlockSpec((n_route, ROUTE_TQ, PEER_QDIM), lambda i, j: (j, i, 0)),
                  pl.BlockSpec((n_route, 2, PEER_NKEYS, PEER_QDIM // 2), lambda i, j: (j, 0, 0, 0))],
        out_specs=[tab] * 4,
        out_shape=[jax.ShapeDtypeStruct((PEER_HEADS, PEER_NKEYS, t), dt) for dt in (F32, F32, BF16, BF16)],
        compiler_params=_cparams(2, 32),
        name="peer_route",
    )(qh, sk_bf)


def _peer_dense_kernel(xnt_ref, u_ref, vt_ref, cnt_ref, p1_ref, rk2_ref, e2_ref, h_ref, fg_ref, y_ref, acc_ref,
                       *, te, n_e):
    j = pl.program_id(1)

    @pl.when(j == 0)
    def _():
        acc_ref[...] = jnp.zeros(acc_ref.shape, F32)

    tq = xnt_ref.shape[1]
    groups = te // PEER_NKEYS
    gpc = PEER_ROW_CHUNK // PEER_NKEYS
    parts = []
    for a in range(groups):
        if a % gpc == 0:
            rows = slice(a * PEER_NKEYS, (a + gpc) * PEER_NKEYS)
            act = jax.nn.gelu(jnp.dot(u_ref[rows, :], xnt_ref[...], preferred_element_type=F32).astype(BF16))
        i1 = j * groups + a
        gate = jnp.zeros((PEER_NKEYS, tq), BF16)
        for hh in range(PEER_HEADS):
            rep = lambda ref: jnp.concatenate(
                [jnp.broadcast_to(ref[hh, pl.ds(i1, 1), :], (16, tq)).astype(BF16)] * (PEER_NKEYS // 16), axis=0)
            gate = gate + jnp.where(rk2_ref[hh] < rep(cnt_ref), rep(p1_ref) * e2_ref[hh], jnp.zeros((), BF16))
        lo = (a % gpc) * PEER_NKEYS
        parts.append(act[lo:lo + PEER_NKEYS] * gate)
    hm_t = parts[0] if groups == 1 else jnp.concatenate(parts, axis=0)
    acc_ref[...] += jnp.dot(vt_ref[...], hm_t, preferred_element_type=F32)

    @pl.when(j == n_e - 1)
    def _():
        y_ref[...] = _rms(h_ref[...] + acc_ref[...].T, fg_ref[...])


def peer_dense(xnt, u_bf, vt_bf, tabs, h, final_g, tq, te):
    t, d = h.shape
    n_e = u_bf.shape[0] // te
    once = dict(pipeline_mode=pl.Buffered(1))
    tab = pl.BlockSpec((PEER_HEADS, PEER_NKEYS, tq), lambda i, j: (0, 0, i), **once)
    return pl.pallas_call(
        functools.partial(_peer_dense_kernel, te=te, n_e=n_e),
        grid=(t // tq, n_e),
        in_specs=[pl.BlockSpec((d, tq), lambda i, j: (0, i), **once),
                  pl.BlockSpec((te, d), lambda i, j: (j, 0)),
                  pl.BlockSpec((d, te), lambda i, j: (0, j)),
                  tab, tab, tab, tab,
                  pl.BlockSpec((tq, d), lambda i, j: (i, 0), **once),
                  pl.BlockSpec((1, d), lambda i, j: (0, 0))],
        out_specs=pl.BlockSpec((tq, d), lambda i, j: (i, 0)),
        out_shape=jax.ShapeDtypeStruct((t, d), F32),
        scratch_shapes=[pltpu.VMEM((d, tq), F32)],
        compiler_params=_cparams(2, 56),
        name="peer_dense",
    )(xnt, u_bf, vt_bf, *tabs, h, final_g.reshape(1, d))


def kernel(x_prompt, x_sample, mem_prompt, cache_k, cache_v, cache_conv, cache_mem_k, cache_mem_v, page_table,
           norm1_g, w_in, rel_bias, conv_w, conv_b, conv_ln_g, conv_ln_b, w_out, norm2_g, mem_norm_g, w_cq,
           w_mk, w_mv, w_co, norm3_g, w_pq, peer_sub_keys, peer_u, peer_v, final_g):
    depth = w_in.shape[0]
    assert depth == 1, "single-layer step"
    n_p, seq, d = x_prompt.shape
    n_s, t_s, _ = x_sample.shape
    l = 0
    bf = lambda a: a.astype(BF16)

    w_in_bf, w_out_bf = bf(w_in[l]), bf(w_out[l])
    w_cq_bf, w_co_bf, w_pq_bf = bf(w_cq[l]), bf(w_co[l]), bf(w_pq[l])
    w_mem_bf = bf(jnp.concatenate([w_mk[l], w_mv[l]], axis=1))
    sk_bf, u_bf, vt_bf = bf(peer_sub_keys[l]), bf(peer_u[l]), bf(peer_v[l].T)

    n_pool, n_pages = cache_k.shape[1], page_table.shape[1]
    bias_p, t_far, t_last, t_own = bias_tiles(rel_bias, n_pages * PAGE_SIZE)
    c31_h = rel_bias[NUM_BUCKETS - 1]

    def cross_and_query(x, attn, conv, mk, mv, n_grp, rows, tiles_per_seq):
        h = out_proj(x, attn, conv, w_out_bf, tm=OUT_PROJ_ROWS)
        h = cross_block(h, norm2_g[l], w_cq_bf, w_co_bf, mk, mv, n_grp, rows, tiles_per_seq)
        return (h,) + tuple(peer_query(h, norm3_g[l], w_pq_bf))

    experts = lambda xnt, tabs, h: peer_dense(xnt, u_bf, vt_bf, tabs, h, final_g,
                                              tq=PEER_TOKEN_TILE, te=PEER_EXPERT_TILE)

    xp = x_prompt.reshape(n_p * seq, d)
    mk_p, mv_p = rms_matmul(mem_prompt.reshape(n_p * N_MEM, d), mem_norm_g[l], w_mem_bf, 2, tm=MEM_PROJ_ROWS)
    q_p, k_p, v_p, ga_p, gg_p = rms_matmul(xp, norm1_g[l], w_in_bf, 5, tm=IN_PROJ_ROWS)
    attn_p = moba_prompt(q_p, k_p, v_p, bias_p, c31_h, n_p, seq)
    zero_buf = jnp.zeros((n_p, CONV_K - 1, CONV_CH), F32)
    conv_p, buf_p = conformer_conv(ga_p, gg_p, zero_buf, conv_w[l], conv_b[l], conv_ln_g[l], conv_ln_b[l],
                                   n_p, seq, tt=CONV_ROWS, sb=1)
    h_p, xnt_p, qh_p = cross_and_query(xp, attn_p, conv_p, mk_p.reshape(n_p, N_MEM, X_WIDTH),
                                       mv_p.reshape(n_p, N_MEM, X_WIDTH), 1, CROSS_ROWS, seq // CROSS_ROWS)

    xs = x_sample.reshape(n_s * t_s, d)
    q_s, k_s, v_s, ga_s, gg_s = rms_matmul(xs, norm1_g[l], w_in_bf, 5, tm=IN_PROJ_ROWS)
    tok3 = lambda a: a.reshape(n_s, t_s, ATTN_WIDTH)
    units_p = PEER_HEADS * (n_p * seq // ROUTE_TQ)
    fuse = units_p % n_s == 0 and PEER_HEADS % (units_p // n_s) == 0
    attn_s, *tabs_p = moba_sample(tok3(q_s), tok3(k_s), tok3(v_s),
                                  cache_k[l].reshape(n_pool, PAGE_SIZE * N_HEADS, HEAD_DIM),
                                  cache_v[l].reshape(n_pool, PAGE_SIZE * N_HEADS, HEAD_DIM),
                                  page_table, t_far, t_last, t_own, route=(qh_p, sk_bf) if fuse else None)
    if not fuse:
        tabs_p = peer_route(qh_p, sk_bf)
    y_p = experts(xnt_p, tabs_p, h_p)
    conv_s, buf_s = conformer_conv(ga_s, gg_s, cache_conv[l], conv_w[l], conv_b[l], conv_ln_g[l], conv_ln_b[l],
                                   n_s, t_s, tt=t_s, sb=SHORT_SEQS_PER_STEP)
    h_s, xnt_s, qh_s = cross_and_query(xs, attn_s.reshape(n_s * t_s, ATTN_WIDTH), conv_s,
                                       cache_mem_k[l].reshape(n_s, N_MEM * X_HEADS, HEAD_DIM),
                                       cache_mem_v[l].reshape(n_s, N_MEM * X_HEADS, HEAD_DIM),
                                       SHORT_SEQS_PER_STEP, t_s, 1)
    y_s = experts(xnt_s, peer_route(qh_s, sk_bf), h_s)

    n_pg = seq // PAGE_SIZE
    return (y_p.reshape(n_p, seq, d), y_s.reshape(n_s, t_s, d),
            k_p.reshape(1, n_p, n_pg, PAGE_SIZE, N_HEADS, HEAD_DIM),
            v_p.reshape(1, n_p, n_pg, PAGE_SIZE, N_HEADS, HEAD_DIM),
            buf_p[None],
            mk_p.reshape(1, n_p, N_MEM, X_HEADS, HEAD_DIM), mv_p.reshape(1, n_p, N_MEM, X_HEADS, HEAD_DIM),
            k_s.reshape(1, n_s, t_s, N_HEADS, HEAD_DIM), v_s.reshape(1, n_s, t_s, N_HEADS, HEAD_DIM),
            buf_s[None])
```

```python
import functools
import math

import numpy as np
import jax
import jax.numpy as jnp
from jax import lax
from jax.experimental import pallas as pl
from jax.experimental.pallas import tpu as pltpu

F32 = jnp.float32
BF16 = jnp.bfloat16

D_MODEL = 2048
HEAD_DIM = 128
N_HEADS = 8
ATTN_WIDTH = N_HEADS * HEAD_DIM
CONV_CH = 1024
MOBA_BLOCK = 256
MOBA_TOPK = 3
PAGE_SIZE = 128
NUM_BUCKETS = 32
MAX_EXACT = 16
REL_MAX_DIST = 128
CONV_K = 31
N_MEM = 256
X_HEADS = 4
X_WIDTH = X_HEADS * HEAD_DIM
PEER_HEADS = 8
PEER_NKEYS = 128
PEER_TOPK = 16
PEER_QDIM = 256
PEER_ROW_CHUNK = 512
ROUTE_TQ = 256
QUERY_TQ = 256
ROUTE_STEPS_PER_PAGE = 2
EPS = 1e-6
NEG = -1e30

IN_PROJ_ROWS = 256
MEM_PROJ_ROWS = 512
OUT_PROJ_ROWS = 512
CONV_ROWS = 256
SHORT_SEQS_PER_STEP = 8
CROSS_ROWS = 256
PEER_TOKEN_TILE = 512
PEER_EXPERT_TILE = 1024

MIB = 1024 * 1024
NT_DIMS = (((1,), (1,)), ((), ()))


def _cparams(n_grid, vmem_mib):
    return pltpu.CompilerParams(dimension_semantics=("arbitrary",) * n_grid,
                                vmem_limit_bytes=vmem_mib * MIB)


def _rms(x, g):
    return x * lax.rsqrt(jnp.mean(x * x, axis=-1, keepdims=True) + EPS) * g


def _rms_matmul_kernel(x_ref, g_ref, w_ref, *outs):
    xn = _rms(x_ref[...], g_ref[...]).astype(BF16)
    tn = outs[0].shape[1]
    for s, o_ref in enumerate(outs):
        o_ref[...] = jnp.dot(xn, w_ref[:, s * tn:(s + 1) * tn], preferred_element_type=F32)


def rms_matmul(x, g, w_bf, n_out, tm):
    t, d = x.shape
    tn = w_bf.shape[1] // n_out
    return pl.pallas_call(
        _rms_matmul_kernel,
        grid=(t // tm,),
        in_specs=[pl.BlockSpec((tm, d), lambda i: (i, 0)),
                  pl.BlockSpec((1, d), lambda i: (0, 0)),
                  pl.BlockSpec(w_bf.shape, lambda i: (0, 0), pipeline_mode=pl.Buffered(1))],
        out_specs=[pl.BlockSpec((tm, tn), lambda i: (i, 0))] * n_out,
        out_shape=[jax.ShapeDtypeStruct((t, tn), F32)] * n_out,
        compiler_params=_cparams(1, 56),
        name="rms_matmul",
    )(x, g.reshape(1, d), w_bf)


def _bucket_np(rel):
    n = np.maximum(rel, 0)
    nf = np.maximum(n, 1).astype(np.float32)
    large = MAX_EXACT + (np.log(nf / MAX_EXACT) / np.float32(math.log(REL_MAX_DIST / MAX_EXACT))
                         * (NUM_BUCKETS - MAX_EXACT)).astype(np.int32)
    large = np.minimum(large, NUM_BUCKETS - 1)
    return np.where(n < MAX_EXACT, n, large).astype(np.int32)


def _bias_kernel(rb_ref, rbx_ref, bkp_ref, bkf_ref, bkl_ref, bko_ref, tp_ref, tf_ref, tl_ref, to_ref):
    def lookup(bk, table):
        acc = jnp.full(bk.shape, NEG, F32)
        for b in range(NUM_BUCKETS):
            acc = jnp.where(bk == b, table(b), acc)
        return acc

    for h in range(N_HEADS):
        for t in range(2):
            tp_ref[h, t] = lookup(bkp_ref[t], lambda b: rb_ref[b, h])
    by_row = lambda b: rbx_ref[b]
    tf_ref[...] = lookup(bkf_ref[...], by_row)
    tl_ref[...] = lookup(bkl_ref[...], by_row)
    to_ref[...] = lookup(bko_ref[...], by_row)


def bias_tiles(rel_bias, n_past):
    key = np.arange(MOBA_BLOCK)[:, None]
    qry = np.arange(MOBA_BLOCK)[None, :]
    bkp = np.stack([_bucket_np(qry - key), _bucket_np(MOBA_BLOCK + qry - key)])
    r = np.arange(N_HEADS * 8)[:, None]
    rh, rq = r // 8, r % 8
    c = np.arange(PAGE_SIZE * N_HEADS)[None, :]
    ct, ch = c // N_HEADS, c % N_HEADS
    bkf = np.where(rh == ch, NUM_BUCKETS - 1, -1)
    bkl = np.where(rh == ch, _bucket_np(n_past + rq - (n_past - PAGE_SIZE + ct)), -1)
    assert PAGE_SIZE >= REL_MAX_DIST
    co = np.arange(128)[None, :]
    coh, cot = co // 8, co % 8
    bko = np.where((rh == coh) & (cot <= rq), _bucket_np(rq - cot), -1)
    rbx = jnp.repeat(rel_bias, 8, axis=1)[:, :, None]
    vm = pl.BlockSpec(memory_space=pltpu.VMEM)
    i32 = lambda a: jnp.asarray(a.astype(np.int32))
    return pl.pallas_call(
        _bias_kernel,
        in_specs=[pl.BlockSpec(memory_space=pltpu.SMEM), vm, vm, vm, vm, vm],
        out_specs=[vm, vm, vm, vm],
        out_shape=[jax.ShapeDtypeStruct((N_HEADS, 2, MOBA_BLOCK, MOBA_BLOCK), F32),
                   jax.ShapeDtypeStruct(bkf.shape, F32),
                   jax.ShapeDtypeStruct(bkl.shape, F32),
                   jax.ShapeDtypeStruct(bko.shape, F32)],
        compiler_params=pltpu.CompilerParams(vmem_limit_bytes=32 * MIB),
        name="bias_tiles",
    )(rel_bias, rbx, i32(bkp), i32(bkf), i32(bkl), i32(bko))


def _split_bf16(x):
    hi = x.astype(BF16)
    lo = (x - hi.astype(F32)).astype(BF16)
    return hi, lo


def _moba_prompt_kernel(q_ref, k_ref, v_ref, bias_ref, c31_ref, o_ref, qbf, kbf, vt, pen_ref, s_ref):
    nb = kbf.shape[0] // MOBA_BLOCK
    scale = HEAD_DIM ** -0.5
    blk = lambda b: slice(b * MOBA_BLOCK, (b + 1) * MOBA_BLOCK)

    q = q_ref[...]
    q_hi, q_lo = _split_bf16(q)
    qbf[...] = q_hi
    kbf[...] = k_ref[...].astype(BF16)
    km = jnp.concatenate([jnp.mean(k_ref[blk(b), :], axis=0, keepdims=True) for b in range(nb)]
                         + [jnp.zeros((16 - nb, HEAD_DIM), F32)], axis=0)
    for b in range(nb):
        vt[b] = v_ref[blk(b), :].T.astype(BF16)
    km_hi, km_lo = _split_bf16(km)
    gate = (lax.dot_general(km_hi, q_hi, NT_DIMS, preferred_element_type=F32)
            + lax.dot_general(km_lo, q_hi, NT_DIMS, preferred_element_type=F32)
            + lax.dot_general(km_hi, q_lo, NT_DIMS, preferred_element_type=F32))

    row = lax.broadcasted_iota(jnp.int32, gate.shape, 0)
    own = lax.broadcasted_iota(jnp.int32, gate.shape, 1) // MOBA_BLOCK
    rank = jnp.zeros(gate.shape, F32)
    for b2 in range(nb):
        gb = gate[b2:b2 + 1, :]
        beats = ((gb > gate) | ((gb == gate) & (b2 < row))) & (b2 < own)
        rank = rank + jnp.where(beats, 1.0, 0.0)
    pen_ref[...] = jnp.where((row < own) & (rank < float(MOBA_TOPK)), 0.0, NEG)

    key = lax.broadcasted_iota(jnp.int32, (MOBA_BLOCK, MOBA_BLOCK), 0)
    qry = lax.broadcasted_iota(jnp.int32, (MOBA_BLOCK, MOBA_BLOCK), 1)
    c31 = c31_ref[pl.program_id(1)]
    for qi in range(nb):
        m = None
        for kb in range(qi + 1):
            s = lax.dot_general(kbf[blk(kb), :], qbf[blk(qi), :], NT_DIMS, preferred_element_type=F32) * scale
            if kb == qi:
                s = jnp.where(key <= qry, s + bias_ref[0, 0], NEG)
            else:
                s = s + (bias_ref[0, 1] if kb == qi - 1 else c31) + pen_ref[kb:kb + 1, blk(qi)]
            s_ref[kb] = s
            cm = jnp.max(s, axis=0, keepdims=True)
            m = cm if m is None else jnp.maximum(m, cm)
        lsum = jnp.zeros((1, MOBA_BLOCK), F32)
        acc = jnp.zeros((HEAD_DIM, MOBA_BLOCK), F32)
        for kb in range(qi + 1):
            p = jnp.exp(s_ref[kb] - m)
            lsum = lsum + jnp.sum(p, axis=0, keepdims=True)
            acc = acc + jnp.dot(vt[kb], p.astype(BF16), preferred_element_type=F32)
        o_ref[blk(qi), :] = (acc / lsum).T.astype(o_ref.dtype)


def moba_prompt(q, k, v, bias_p, c31_h, n_seq, seq):
    nq = seq // MOBA_BLOCK
    tok = pl.BlockSpec((seq, HEAD_DIM), lambda n, h: (n, h))
    return pl.pallas_call(
        _moba_prompt_kernel,
        grid=(n_seq, N_HEADS),
        in_specs=[tok, tok, tok,
                  pl.BlockSpec((1, 2, MOBA_BLOCK, MOBA_BLOCK), lambda n, h: (h, 0, 0, 0)),
                  pl.BlockSpec(memory_space=pltpu.SMEM)],
        out_specs=tok,
        out_shape=jax.ShapeDtypeStruct(q.shape, BF16),
        scratch_shapes=[pltpu.VMEM((seq, HEAD_DIM), BF16), pltpu.VMEM((seq, HEAD_DIM), BF16),
                        pltpu.VMEM((nq, HEAD_DIM, MOBA_BLOCK), BF16), pltpu.VMEM((16, seq), F32),
                        pltpu.VMEM((nq, MOBA_BLOCK, MOBA_BLOCK), F32)],
        compiler_params=_cparams(2, 48),
        name="moba_prompt",
    )(q, k, v, bias_p, c31_h)


def _by_head_rows(x):
    return jnp.concatenate([x[:, h * HEAD_DIM:(h + 1) * HEAD_DIM] for h in range(x.shape[1] // HEAD_DIM)], axis=0)


def _by_head_lanes(x, n_heads):
    t = x.shape[0] // n_heads
    return jnp.concatenate([x[h * t:(h + 1) * t] for h in range(n_heads)], axis=1)


def _moba_sample_kernel(pt_ref, q_ref, kn_ref, vn_ref, tf_ref, tl_ref, to_ref, *refs, n_pages, n_route):
    kp, vp = refs[:n_pages], refs[n_pages:2 * n_pages]
    refs = refs[2 * n_pages:]
    if n_route:
        qh_ref, sk_ref, o_ref, cnt_ref, p1_ref, rk2_ref, e2_ref, s_ref = refs
    else:
        o_ref, s_ref = refs
    ppb = MOBA_BLOCK // PAGE_SIZE
    nb = n_pages // ppb
    scale = HEAD_DIM ** -0.5
    t_new = q_ref.shape[1]
    nr = N_HEADS * t_new

    route = _RouteUnits(qh_ref, sk_ref, (cnt_ref, p1_ref, rk2_ref, e2_ref), n_route) if n_route else None
    interleave = (lambda: route.advance(ROUTE_STEPS_PER_PAGE)) if n_route else (lambda: None)

    q = _by_head_rows(q_ref[0])
    q_bf = (q * scale).astype(BF16)
    zpad = jnp.zeros((128 - nr, HEAD_DIM), F32)
    kn = jnp.concatenate([_by_head_rows(kn_ref[0]), zpad], axis=0).astype(BF16)
    vn = jnp.concatenate([_by_head_rows(vn_ref[0]), zpad], axis=0).astype(BF16)

    s_own = lax.dot_general(q_bf, kn, NT_DIMS, preferred_element_type=F32) + to_ref[...]
    blk_max, gates = [], []
    for b in range(nb):
        bm = None
        ksum = jnp.zeros((N_HEADS, HEAD_DIM), F32)
        for pg in range(b * ppb, (b + 1) * ppb):
            kpage = kp[pg][...]
            ksum = ksum + jnp.sum(kpage.reshape(PAGE_SIZE, N_HEADS, HEAD_DIM), axis=0)
            s = lax.dot_general(q_bf, kpage.astype(BF16), NT_DIMS, preferred_element_type=F32)
            s = s + (tl_ref[...] if pg == n_pages - 1 else tf_ref[...])
            s_ref[pg] = s
            pm = jnp.max(s, axis=-1, keepdims=True)
            bm = pm if bm is None else jnp.maximum(bm, pm)
            interleave()
        blk_max.append(bm)
        kmean = ksum / float(MOBA_BLOCK)
        krep = jnp.concatenate([jnp.broadcast_to(kmean[h:h + 1, :], (t_new, HEAD_DIM)) for h in range(N_HEADS)],
                               axis=0)
        gates.append(jnp.sum(q * krep, axis=1, keepdims=True))

    penalty = []
    for b in range(nb):
        rank = jnp.zeros((nr, 1), F32)
        for b2 in range(nb):
            if b2 != b:
                beats = (gates[b2] > gates[b]) | ((gates[b2] == gates[b]) & (b2 < b))
                rank = rank + jnp.where(beats, 1.0, 0.0)
        penalty.append(jnp.where(rank < float(MOBA_TOPK), 0.0, NEG))

    m = jnp.max(s_own, axis=-1, keepdims=True)
    for b in range(nb):
        m = jnp.maximum(m, blk_max[b] + penalty[b])

    p = jnp.exp(s_own - m)
    lsum = jnp.sum(p, axis=-1, keepdims=True)
    acc = jnp.dot(p.astype(BF16), vn, preferred_element_type=F32)
    for pg in range(n_pages):
        p = jnp.exp(s_ref[pg] + (penalty[pg // ppb] - m))
        lsum = lsum + jnp.sum(p, axis=-1, keepdims=True)
        acc = acc + jnp.dot(p.astype(BF16), vp[pg][...].astype(BF16), preferred_element_type=F32)
        interleave()
    o_ref[0] = _by_head_lanes(acc / lsum, N_HEADS)

    if n_route:
        route.finish()


def moba_sample(q, kn, vn, ck, cv, page_table, t_far, t_last, t_own, route=None):
    nseq, n_pages = page_table.shape
    tok = pl.BlockSpec((1,) + q.shape[1:], lambda b, pt: (b, 0, 0))

    def page_spec(p):
        return pl.BlockSpec((None,) + ck.shape[1:], lambda b, pt, p=p: (pt[b, p], 0, 0))

    const = lambda a: pl.BlockSpec(a.shape, lambda b, pt: (0, 0))
    in_specs = ([tok, tok, tok, const(t_far), const(t_last), const(t_own)]
                + [page_spec(p) for p in range(n_pages)] * 2)
    out_specs, out_shape, extra, n_route = [tok], [jax.ShapeDtypeStruct(q.shape, F32)], [], 0
    if route is not None:
        qh, sk_bf = route
        t = qh.shape[1]
        units = PEER_HEADS * (t // ROUTE_TQ)
        n_route = units // nseq
        assert n_route * nseq == units and PEER_HEADS % n_route == 0
        hps = PEER_HEADS // n_route
        in_specs += [pl.BlockSpec((n_route, ROUTE_TQ, PEER_QDIM), lambda b, pt: (b % hps, b // hps, 0)),
                     pl.BlockSpec((n_route, 2, PEER_NKEYS, PEER_QDIM // 2), lambda b, pt: (b % hps, 0, 0, 0))]
        tab = pl.BlockSpec((n_route, PEER_NKEYS, ROUTE_TQ), lambda b, pt: (b % hps, 0, b // hps))
        out_specs += [tab] * 4
        out_shape += [jax.ShapeDtypeStruct((PEER_HEADS, PEER_NKEYS, t), dt) for dt in (F32, F32, BF16, BF16)]
        extra = [qh, sk_bf]
    grid_spec = pltpu.PrefetchScalarGridSpec(
        num_scalar_prefetch=1, grid=(nseq,), in_specs=in_specs, out_specs=out_specs,
        scratch_shapes=[pltpu.VMEM((n_pages, N_HEADS * q.shape[1], ck.shape[1]), F32)])
    return pl.pallas_call(
        functools.partial(_moba_sample_kernel, n_pages=n_pages, n_route=n_route),
        grid_spec=grid_spec,
        out_shape=out_shape,
        compiler_params=_cparams(1, 56),
        name="moba_sample",
    )(page_table, q, kn, vn, t_far, t_last, t_own, *([ck] * n_pages), *([cv] * n_pages), *extra)


HIST_ROWS = 32
HIST_OFF = HIST_ROWS - (CONV_K - 1)


def _conv_kernel(ga_ref, gg_ref, hist_ref, w_ref, b_ref, lg_ref, lb_ref, o_ref, nb_ref, ext_ref, y_ref,
                 *, tt, n_t, sb):
    t = pl.program_id(1)

    @pl.when(t == 0)
    def _():
        for s in range(sb):
            ext_ref[s, HIST_OFF:HIST_ROWS, :] = hist_ref[s]

    rt = min(tt, 128)
    for s in range(sb):
        rows = slice(s * tt, (s + 1) * tt)
        ext_ref[s, HIST_ROWS:HIST_ROWS + tt, :] = ga_ref[rows, :] * jax.nn.sigmoid(gg_ref[rows, :])
        for c in range(CONV_CH // 128):
            cs = slice(c * 128, (c + 1) * 128)
            for r0 in range(0, tt, rt):
                acc = jnp.zeros((rt, 128), F32)
                for res in range(8):
                    taps = [j for j in range(CONV_K) if (HIST_OFF + j) % 8 == res]
                    q0 = (HIST_OFF + taps[0]) // 8
                    q1 = (HIST_OFF + taps[-1]) // 8
                    if res:
                        win = ext_ref[s, r0 + 8 * q0:r0 + 8 * (q1 + 1) + rt, cs]
                        win = pltpu.roll(win, win.shape[0] - res, axis=0)
                    else:
                        win = ext_ref[s, r0 + 8 * q0:r0 + 8 * q1 + rt, cs]
                    for j in taps:
                        off = 8 * ((HIST_OFF + j) // 8 - q0)
                        acc = acc + w_ref[j:j + 1, cs] * win[off:off + rt]
                y_ref[s * tt + r0:s * tt + r0 + rt, cs] = acc + b_ref[:, cs]
    y = y_ref[...]
    mu = jnp.mean(y, axis=-1, keepdims=True)
    yc = y - mu
    var = jnp.mean(yc * yc, axis=-1, keepdims=True)
    yn = yc * lax.rsqrt(var + EPS) * lg_ref[...] + lb_ref[...]
    o_ref[...] = (yn * jax.nn.sigmoid(yn)).astype(o_ref.dtype)

    @pl.when(t == n_t - 1)
    def _():
        for s in range(sb):
            nb_ref[s] = ext_ref[s, tt + HIST_OFF:tt + HIST_ROWS, :]

    if n_t > 1:
        for s in range(sb):
            ext_ref[s, 0:HIST_ROWS, :] = ext_ref[s, tt:tt + HIST_ROWS, :]


def conformer_conv(ga, gg, hist, w_dw, b_dw, ln_g, ln_b, n_seq, seq, tt, sb):
    n_t = seq // tt
    assert sb == 1 or n_t == 1
    row = lambda a: a.reshape(1, CONV_CH)
    cvec = pl.BlockSpec((1, CONV_CH), lambda n, t: (0, 0))
    tile = pl.BlockSpec((sb * tt, CONV_CH), lambda n, t: (n * n_t + t, 0))
    hist_spec = pl.BlockSpec((sb, CONV_K - 1, CONV_CH), lambda n, t: (n, 0, 0))
    return pl.pallas_call(
        functools.partial(_conv_kernel, tt=tt, n_t=n_t, sb=sb),
        grid=(n_seq // sb, n_t),
        in_specs=[tile, tile, hist_spec, pl.BlockSpec((CONV_K, CONV_CH), lambda n, t: (0, 0)),
                  cvec, cvec, cvec],
        out_specs=[tile, hist_spec],
        out_shape=[jax.ShapeDtypeStruct((n_seq * seq, CONV_CH), BF16),
                   jax.ShapeDtypeStruct((n_seq, CONV_K - 1, CONV_CH), F32)],
        scratch_shapes=[pltpu.VMEM((sb, HIST_ROWS + tt, CONV_CH), F32), pltpu.VMEM((sb * tt, CONV_CH), F32)],
        compiler_params=_cparams(2, 32),
        name="conformer_conv",
    )(ga, gg, hist, w_dw, row(b_dw), row(ln_g), row(ln_b))


def _out_proj_kernel(x_ref, a_ref, c_ref, w_ref, o_ref):
    wa = a_ref.shape[1]
    y = jnp.dot(a_ref[...].astype(BF16), w_ref[0:wa, :], preferred_element_type=F32)
    y = y + jnp.dot(c_ref[...].astype(BF16), w_ref[wa:, :], preferred_element_type=F32)
    o_ref[...] = x_ref[...] + y


def out_proj(x, a, c, w_bf, tm):
    t, d = x.shape
    return pl.pallas_call(
        _out_proj_kernel,
        grid=(t // tm,),
        in_specs=[pl.BlockSpec((tm, d), lambda i: (i, 0)),
                  pl.BlockSpec((tm, a.shape[1]), lambda i: (i, 0)),
                  pl.BlockSpec((tm, c.shape[1]), lambda i: (i, 0)),
                  pl.BlockSpec(w_bf.shape, lambda i: (0, 0))],
        out_specs=pl.BlockSpec((tm, d), lambda i: (i, 0)),
        out_shape=jax.ShapeDtypeStruct((t, d), F32),
        compiler_params=_cparams(1, 48),
        name="out_proj",
    )(x, a, c, w_bf)


def _softmax_pv(s, mv_bf):
    m = jnp.max(s, axis=-1, keepdims=True)
    p = jnp.exp(s - m)
    o = jnp.dot(p.astype(BF16), mv_bf, preferred_element_type=F32)
    return o / jnp.sum(p, axis=-1, keepdims=True)


def _cross_kernel(h_ref, g_ref, wq_ref, wo_ref, mk_ref, mv_ref, o_ref, *, n_grp, rows):
    scale = HEAD_DIM ** -0.5
    h = h_ref[...]
    hn = _rms(h, g_ref[...]).astype(BF16)
    q = jnp.dot(hn, wq_ref[...], preferred_element_type=F32)
    outs = []
    for g in range(n_grp):
        mk = mk_ref[g].astype(BF16)
        mv = mv_ref[g].astype(BF16)
        qg = q[g * rows:(g + 1) * rows]
        if rows >= 128:
            heads = []
            for hd in range(X_HEADS):
                cs = slice(hd * HEAD_DIM, (hd + 1) * HEAD_DIM)
                s = lax.dot_general(qg[:, cs].astype(BF16), mk[:, cs], NT_DIMS, preferred_element_type=F32)
                heads.append(_softmax_pv(s * scale, mv[:, cs]))
            outs.append(jnp.concatenate(heads, axis=1))
        else:
            qx = _by_head_rows(qg).astype(BF16)
            s = lax.dot_general(qx, mk, NT_DIMS, preferred_element_type=F32) * scale
            rowh = lax.broadcasted_iota(jnp.int32, s.shape, 0) // rows
            colh = lax.broadcasted_iota(jnp.int32, s.shape, 1) % X_HEADS
            o = _softmax_pv(jnp.where(rowh == colh, s, NEG), mv)
            outs.append(_by_head_lanes(o, X_HEADS))
    o_all = outs[0] if n_grp == 1 else jnp.concatenate(outs, axis=0)
    o_ref[...] = h + jnp.dot(o_all.astype(BF16), wo_ref[...], preferred_element_type=F32)


def cross_block(h, g, wq_bf, wo_bf, mk, mv, n_grp, rows, tiles_per_seq):
    t, d = h.shape
    tm = n_grp * rows
    if n_grp == 1:
        mem_map = lambda i: (i // tiles_per_seq, 0, 0)
    else:
        mem_map = lambda i: (i, 0, 0)
    mem_spec = pl.BlockSpec((n_grp,) + mk.shape[1:], mem_map)
    return pl.pallas_call(
        functools.partial(_cross_kernel, n_grp=n_grp, rows=rows),
        grid=(t // tm,),
        in_specs=[pl.BlockSpec((tm, d), lambda i: (i, 0)),
                  pl.BlockSpec((1, d), lambda i: (0, 0)),
                  pl.BlockSpec(wq_bf.shape, lambda i: (0, 0)),
                  pl.BlockSpec(wo_bf.shape, lambda i: (0, 0)),
                  mem_spec, mem_spec],
        out_specs=pl.BlockSpec((tm, d), lambda i: (i, 0)),
        out_shape=jax.ShapeDtypeStruct((t, d), F32),
        compiler_params=_cparams(1, 48),
        name="cross_block",
    )(h, g.reshape(1, d), wq_bf, wo_bf, mk, mv)


def _drain(steps):
    try:
        while True:
            next(steps)
    except StopIteration as done:
        return done.value


def _top16(s, exact_ties, want_rank=True):
    n, t = s.shape
    row = lax.broadcasted_iota(jnp.int32, (n, t), 0).astype(F32)
    row16 = lax.broadcasted_iota(jnp.int32, (PEER_TOPK, t), 0)
    rank = jnp.full((n, t), float(PEER_TOPK), F32)
    vals = jnp.zeros((PEER_TOPK, t), F32)
    for r in range(PEER_TOPK):
        m = jnp.max(s, axis=0, keepdims=True)
        if exact_ties:
            idx = jnp.min(jnp.where(s == m, row, float(n)), axis=0, keepdims=True)
            hit = row == idx
        else:
            hit = s == m
        if want_rank:
            rank = jnp.where(hit, float(r), rank)
        s = jnp.where(hit, -jnp.inf, s)
        vals = jnp.where(row16 == r, m, vals)
        yield
    return vals, rank


def _pair_pieces(v1, v2, e1, e2):
    t = v1.shape[1]
    sub = lax.broadcasted_iota(jnp.int32, (8, t), 0)
    subf = sub.astype(F32)
    pieces = []

    def col(b, a0, a_max):
        a = sub + a0
        pieces.append(dict(c=v1[a0:a0 + 8] + v2[b:b + 1], e=e1[a0:a0 + 8] * e2[b:b + 1],
                           f=(subf + a0) * 16.0 + b, ok=a <= a_max, a0=a0, row_a=None))

    def rowp(a, b0, b_min, b_max):
        b = sub + b0
        pieces.append(dict(c=v1[a:a + 1] + v2[b0:b0 + 8], e=e1[a:a + 1] * e2[b0:b0 + 8],
                           f=a * 16.0 + (subf + b0), ok=(b >= b_min) & (b <= b_max), a0=None, row_a=a))

    col(0, 0, 15), col(0, 8, 15), col(1, 0, 7), col(2, 0, 4), col(3, 0, 3)
    rowp(0, 8, 8, 15), rowp(0, 0, 4, 7), rowp(1, 0, 4, 7), rowp(2, 0, 4, 4)
    for p in pieces:
        p["c"] = jnp.where(p["ok"], p["c"], -jnp.inf)
        p["f"] = jnp.where(p["ok"], p["f"], -1.0)
    return pieces


def _route_head(s1, s2, exact_ties):
    tq = s1.shape[1]
    v1, rank1 = yield from _top16(s1, exact_ties, want_rank=exact_ties)
    v2, rank2 = yield from _top16(s2, exact_ties)
    e1 = jnp.exp(v1 - v1[0:1])
    e2 = jnp.exp(v2 - v2[0:1])
    pieces = _pair_pieces(v1, v2, e1, e2)

    taken = [jnp.zeros((8, tq), F32) for _ in pieces]
    cs = [p["c"] for p in pieces]
    for _ in range(PEER_TOPK):
        m = cs[0]
        for c in cs[1:]:
            m = jnp.maximum(m, c)
        m = jnp.max(m, axis=0, keepdims=True)
        if exact_ties:
            fm = None
            for c, p in zip(cs, pieces):
                cand = jnp.where(c == m, p["f"], 1e9)
                fm = cand if fm is None else jnp.minimum(fm, cand)
            fm = jnp.min(fm, axis=0, keepdims=True)
        for i, p in enumerate(pieces):
            hit = (p["f"] == fm) if exact_ties else (cs[i] == m)
            taken[i] = jnp.where(hit, 1.0, taken[i])
            cs[i] = jnp.where(hit, -jnp.inf, cs[i])
        yield

    row16 = lax.broadcasted_iota(jnp.int32, (PEER_TOPK, tq), 0)
    n1 = jnp.zeros((PEER_TOPK, tq), F32)
    z = jnp.zeros((1, tq), F32)
    for tf, p in zip(taken, pieces):
        z = z + jnp.sum(tf * p["e"], axis=0, keepdims=True)
        if p["row_a"] is None:
            pad = jnp.zeros((8, tq), F32)
            n1 = n1 + (jnp.concatenate([tf, pad], axis=0) if p["a0"] == 0
                       else jnp.concatenate([pad, tf], axis=0))
        else:
            n1 = n1 + jnp.where(row16 == p["row_a"], jnp.sum(tf, axis=0, keepdims=True), 0.0)

    in2 = rank2 < float(PEER_TOPK)
    cnt1 = jnp.zeros(s1.shape, F32)
    if exact_ties:
        in1 = rank1 < float(PEER_TOPK)
        for r in range(PEER_TOPK):
            cnt1 = jnp.where(rank1 == float(r), n1[r:r + 1], cnt1)
    else:
        in1 = s1 >= v1[PEER_TOPK - 1:PEER_TOPK]
        for r in range(PEER_TOPK):
            cnt1 = jnp.where(s1 == v1[r:r + 1], n1[r:r + 1], cnt1)
    p1 = jnp.where(in1, jnp.exp(s1 - v1[0:1]) / z, 0.0)
    e2_dense = jnp.where(in2, jnp.exp(s2 - v2[0:1]), 0.0)
    winners = (jnp.sum(jnp.where(in1, 1.0, 0.0), axis=0, keepdims=True)
               + jnp.sum(jnp.where(in2, 1.0, 0.0), axis=0, keepdims=True)
               + jnp.sum(n1, axis=0, keepdims=True))
    return cnt1, p1, rank2, e2_dense, winners


def _route_unit(qh, sk, tabs, exact_ties):
    half = PEER_QDIM // 2
    s1 = lax.dot_general(sk[0], qh[:, :half], NT_DIMS, preferred_element_type=F32)
    s2 = lax.dot_general(sk[1], qh[:, half:], NT_DIMS, preferred_element_type=F32)
    cnt1, p1, rank2, e2_dense, winners = yield from _route_head(s1, s2, exact_ties)
    cnt_ref, p1_ref, rk2_ref, e2_ref = tabs
    cnt_ref[...] = cnt1
    p1_ref[...] = p1
    rk2_ref[...] = rank2.astype(BF16)
    e2_ref[...] = e2_dense.astype(BF16)
    return winners


class _RouteUnits:
    def __init__(self, qh_ref, sk_ref, tabs, n):
        self.args = [(qh_ref, sk_ref, [t.at[u] for t in tabs], u) for u in range(n)]
        self.steps = [_route_unit(q[u], k[u], t, False) for q, k, t, u in self.args]
        self.winners = [None] * n

    def advance(self, k):
        for u, steps in enumerate(self.steps):
            if self.winners[u] is None:
                try:
                    for _ in range(k):
                        next(steps)
                except StopIteration as done:
                    self.winners[u] = done.value

    def finish(self):
        while any(w is None for w in self.winners):
            self.advance(1)
        for (q, k, t, u), w in zip(self.args, self.winners):
            @pl.when(jnp.max(w) > float(3 * PEER_TOPK))
            def _(q=q, k=k, t=t, u=u):
                _drain(_route_unit(q[u], k[u], t, True))


def _peer_query_kernel(h_ref, g_ref, wpq_ref, xnt_ref, qh_ref):
    hn = _rms(h_ref[...], g_ref[...])
    xnt_ref[...] = hn.T.astype(BF16)
    q = jnp.dot(hn.astype(BF16), wpq_ref[...], preferred_element_type=F32)
    for hh in range(PEER_HEADS):
        qh_ref[hh] = q[:, hh * PEER_QDIM:(hh + 1) * PEER_QDIM].astype(BF16)


def peer_query(h, g, wpq_bf):
    t, d = h.shape
    tq = QUERY_TQ
    return pl.pallas_call(
        _peer_query_kernel,
        grid=(t // tq,),
        in_specs=[pl.BlockSpec((tq, d), lambda i: (i, 0)),
                  pl.BlockSpec((1, d), lambda i: (0, 0)),
                  pl.BlockSpec(wpq_bf.shape, lambda i: (0, 0), pipeline_mode=pl.Buffered(1))],
        out_specs=[pl.BlockSpec((d, tq), lambda i: (0, i)),
                   pl.BlockSpec((PEER_HEADS, tq, PEER_QDIM), lambda i: (0, i, 0))],
        out_shape=[jax.ShapeDtypeStruct((d, t), BF16), jax.ShapeDtypeStruct((PEER_HEADS, t, PEER_QDIM), BF16)],
        compiler_params=_cparams(1, 48),
        name="peer_query",
    )(h, g.reshape(1, d), wpq_bf)


def _peer_route_kernel(qh_ref, sk_ref, cnt_ref, p1_ref, rk2_ref, e2_ref, *, n_route):
    _RouteUnits(qh_ref, sk_ref, (cnt_ref, p1_ref, rk2_ref, e2_ref), n_route).finish()


def peer_route(qh, sk_bf, n_route=2):
    t = qh.shape[1]
    hps = PEER_HEADS // n_route
    tab = pl.BlockSpec((n_route, PEER_NKEYS, ROUTE_TQ), lambda i, j: (j, 0, i))
    return pl.pallas_call(
        functools.partial(_peer_route_kernel, n_route=n_route),
        grid=(t // ROUTE_TQ, hps),
        in_specs=[pl.BlockSpec((n_route, ROUTE_TQ, PEER_QDIM), lambda i, j: (j, i, 0)),
                  pl.BlockSpec((n_route, 2, PEER_NKEYS, PEER_QDIM // 2), lambda i, j: (j, 0, 0, 0))],
        out_specs=[tab] * 4,
        out_shape=[jax.ShapeDtypeStruct((PEER_HEADS, PEER_NKEYS, t), dt) for dt in (F32, F32, BF16, BF16)],
        compiler_params=_cparams(2, 32),
        name="peer_route",
    )(qh, sk_bf)


def _peer_dense_kernel(xnt_ref, u_ref, vt_ref, cnt_ref, p1_ref, rk2_ref, e2_ref, h_ref, fg_ref, y_ref, acc_ref,
                       *, te, n_e):
    j = pl.program_id(1)

    @pl.when(j == 0)
    def _():
        acc_ref[...] = jnp.zeros(acc_ref.shape, F32)

    tq = xnt_ref.shape[1]
    groups = te // PEER_NKEYS
    gpc = PEER_ROW_CHUNK // PEER_NKEYS
    parts = []
    for a in range(groups):
        if a % gpc == 0:
            rows = slice(a * PEER_NKEYS, (a + gpc) * PEER_NKEYS)
            act = jax.nn.gelu(jnp.dot(u_ref[rows, :], xnt_ref[...], preferred_element_type=F32).astype(BF16))
        i1 = j * groups + a
        gate = jnp.zeros((PEER_NKEYS, tq), BF16)
        for hh in range(PEER_HEADS):
            rep = lambda ref: jnp.concatenate(
                [jnp.broadcast_to(ref[hh, pl.ds(i1, 1), :], (16, tq)).astype(BF16)] * (PEER_NKEYS // 16), axis=0)
            gate = gate + jnp.where(rk2_ref[hh] < rep(cnt_ref), rep(p1_ref) * e2_ref[hh], jnp.zeros((), BF16))
        lo = (a % gpc) * PEER_NKEYS
        parts.append(act[lo:lo + PEER_NKEYS] * gate)
    hm_t = parts[0] if groups == 1 else jnp.concatenate(parts, axis=0)
    acc_ref[...] += jnp.dot(vt_ref[...], hm_t, preferred_element_type=F32)

    @pl.when(j == n_e - 1)
    def _():
        y_ref[...] = _rms(h_ref[...] + acc_ref[...].T, fg_ref[...])


def peer_dense(xnt, u_bf, vt_bf, tabs, h, final_g, tq, te):
    t, d = h.shape
    n_e = u_bf.shape[0] // te
    once = dict(pipeline_mode=pl.Buffered(1))
    tab = pl.BlockSpec((PEER_HEADS, PEER_NKEYS, tq), lambda i, j: (0, 0, i), **once)
    return pl.pallas_call(
        functools.partial(_peer_dense_kernel, te=te, n_e=n_e),
        grid=(t // tq, n_e),
        in_specs=[pl.BlockSpec((d, tq), lambda i, j: (0, i), **once),
                  pl.BlockSpec((te, d), lambda i, j: (j, 0)),
                  pl.BlockSpec((d, te), lambda i, j: (0, j)),
                  tab, tab, tab, tab,
                  pl.BlockSpec((tq, d), lambda i, j: (i, 0), **once),
                  pl.BlockSpec((1, d), lambda i, j: (0, 0))],
        out_specs=pl.BlockSpec((tq, d), lambda i, j: (i, 0)),
        out_shape=jax.ShapeDtypeStruct((t, d), F32),
        scratch_shapes=[pltpu.VMEM((d, tq), F32)],
        compiler_params=_cparams(2, 56),
        name="peer_dense",
    )(xnt, u_bf, vt_bf, *tabs, h, final_g.reshape(1, d))


def _cast_transpose_kernel(x_ref, o_ref):
    o_ref[...] = x_ref[...].T.astype(o_ref.dtype)


def cast_transpose(w, rows):
    n, d = w.shape
    return pl.pallas_call(
        _cast_transpose_kernel,
        grid=(n // rows,),
        in_specs=[pl.BlockSpec((rows, d), lambda i: (i, 0))],
        out_specs=pl.BlockSpec((d, rows), lambda i: (0, i)),
        out_shape=jax.ShapeDtypeStruct((d, n), BF16),
        compiler_params=_cparams(1, 48),
        name="cast_transpose",
    )(w)


def kernel(x_prompt, x_sample, mem_prompt, cache_k, cache_v, cache_conv, cache_mem_k, cache_mem_v, page_table,
           norm1_g, w_in, rel_bias, conv_w, conv_b, conv_ln_g, conv_ln_b, w_out, norm2_g, mem_norm_g, w_cq,
           w_mk, w_mv, w_co, norm3_g, w_pq, peer_sub_keys, peer_u, peer_v, final_g):
    depth = w_in.shape[0]
    assert depth == 1, "single-layer step"
    n_p, seq, d = x_prompt.shape
    n_s, t_s, _ = x_sample.shape
    l = 0
    bf = lambda a: a.astype(BF16)

    w_in_bf, w_out_bf = bf(w_in[l]), bf(w_out[l])
    w_cq_bf, w_co_bf, w_pq_bf = bf(w_cq[l]), bf(w_co[l]), bf(w_pq[l])
    w_mem_bf = bf(jnp.concatenate([w_mk[l], w_mv[l]], axis=1))
    sk_bf, u_bf, vt_bf = bf(peer_sub_keys[l]), bf(peer_u[l]), cast_transpose(peer_v[l], PEER_EXPERT_TILE)

    n_pool, n_pages = cache_k.shape[1], page_table.shape[1]
    bias_p, t_far, t_last, t_own = bias_tiles(rel_bias, n_pages * PAGE_SIZE)
    c31_h = rel_bias[NUM_BUCKETS - 1]

    def cross_and_query(x, attn, conv, mk, mv, n_grp, rows, tiles_per_seq):
        h = out_proj(x, attn, conv, w_out_bf, tm=OUT_PROJ_ROWS)
        h = cross_block(h, norm2_g[l], w_cq_bf, w_co_bf, mk, mv, n_grp, rows, tiles_per_seq)
        return (h,) + tuple(peer_query(h, norm3_g[l], w_pq_bf))

    experts = lambda xnt, tabs, h: peer_dense(xnt, u_bf, vt_bf, tabs, h, final_g,
                                              tq=PEER_TOKEN_TILE, te=PEER_EXPERT_TILE)

    xp = x_prompt.reshape(n_p * seq, d)
    mk_p, mv_p = rms_matmul(mem_prompt.reshape(n_p * N_MEM, d), mem_norm_g[l], w_mem_bf, 2, tm=MEM_PROJ_ROWS)
    q_p, k_p, v_p, ga_p, gg_p = rms_matmul(xp, norm1_g[l], w_in_bf, 5, tm=IN_PROJ_ROWS)
    attn_p = moba_prompt(q_p, k_p, v_p, bias_p, c31_h, n_p, seq)
    zero_buf = jnp.zeros((n_p, CONV_K - 1, CONV_CH), F32)
    conv_p, buf_p = conformer_conv(ga_p, gg_p, zero_buf, conv_w[l], conv_b[l], conv_ln_g[l], conv_ln_b[l],
                                   n_p, seq, tt=CONV_ROWS, sb=1)
    h_p, xnt_p, qh_p = cross_and_query(xp, attn_p, conv_p, mk_p.reshape(n_p, N_MEM, X_WIDTH),
                                       mv_p.reshape(n_p, N_MEM, X_WIDTH), 1, CROSS_ROWS, seq // CROSS_ROWS)

    xs = x_sample.reshape(n_s * t_s, d)
    q_s, k_s, v_s, ga_s, gg_s = rms_matmul(xs, norm1_g[l], w_in_bf, 5, tm=IN_PROJ_ROWS)
    tok3 = lambda a: a.reshape(n_s, t_s, ATTN_WIDTH)
    units_p = PEER_HEADS * (n_p * seq // ROUTE_TQ)
    fuse = units_p % n_s == 0 and PEER_HEADS % (units_p // n_s) == 0
    attn_s, *tabs_p = moba_sample(tok3(q_s), tok3(k_s), tok3(v_s),
                                  cache_k[l].reshape(n_pool, PAGE_SIZE * N_HEADS, HEAD_DIM),
                                  cache_v[l].reshape(n_pool, PAGE_SIZE * N_HEADS, HEAD_DIM),
                                  page_table, t_far, t_last, t_own, route=(qh_p, sk_bf) if fuse else None)
    if not fuse:
        tabs_p = peer_route(qh_p, sk_bf)
    y_p = experts(xnt_p, tabs_p, h_p)
    conv_s, buf_s = conformer_conv(ga_s, gg_s, cache_conv[l], conv_w[l], conv_b[l], conv_ln_g[l], conv_ln_b[l],
                                   n_s, t_s, tt=t_s, sb=SHORT_SEQS_PER_STEP)
    h_s, xnt_s, qh_s = cross_and_query(xs, attn_s.reshape(n_s * t_s, ATTN_WIDTH), conv_s,
                                       cache_mem_k[l].reshape(n_s, N_MEM * X_HEADS, HEAD_DIM),
                                       cache_mem_v[l].reshape(n_s, N_MEM * X_HEADS, HEAD_DIM),
                                       SHORT_SEQS_PER_STEP, t_s, 1)
    y_s = experts(xnt_s, peer_route(qh_s, sk_bf), h_s)

    n_pg = seq // PAGE_SIZE
    return (y_p.reshape(n_p, seq, d), y_s.reshape(n_s, t_s, d),
            k_p.reshape(1, n_p, n_pg, PAGE_SIZE, N_HEADS, HEAD_DIM),
            v_p.reshape(1, n_p, n_pg, PAGE_SIZE, N_HEADS, HEAD_DIM),
            buf_p[None],
            mk_p.reshape(1, n_p, N_MEM, X_HEADS, HEAD_DIM), mv_p.reshape(1, n_p, N_MEM, X_HEADS, HEAD_DIM),
            k_s.reshape(1, n_s, t_s, N_HEADS, HEAD_DIM), v_s.reshape(1, n_s, t_s, N_HEADS, HEAD_DIM),
            buf_s[None])
```

```python
import functools
import math

import numpy as np
import jax
import jax.numpy as jnp
from jax import lax
from jax.experimental import pallas as pl
from jax.experimental.pallas import tpu as pltpu

F32 = jnp.float32
BF16 = jnp.bfloat16

D_MODEL = 2048
HEAD_DIM = 128
N_HEADS = 8
ATTN_WIDTH = N_HEADS * HEAD_DIM
CONV_CH = 1024
MOBA_BLOCK = 256
MOBA_TOPK = 3
PAGE_SIZE = 128
NUM_BUCKETS = 32
MAX_EXACT = 16
REL_MAX_DIST = 128
CONV_K = 31
N_MEM = 256
X_HEADS = 4
X_WIDTH = X_HEADS * HEAD_DIM
PEER_HEADS = 8
PEER_NKEYS = 128
PEER_TOPK = 16
PEER_QDIM = 256
PEER_ROW_CHUNK = 512
ROUTE_TQ = 256
QUERY_TQ = 512
ROUTE_STEPS_PER_PAGE = 2
EPS = 1e-6
NEG = -1e30

IN_PROJ_ROWS = 512
MEM_PROJ_ROWS = 512
OUT_PROJ_ROWS = 512
CONV_ROWS = 256
SHORT_SEQS_PER_STEP = 8
CROSS_ROWS = 256
PEER_TOKEN_TILE = 512
PEER_EXPERT_TILE = 1024

MIB = 1024 * 1024
NT_DIMS = (((1,), (1,)), ((), ()))


def _cparams(n_grid, vmem_mib):
    return pltpu.CompilerParams(dimension_semantics=("arbitrary",) * n_grid,
                                vmem_limit_bytes=vmem_mib * MIB)


def _rms(x, g):
    return x * lax.rsqrt(jnp.mean(x * x, axis=-1, keepdims=True) + EPS) * g


def _rms_matmul_kernel(x_ref, g_ref, w_ref, *outs):
    xn = _rms(x_ref[...], g_ref[...]).astype(BF16)
    tn = outs[0].shape[1]
    for s, o_ref in enumerate(outs):
        o_ref[...] = jnp.dot(xn, w_ref[:, s * tn:(s + 1) * tn], preferred_element_type=F32)


def rms_matmul(x, g, w_bf, n_out, tm):
    t, d = x.shape
    tn = w_bf.shape[1] // n_out
    return pl.pallas_call(
        _rms_matmul_kernel,
        grid=(t // tm,),
        in_specs=[pl.BlockSpec((tm, d), lambda i: (i, 0)),
                  pl.BlockSpec((1, d), lambda i: (0, 0)),
                  pl.BlockSpec(w_bf.shape, lambda i: (0, 0), pipeline_mode=pl.Buffered(1))],
        out_specs=[pl.BlockSpec((tm, tn), lambda i: (i, 0))] * n_out,
        out_shape=[jax.ShapeDtypeStruct((t, tn), F32)] * n_out,
        compiler_params=_cparams(1, 56),
        name="rms_matmul",
    )(x, g.reshape(1, d), w_bf)


def _bucket_np(rel):
    n = np.maximum(rel, 0)
    nf = np.maximum(n, 1).astype(np.float32)
    large = MAX_EXACT + (np.log(nf / MAX_EXACT) / np.float32(math.log(REL_MAX_DIST / MAX_EXACT))
                         * (NUM_BUCKETS - MAX_EXACT)).astype(np.int32)
    large = np.minimum(large, NUM_BUCKETS - 1)
    return np.where(n < MAX_EXACT, n, large).astype(np.int32)


def _bias_kernel(rb_ref, rbx_ref, bkp_ref, bkf_ref, bkl_ref, bko_ref, tp_ref, tf_ref, tl_ref, to_ref):
    def lookup(bk, table):
        acc = jnp.full(bk.shape, NEG, F32)
        for b in range(NUM_BUCKETS):
            acc = jnp.where(bk == b, table(b), acc)
        return acc

    for h in range(N_HEADS):
        for t in range(2):
            tp_ref[h, t] = lookup(bkp_ref[t], lambda b: rb_ref[b, h])
    by_row = lambda b: rbx_ref[b]
    tf_ref[...] = lookup(bkf_ref[...], by_row)
    tl_ref[...] = lookup(bkl_ref[...], by_row)
    to_ref[...] = lookup(bko_ref[...], by_row)


def bias_tiles(rel_bias, n_past):
    key = np.arange(MOBA_BLOCK)[:, None]
    qry = np.arange(MOBA_BLOCK)[None, :]
    bkp = np.stack([_bucket_np(qry - key), _bucket_np(MOBA_BLOCK + qry - key)])
    r = np.arange(N_HEADS * 8)[:, None]
    rh, rq = r // 8, r % 8
    c = np.arange(PAGE_SIZE * N_HEADS)[None, :]
    ct, ch = c // N_HEADS, c % N_HEADS
    bkf = np.where(rh == ch, NUM_BUCKETS - 1, -1)
    bkl = np.where(rh == ch, _bucket_np(n_past + rq - (n_past - PAGE_SIZE + ct)), -1)
    assert PAGE_SIZE >= REL_MAX_DIST
    co = np.arange(128)[None, :]
    coh, cot = co // 8, co % 8
    bko = np.where((rh == coh) & (cot <= rq), _bucket_np(rq - cot), -1)
    rbx = jnp.repeat(rel_bias, 8, axis=1)[:, :, None]
    vm = pl.BlockSpec(memory_space=pltpu.VMEM)
    i32 = lambda a: jnp.asarray(a.astype(np.int32))
    return pl.pallas_call(
        _bias_kernel,
        in_specs=[pl.BlockSpec(memory_space=pltpu.SMEM), vm, vm, vm, vm, vm],
        out_specs=[vm, vm, vm, vm],
        out_shape=[jax.ShapeDtypeStruct((N_HEADS, 2, MOBA_BLOCK, MOBA_BLOCK), F32),
                   jax.ShapeDtypeStruct(bkf.shape, F32),
                   jax.ShapeDtypeStruct(bkl.shape, F32),
                   jax.ShapeDtypeStruct(bko.shape, F32)],
        compiler_params=pltpu.CompilerParams(vmem_limit_bytes=32 * MIB),
        name="bias_tiles",
    )(rel_bias, rbx, i32(bkp), i32(bkf), i32(bkl), i32(bko))


def _split_bf16(x):
    hi = x.astype(BF16)
    lo = (x - hi.astype(F32)).astype(BF16)
    return hi, lo


def _moba_prompt_kernel(q_ref, k_ref, v_ref, bias_ref, c31_ref, o_ref, qbf, kbf, vt, pen_ref, s_ref):
    nb = kbf.shape[0] // MOBA_BLOCK
    scale = HEAD_DIM ** -0.5
    blk = lambda b: slice(b * MOBA_BLOCK, (b + 1) * MOBA_BLOCK)

    q = q_ref[...]
    q_hi, q_lo = _split_bf16(q)
    qbf[...] = q_hi
    kbf[...] = k_ref[...].astype(BF16)
    km = jnp.concatenate([jnp.mean(k_ref[blk(b), :], axis=0, keepdims=True) for b in range(nb)]
                         + [jnp.zeros((16 - nb, HEAD_DIM), F32)], axis=0)
    for b in range(nb):
        vt[b] = v_ref[blk(b), :].T.astype(BF16)
    km_hi, km_lo = _split_bf16(km)
    gate = (lax.dot_general(km_hi, q_hi, NT_DIMS, preferred_element_type=F32)
            + lax.dot_general(km_lo, q_hi, NT_DIMS, preferred_element_type=F32)
            + lax.dot_general(km_hi, q_lo, NT_DIMS, preferred_element_type=F32))

    row = lax.broadcasted_iota(jnp.int32, gate.shape, 0)
    own = lax.broadcasted_iota(jnp.int32, gate.shape, 1) // MOBA_BLOCK
    rank = jnp.zeros(gate.shape, F32)
    for b2 in range(nb):
        gb = gate[b2:b2 + 1, :]
        beats = ((gb > gate) | ((gb == gate) & (b2 < row))) & (b2 < own)
        rank = rank + jnp.where(beats, 1.0, 0.0)
    pen_ref[...] = jnp.where((row < own) & (rank < float(MOBA_TOPK)), 0.0, NEG)

    key = lax.broadcasted_iota(jnp.int32, (MOBA_BLOCK, MOBA_BLOCK), 0)
    qry = lax.broadcasted_iota(jnp.int32, (MOBA_BLOCK, MOBA_BLOCK), 1)
    c31 = c31_ref[pl.program_id(1)]
    for qi in range(nb):
        m = None
        for kb in range(qi + 1):
            s = lax.dot_general(kbf[blk(kb), :], qbf[blk(qi), :], NT_DIMS, preferred_element_type=F32) * scale
            if kb == qi:
                s = jnp.where(key <= qry, s + bias_ref[0, 0], NEG)
            else:
                s = s + (bias_ref[0, 1] if kb == qi - 1 else c31) + pen_ref[kb:kb + 1, blk(qi)]
            s_ref[kb] = s
            cm = jnp.max(s, axis=0, keepdims=True)
            m = cm if m is None else jnp.maximum(m, cm)
        lsum = jnp.zeros((1, MOBA_BLOCK), F32)
        acc = jnp.zeros((HEAD_DIM, MOBA_BLOCK), F32)
        for kb in range(qi + 1):
            p = jnp.exp(s_ref[kb] - m)
            lsum = lsum + jnp.sum(p, axis=0, keepdims=True)
            acc = acc + jnp.dot(vt[kb], p.astype(BF16), preferred_element_type=F32)
        o_ref[blk(qi), :] = (acc / lsum).T.astype(o_ref.dtype)


def moba_prompt(q, k, v, bias_p, c31_h, n_seq, seq):
    nq = seq // MOBA_BLOCK
    tok = pl.BlockSpec((seq, HEAD_DIM), lambda n, h: (n, h))
    return pl.pallas_call(
        _moba_prompt_kernel,
        grid=(n_seq, N_HEADS),
        in_specs=[tok, tok, tok,
                  pl.BlockSpec((1, 2, MOBA_BLOCK, MOBA_BLOCK), lambda n, h: (h, 0, 0, 0)),
                  pl.BlockSpec(memory_space=pltpu.SMEM)],
        out_specs=tok,
        out_shape=jax.ShapeDtypeStruct(q.shape, BF16),
        scratch_shapes=[pltpu.VMEM((seq, HEAD_DIM), BF16), pltpu.VMEM((seq, HEAD_DIM), BF16),
                        pltpu.VMEM((nq, HEAD_DIM, MOBA_BLOCK), BF16), pltpu.VMEM((16, seq), F32),
                        pltpu.VMEM((nq, MOBA_BLOCK, MOBA_BLOCK), F32)],
        compiler_params=_cparams(2, 48),
        name="moba_prompt",
    )(q, k, v, bias_p, c31_h)


def _by_head_rows(x):
    return jnp.concatenate([x[:, h * HEAD_DIM:(h + 1) * HEAD_DIM] for h in range(x.shape[1] // HEAD_DIM)], axis=0)


def _by_head_lanes(x, n_heads):
    t = x.shape[0] // n_heads
    return jnp.concatenate([x[h * t:(h + 1) * t] for h in range(n_heads)], axis=1)


def _moba_sample_kernel(pt_ref, q_ref, kn_ref, vn_ref, tf_ref, tl_ref, to_ref, *refs, n_pages, n_route):
    kp, vp = refs[:n_pages], refs[n_pages:2 * n_pages]
    refs = refs[2 * n_pages:]
    if n_route:
        qh_ref, sk_ref, o_ref, cnt_ref, p1_ref, rk2_ref, e2_ref, s_ref = refs
    else:
        o_ref, s_ref = refs
    ppb = MOBA_BLOCK // PAGE_SIZE
    nb = n_pages // ppb
    scale = HEAD_DIM ** -0.5
    t_new = q_ref.shape[1]
    nr = N_HEADS * t_new

    route = _RouteUnits(qh_ref, sk_ref, (cnt_ref, p1_ref, rk2_ref, e2_ref), n_route) if n_route else None
    interleave = (lambda: route.advance(ROUTE_STEPS_PER_PAGE)) if n_route else (lambda: None)

    q = _by_head_rows(q_ref[0])
    q_bf = (q * scale).astype(BF16)
    zpad = jnp.zeros((128 - nr, HEAD_DIM), F32)
    kn = jnp.concatenate([_by_head_rows(kn_ref[0]), zpad], axis=0).astype(BF16)
    vn = jnp.concatenate([_by_head_rows(vn_ref[0]), zpad], axis=0).astype(BF16)

    s_own = lax.dot_general(q_bf, kn, NT_DIMS, preferred_element_type=F32) + to_ref[...]
    blk_max, gates = [], []
    for b in range(nb):
        bm = None
        ksum = jnp.zeros((N_HEADS, HEAD_DIM), F32)
        for pg in range(b * ppb, (b + 1) * ppb):
            kpage = kp[pg][...]
            ksum = ksum + jnp.sum(kpage.reshape(PAGE_SIZE, N_HEADS, HEAD_DIM), axis=0)
            s = lax.dot_general(q_bf, kpage.astype(BF16), NT_DIMS, preferred_element_type=F32)
            s = s + (tl_ref[...] if pg == n_pages - 1 else tf_ref[...])
            s_ref[pg] = s
            pm = jnp.max(s, axis=-1, keepdims=True)
            bm = pm if bm is None else jnp.maximum(bm, pm)
            interleave()
        blk_max.append(bm)
        kmean = ksum / float(MOBA_BLOCK)
        krep = jnp.concatenate([jnp.broadcast_to(kmean[h:h + 1, :], (t_new, HEAD_DIM)) for h in range(N_HEADS)],
                               axis=0)
        gates.append(jnp.sum(q * krep, axis=1, keepdims=True))

    penalty = []
    for b in range(nb):
        rank = jnp.zeros((nr, 1), F32)
        for b2 in range(nb):
            if b2 != b:
                beats = (gates[b2] > gates[b]) | ((gates[b2] == gates[b]) & (b2 < b))
                rank = rank + jnp.where(beats, 1.0, 0.0)
        penalty.append(jnp.where(rank < float(MOBA_TOPK), 0.0, NEG))

    m = jnp.max(s_own, axis=-1, keepdims=True)
    for b in range(nb):
        m = jnp.maximum(m, blk_max[b] + penalty[b])

    p = jnp.exp(s_own - m)
    lsum = jnp.sum(p, axis=-1, keepdims=True)
    acc = jnp.dot(p.astype(BF16), vn, preferred_element_type=F32)
    for pg in range(n_pages):
        p = jnp.exp(s_ref[pg] + (penalty[pg // ppb] - m))
        lsum = lsum + jnp.sum(p, axis=-1, keepdims=True)
        acc = acc + jnp.dot(p.astype(BF16), vp[pg][...].astype(BF16), preferred_element_type=F32)
        interleave()
    o_ref[0] = _by_head_lanes(acc / lsum, N_HEADS)

    if n_route:
        route.finish()


def moba_sample(q, kn, vn, ck, cv, page_table, t_far, t_last, t_own, route=None):
    nseq, n_pages = page_table.shape
    tok = pl.BlockSpec((1,) + q.shape[1:], lambda b, pt: (b, 0, 0))

    def page_spec(p):
        return pl.BlockSpec((None,) + ck.shape[1:], lambda b, pt, p=p: (pt[b, p], 0, 0))

    const = lambda a: pl.BlockSpec(a.shape, lambda b, pt: (0, 0))
    in_specs = ([tok, tok, tok, const(t_far), const(t_last), const(t_own)]
                + [page_spec(p) for p in range(n_pages)] * 2)
    out_specs, out_shape, extra, n_route = [tok], [jax.ShapeDtypeStruct(q.shape, F32)], [], 0
    if route is not None:
        qh, sk_bf = route
        t = qh.shape[1]
        units = PEER_HEADS * (t // ROUTE_TQ)
        n_route = units // nseq
        assert n_route * nseq == units and PEER_HEADS % n_route == 0
        hps = PEER_HEADS // n_route
        in_specs += [pl.BlockSpec((n_route, ROUTE_TQ, PEER_QDIM), lambda b, pt: (b % hps, b // hps, 0)),
                     pl.BlockSpec((n_route, 2, PEER_NKEYS, PEER_QDIM // 2), lambda b, pt: (b % hps, 0, 0, 0))]
        tab = pl.BlockSpec((n_route, PEER_NKEYS, ROUTE_TQ), lambda b, pt: (b % hps, 0, b // hps))
        out_specs += [tab] * 4
        out_shape += [jax.ShapeDtypeStruct((PEER_HEADS, PEER_NKEYS, t), dt) for dt in (F32, F32, BF16, BF16)]
        extra = [qh, sk_bf]
    grid_spec = pltpu.PrefetchScalarGridSpec(
        num_scalar_prefetch=1, grid=(nseq,), in_specs=in_specs, out_specs=out_specs,
        scratch_shapes=[pltpu.VMEM((n_pages, N_HEADS * q.shape[1], ck.shape[1]), F32)])
    return pl.pallas_call(
        functools.partial(_moba_sample_kernel, n_pages=n_pages, n_route=n_route),
        grid_spec=grid_spec,
        out_shape=out_shape,
        compiler_params=_cparams(1, 56),
        name="moba_sample",
    )(page_table, q, kn, vn, t_far, t_last, t_own, *([ck] * n_pages), *([cv] * n_pages), *extra)


HIST_ROWS = 32
HIST_OFF = HIST_ROWS - (CONV_K - 1)


def _conv_kernel(ga_ref, gg_ref, hist_ref, w_ref, b_ref, lg_ref, lb_ref, o_ref, nb_ref, ext_ref, y_ref,
                 *, tt, n_t, sb):
    t = pl.program_id(1)

    @pl.when(t == 0)
    def _():
        for s in range(sb):
            ext_ref[s, HIST_OFF:HIST_ROWS, :] = hist_ref[s]

    rt = min(tt, 128)
    for s in range(sb):
        rows = slice(s * tt, (s + 1) * tt)
        ext_ref[s, HIST_ROWS:HIST_ROWS + tt, :] = ga_ref[rows, :] * jax.nn.sigmoid(gg_ref[rows, :])
        for c in range(CONV_CH // 128):
            cs = slice(c * 128, (c + 1) * 128)
            for r0 in range(0, tt, rt):
                acc = jnp.zeros((rt, 128), F32)
                for res in range(8):
                    taps = [j for j in range(CONV_K) if (HIST_OFF + j) % 8 == res]
                    q0 = (HIST_OFF + taps[0]) // 8
                    q1 = (HIST_OFF + taps[-1]) // 8
                    if res:
                        win = ext_ref[s, r0 + 8 * q0:r0 + 8 * (q1 + 1) + rt, cs]
                        win = pltpu.roll(win, win.shape[0] - res, axis=0)
                    else:
                        win = ext_ref[s, r0 + 8 * q0:r0 + 8 * q1 + rt, cs]
                    for j in taps:
                        off = 8 * ((HIST_OFF + j) // 8 - q0)
                        acc = acc + w_ref[j:j + 1, cs] * win[off:off + rt]
                y_ref[s * tt + r0:s * tt + r0 + rt, cs] = acc + b_ref[:, cs]
    y = y_ref[...]
    mu = jnp.mean(y, axis=-1, keepdims=True)
    yc = y - mu
    var = jnp.mean(yc * yc, axis=-1, keepdims=True)
    yn = yc * lax.rsqrt(var + EPS) * lg_ref[...] + lb_ref[...]
    o_ref[...] = (yn * jax.nn.sigmoid(yn)).astype(o_ref.dtype)

    @pl.when(t == n_t - 1)
    def _():
        for s in range(sb):
            nb_ref[s] = ext_ref[s, tt + HIST_OFF:tt + HIST_ROWS, :]

    if n_t > 1:
        for s in range(sb):
            ext_ref[s, 0:HIST_ROWS, :] = ext_ref[s, tt:tt + HIST_ROWS, :]


def conformer_conv(ga, gg, hist, w_dw, b_dw, ln_g, ln_b, n_seq, seq, tt, sb):
    n_t = seq // tt
    assert sb == 1 or n_t == 1
    row = lambda a: a.reshape(1, CONV_CH)
    cvec = pl.BlockSpec((1, CONV_CH), lambda n, t: (0, 0))
    tile = pl.BlockSpec((sb * tt, CONV_CH), lambda n, t: (n * n_t + t, 0))
    hist_spec = pl.BlockSpec((sb, CONV_K - 1, CONV_CH), lambda n, t: (n, 0, 0))
    return pl.pallas_call(
        functools.partial(_conv_kernel, tt=tt, n_t=n_t, sb=sb),
        grid=(n_seq // sb, n_t),
        in_specs=[tile, tile, hist_spec, pl.BlockSpec((CONV_K, CONV_CH), lambda n, t: (0, 0)),
                  cvec, cvec, cvec],
        out_specs=[tile, hist_spec],
        out_shape=[jax.ShapeDtypeStruct((n_seq * seq, CONV_CH), BF16),
                   jax.ShapeDtypeStruct((n_seq, CONV_K - 1, CONV_CH), F32)],
        scratch_shapes=[pltpu.VMEM((sb, HIST_ROWS + tt, CONV_CH), F32), pltpu.VMEM((sb * tt, CONV_CH), F32)],
        compiler_params=_cparams(2, 32),
        name="conformer_conv",
    )(ga, gg, hist, w_dw, row(b_dw), row(ln_g), row(ln_b))


def _out_proj_kernel(x_ref, a_ref, c_ref, w_ref, o_ref):
    wa = a_ref.shape[1]
    y = jnp.dot(a_ref[...].astype(BF16), w_ref[0:wa, :], preferred_element_type=F32)
    y = y + jnp.dot(c_ref[...].astype(BF16), w_ref[wa:, :], preferred_element_type=F32)
    o_ref[...] = x_ref[...] + y


def out_proj(x, a, c, w_bf, tm):
    t, d = x.shape
    return pl.pallas_call(
        _out_proj_kernel,
        grid=(t // tm,),
        in_specs=[pl.BlockSpec((tm, d), lambda i: (i, 0)),
                  pl.BlockSpec((tm, a.shape[1]), lambda i: (i, 0)),
                  pl.BlockSpec((tm, c.shape[1]), lambda i: (i, 0)),
                  pl.BlockSpec(w_bf.shape, lambda i: (0, 0))],
        out_specs=pl.BlockSpec((tm, d), lambda i: (i, 0)),
        out_shape=jax.ShapeDtypeStruct((t, d), F32),
        compiler_params=_cparams(1, 48),
        name="out_proj",
    )(x, a, c, w_bf)


def _softmax_pv(s, mv_bf):
    m = jnp.max(s, axis=-1, keepdims=True)
    p = jnp.exp(s - m)
    o = jnp.dot(p.astype(BF16), mv_bf, preferred_element_type=F32)
    return o / jnp.sum(p, axis=-1, keepdims=True)


def _cross_kernel(h_ref, g_ref, wq_ref, wo_ref, mk_ref, mv_ref, o_ref, *, n_grp, rows):
    scale = HEAD_DIM ** -0.5
    h = h_ref[...]
    hn = _rms(h, g_ref[...]).astype(BF16)
    q = jnp.dot(hn, wq_ref[...], preferred_element_type=F32)
    outs = []
    for g in range(n_grp):
        mk = mk_ref[g].astype(BF16)
        mv = mv_ref[g].astype(BF16)
        qg = q[g * rows:(g + 1) * rows]
        if rows >= 128:
            heads = []
            for hd in range(X_HEADS):
                cs = slice(hd * HEAD_DIM, (hd + 1) * HEAD_DIM)
                s = lax.dot_general(qg[:, cs].astype(BF16), mk[:, cs], NT_DIMS, preferred_element_type=F32)
                heads.append(_softmax_pv(s * scale, mv[:, cs]))
            outs.append(jnp.concatenate(heads, axis=1))
        else:
            qx = _by_head_rows(qg).astype(BF16)
            s = lax.dot_general(qx, mk, NT_DIMS, preferred_element_type=F32) * scale
            rowh = lax.broadcasted_iota(jnp.int32, s.shape, 0) // rows
            colh = lax.broadcasted_iota(jnp.int32, s.shape, 1) % X_HEADS
            o = _softmax_pv(jnp.where(rowh == colh, s, NEG), mv)
            outs.append(_by_head_lanes(o, X_HEADS))
    o_all = outs[0] if n_grp == 1 else jnp.concatenate(outs, axis=0)
    o_ref[...] = h + jnp.dot(o_all.astype(BF16), wo_ref[...], preferred_element_type=F32)


def cross_block(h, g, wq_bf, wo_bf, mk, mv, n_grp, rows, tiles_per_seq):
    t, d = h.shape
    tm = n_grp * rows
    if n_grp == 1:
        mem_map = lambda i: (i // tiles_per_seq, 0, 0)
    else:
        mem_map = lambda i: (i, 0, 0)
    mem_spec = pl.BlockSpec((n_grp,) + mk.shape[1:], mem_map)
    return pl.pallas_call(
        functools.partial(_cross_kernel, n_grp=n_grp, rows=rows),
        grid=(t // tm,),
        in_specs=[pl.BlockSpec((tm, d), lambda i: (i, 0)),
                  pl.BlockSpec((1, d), lambda i: (0, 0)),
                  pl.BlockSpec(wq_bf.shape, lambda i: (0, 0)),
                  pl.BlockSpec(wo_bf.shape, lambda i: (0, 0)),
                  mem_spec, mem_spec],
        out_specs=pl.BlockSpec((tm, d), lambda i: (i, 0)),
        out_shape=jax.ShapeDtypeStruct((t, d), F32),
        compiler_params=_cparams(1, 48),
        name="cross_block",
    )(h, g.reshape(1, d), wq_bf, wo_bf, mk, mv)


def _drain(steps):
    try:
        while True:
            next(steps)
    except StopIteration as done:
        return done.value


def _top16(s, exact_ties, want_rank=True):
    n, t = s.shape
    row = lax.broadcasted_iota(jnp.int32, (n, t), 0).astype(F32)
    row16 = lax.broadcasted_iota(jnp.int32, (PEER_TOPK, t), 0)
    rank = jnp.full((n, t), float(PEER_TOPK), F32)
    vals = jnp.zeros((PEER_TOPK, t), F32)
    for r in range(PEER_TOPK):
        m = jnp.max(s, axis=0, keepdims=True)
        if exact_ties:
            idx = jnp.min(jnp.where(s == m, row, float(n)), axis=0, keepdims=True)
            hit = row == idx
        else:
            hit = s == m
        if want_rank:
            rank = jnp.where(hit, float(r), rank)
        s = jnp.where(hit, -jnp.inf, s)
        vals = jnp.where(row16 == r, m, vals)
        yield
    return vals, rank


def _pair_pieces(v1, v2, e1, e2):
    t = v1.shape[1]
    sub = lax.broadcasted_iota(jnp.int32, (8, t), 0)
    subf = sub.astype(F32)
    pieces = []

    def col(b, a0, a_max):
        a = sub + a0
        pieces.append(dict(c=v1[a0:a0 + 8] + v2[b:b + 1], e=e1[a0:a0 + 8] * e2[b:b + 1],
                           f=(subf + a0) * 16.0 + b, ok=a <= a_max, a0=a0, row_a=None))

    def rowp(a, b0, b_min, b_max):
        b = sub + b0
        pieces.append(dict(c=v1[a:a + 1] + v2[b0:b0 + 8], e=e1[a:a + 1] * e2[b0:b0 + 8],
                           f=a * 16.0 + (subf + b0), ok=(b >= b_min) & (b <= b_max), a0=None, row_a=a))

    col(0, 0, 15), col(0, 8, 15), col(1, 0, 7), col(2, 0, 4), col(3, 0, 3)
    rowp(0, 8, 8, 15), rowp(0, 0, 4, 7), rowp(1, 0, 4, 7), rowp(2, 0, 4, 4)
    for p in pieces:
        p["c"] = jnp.where(p["ok"], p["c"], -jnp.inf)
        p["f"] = jnp.where(p["ok"], p["f"], -1.0)
    return pieces


def _route_head(s1, s2, exact_ties):
    tq = s1.shape[1]
    v1, rank1 = yield from _top16(s1, exact_ties, want_rank=exact_ties)
    v2, rank2 = yield from _top16(s2, exact_ties)
    e1 = jnp.exp(v1 - v1[0:1])
    e2 = jnp.exp(v2 - v2[0:1])
    pieces = _pair_pieces(v1, v2, e1, e2)

    taken = [jnp.zeros((8, tq), F32) for _ in pieces]
    cs = [p["c"] for p in pieces]
    for _ in range(PEER_TOPK):
        m = cs[0]
        for c in cs[1:]:
            m = jnp.maximum(m, c)
        m = jnp.max(m, axis=0, keepdims=True)
        if exact_ties:
            fm = None
            for c, p in zip(cs, pieces):
                cand = jnp.where(c == m, p["f"], 1e9)
                fm = cand if fm is None else jnp.minimum(fm, cand)
            fm = jnp.min(fm, axis=0, keepdims=True)
        for i, p in enumerate(pieces):
            hit = (p["f"] == fm) if exact_ties else (cs[i] == m)
            taken[i] = jnp.where(hit, 1.0, taken[i])
            cs[i] = jnp.where(hit, -jnp.inf, cs[i])
        yield

    row16 = lax.broadcasted_iota(jnp.int32, (PEER_TOPK, tq), 0)
    n1 = jnp.zeros((PEER_TOPK, tq), F32)
    z = jnp.zeros((1, tq), F32)
    for tf, p in zip(taken, pieces):
        z = z + jnp.sum(tf * p["e"], axis=0, keepdims=True)
        if p["row_a"] is None:
            pad = jnp.zeros((8, tq), F32)
            n1 = n1 + (jnp.concatenate([tf, pad], axis=0) if p["a0"] == 0
                       else jnp.concatenate([pad, tf], axis=0))
        else:
            n1 = n1 + jnp.where(row16 == p["row_a"], jnp.sum(tf, axis=0, keepdims=True), 0.0)

    in2 = rank2 < float(PEER_TOPK)
    cnt1 = jnp.zeros(s1.shape, F32)
    if exact_ties:
        in1 = rank1 < float(PEER_TOPK)
        for r in range(PEER_TOPK):
            cnt1 = jnp.where(rank1 == float(r), n1[r:r + 1], cnt1)
    else:
        in1 = s1 >= v1[PEER_TOPK - 1:PEER_TOPK]
        for r in range(PEER_TOPK):
            cnt1 = jnp.where(s1 == v1[r:r + 1], n1[r:r + 1], cnt1)
    p1 = jnp.where(in1, jnp.exp(s1 - v1[0:1]) / z, 0.0)
    e2_dense = jnp.where(in2, jnp.exp(s2 - v2[0:1]), 0.0)
    winners = (jnp.sum(jnp.where(in1, 1.0, 0.0), axis=0, keepdims=True)
               + jnp.sum(jnp.where(in2, 1.0, 0.0), axis=0, keepdims=True)
               + jnp.sum(n1, axis=0, keepdims=True))
    return cnt1, p1, rank2, e2_dense, winners


def _route_unit(qh, sk, tabs, exact_ties):
    half = PEER_QDIM // 2
    s1 = lax.dot_general(sk[0], qh[:, :half], NT_DIMS, preferred_element_type=F32)
    s2 = lax.dot_general(sk[1], qh[:, half:], NT_DIMS, preferred_element_type=F32)
    cnt1, p1, rank2, e2_dense, winners = yield from _route_head(s1, s2, exact_ties)
    cnt_ref, p1_ref, rk2_ref, e2_ref = tabs
    cnt_ref[...] = cnt1
    p1_ref[...] = p1
    rk2_ref[...] = rank2.astype(BF16)
    e2_ref[...] = e2_dense.astype(BF16)
    return winners


class _RouteUnits:
    def __init__(self, qh_ref, sk_ref, tabs, n):
        self.args = [(qh_ref, sk_ref, [t.at[u] for t in tabs], u) for u in range(n)]
        self.steps = [_route_unit(q[u], k[u], t, False) for q, k, t, u in self.args]
        self.winners = [None] * n

    def advance(self, k):
        for u, steps in enumerate(self.steps):
            if self.winners[u] is None:
                try:
                    for _ in range(k):
                        next(steps)
                except StopIteration as done:
                    self.winners[u] = done.value

    def finish(self):
        while any(w is None for w in self.winners):
            self.advance(1)
        for (q, k, t, u), w in zip(self.args, self.winners):
            @pl.when(jnp.max(w) > float(3 * PEER_TOPK))
            def _(q=q, k=k, t=t, u=u):
                _drain(_route_unit(q[u], k[u], t, True))


def _peer_query_kernel(h_ref, g_ref, wpq_ref, xnt_ref, qh_ref):
    hn = _rms(h_ref[...], g_ref[...])
    xnt_ref[...] = hn.T.astype(BF16)
    q = jnp.dot(hn.astype(BF16), wpq_ref[...], preferred_element_type=F32)
    for hh in range(PEER_HEADS):
        qh_ref[hh] = q[:, hh * PEER_QDIM:(hh + 1) * PEER_QDIM].astype(BF16)


def peer_query(h, g, wpq_bf):
    t, d = h.shape
    tq = QUERY_TQ
    return pl.pallas_call(
        _peer_query_kernel,
        grid=(t // tq,),
        in_specs=[pl.BlockSpec((tq, d), lambda i: (i, 0)),
                  pl.BlockSpec((1, d), lambda i: (0, 0)),
                  pl.BlockSpec(wpq_bf.shape, lambda i: (0, 0), pipeline_mode=pl.Buffered(1))],
        out_specs=[pl.BlockSpec((d, tq), lambda i: (0, i)),
                   pl.BlockSpec((PEER_HEADS, tq, PEER_QDIM), lambda i: (0, i, 0))],
        out_shape=[jax.ShapeDtypeStruct((d, t), BF16), jax.ShapeDtypeStruct((PEER_HEADS, t, PEER_QDIM), BF16)],
        compiler_params=_cparams(1, 48),
        name="peer_query",
    )(h, g.reshape(1, d), wpq_bf)


def _peer_route_kernel(qh_ref, sk_ref, cnt_ref, p1_ref, rk2_ref, e2_ref, *, n_route):
    _RouteUnits(qh_ref, sk_ref, (cnt_ref, p1_ref, rk2_ref, e2_ref), n_route).finish()


def peer_route(qh, sk_bf, n_route=2):
    t = qh.shape[1]
    hps = PEER_HEADS // n_route
    tab = pl.BlockSpec((n_route, PEER_NKEYS, ROUTE_TQ), lambda i, j: (j, 0, i))
    return pl.pallas_call(
        functools.partial(_peer_route_kernel, n_route=n_route),
        grid=(t // ROUTE_TQ, hps),
        in_specs=[pl.BlockSpec((n_route, ROUTE_TQ, PEER_QDIM), lambda i, j: (j, i, 0)),
                  pl.BlockSpec((n_route, 2, PEER_NKEYS, PEER_QDIM // 2), lambda i, j: (j, 0, 0, 0))],
        out_specs=[tab] * 4,
        out_shape=[jax.ShapeDtypeStruct((PEER_HEADS, PEER_NKEYS, t), dt) for dt in (F32, F32, BF16, BF16)],
        compiler_params=_cparams(2, 32),
        name="peer_route",
    )(qh, sk_bf)


def _peer_dense_kernel(xnt_ref, u_ref, vt_ref, cnt_ref, p1_ref, rk2_ref, e2_ref, h_ref, fg_ref, y_ref, acc_ref,
                       *, te, n_e):
    j = pl.program_id(1)

    @pl.when(j == 0)
    def _():
        acc_ref[...] = jnp.zeros(acc_ref.shape, F32)

    tq = xnt_ref.shape[1]
    groups = te // PEER_NKEYS
    gpc = PEER_ROW_CHUNK // PEER_NKEYS
    parts = []
    for a in range(groups):
        if a % gpc == 0:
            rows = slice(a * PEER_NKEYS, (a + gpc) * PEER_NKEYS)
            act = jax.nn.gelu(jnp.dot(u_ref[rows, :], xnt_ref[...], preferred_element_type=F32).astype(BF16))
        i1 = j * groups + a
        gate = jnp.zeros((PEER_NKEYS, tq), BF16)
        for hh in range(PEER_HEADS):
            rep = lambda ref: jnp.concatenate(
                [jnp.broadcast_to(ref[hh, pl.ds(i1, 1), :], (16, tq)).astype(BF16)] * (PEER_NKEYS // 16), axis=0)
            gate = gate + jnp.where(rk2_ref[hh] < rep(cnt_ref), rep(p1_ref) * e2_ref[hh], jnp.zeros((), BF16))
        lo = (a % gpc) * PEER_NKEYS
        parts.append(act[lo:lo + PEER_NKEYS] * gate)
    hm_t = parts[0] if groups == 1 else jnp.concatenate(parts, axis=0)
    acc_ref[...] += jnp.dot(vt_ref[...], hm_t, preferred_element_type=F32)

    @pl.when(j == n_e - 1)
    def _():
        y_ref[...] = _rms(h_ref[...] + acc_ref[...].T, fg_ref[...])


def peer_dense(xnt, u_bf, vt_bf, tabs, h, final_g, tq, te):
    t, d = h.shape
    n_e = u_bf.shape[0] // te
    once = dict(pipeline_mode=pl.Buffered(1))
    tab = pl.BlockSpec((PEER_HEADS, PEER_NKEYS, tq), lambda i, j: (0, 0, i), **once)
    return pl.pallas_call(
        functools.partial(_peer_dense_kernel, te=te, n_e=n_e),
        grid=(t // tq, n_e),
        in_specs=[pl.BlockSpec((d, tq), lambda i, j: (0, i), **once),
                  pl.BlockSpec((te, d), lambda i, j: (j, 0)),
                  pl.BlockSpec((d, te), lambda i, j: (0, j)),
                  tab, tab, tab, tab,
                  pl.BlockSpec((tq, d), lambda i, j: (i, 0), **once),
                  pl.BlockSpec((1, d), lambda i, j: (0, 0))],
        out_specs=pl.BlockSpec((tq, d), lambda i, j: (i, 0)),
        out_shape=jax.ShapeDtypeStruct((t, d), F32),
        scratch_shapes=[pltpu.VMEM((d, tq), F32)],
        compiler_params=_cparams(2, 56),
        name="peer_dense",
    )(xnt, u_bf, vt_bf, *tabs, h, final_g.reshape(1, d))


def _cast_transpose_kernel(x_ref, o_ref):
    o_ref[...] = x_ref[...].T.astype(o_ref.dtype)


def cast_transpose(w, rows):
    n, d = w.shape
    return pl.pallas_call(
        _cast_transpose_kernel,
        grid=(n // rows,),
        in_specs=[pl.BlockSpec((rows, d), lambda i: (i, 0))],
        out_specs=pl.BlockSpec((d, rows), lambda i: (0, i)),
        out_shape=jax.ShapeDtypeStruct((d, n), BF16),
        compiler_params=_cparams(1, 48),
        name="cast_transpose",
    )(w)


def kernel(x_prompt, x_sample, mem_prompt, cache_k, cache_v, cache_conv, cache_mem_k, cache_mem_v, page_table,
           norm1_g, w_in, rel_bias, conv_w, conv_b, conv_ln_g, conv_ln_b, w_out, norm2_g, mem_norm_g, w_cq,
           w_mk, w_mv, w_co, norm3_g, w_pq, peer_sub_keys, peer_u, peer_v, final_g):
    depth = w_in.shape[0]
    assert depth == 1, "single-layer step"
    n_p, seq, d = x_prompt.shape
    n_s, t_s, _ = x_sample.shape
    l = 0
    bf = lambda a: a.astype(BF16)

    w_in_bf, w_out_bf = bf(w_in[l]), bf(w_out[l])
    w_cq_bf, w_co_bf, w_pq_bf = bf(w_cq[l]), bf(w_co[l]), bf(w_pq[l])
    w_mem_bf = bf(jnp.concatenate([w_mk[l], w_mv[l]], axis=1))
    sk_bf, u_bf, vt_bf = bf(peer_sub_keys[l]), bf(peer_u[l]), cast_transpose(peer_v[l], PEER_EXPERT_TILE)

    n_pool, n_pages = cache_k.shape[1], page_table.shape[1]
    bias_p, t_far, t_last, t_own = bias_tiles(rel_bias, n_pages * PAGE_SIZE)
    c31_h = rel_bias[NUM_BUCKETS - 1]

    def cross_and_query(x, attn, conv, mk, mv, n_grp, rows, tiles_per_seq):
        h = out_proj(x, attn, conv, w_out_bf, tm=OUT_PROJ_ROWS)
        h = cross_block(h, norm2_g[l], w_cq_bf, w_co_bf, mk, mv, n_grp, rows, tiles_per_seq)
        return (h,) + tuple(peer_query(h, norm3_g[l], w_pq_bf))

    experts = lambda xnt, tabs, h: peer_dense(xnt, u_bf, vt_bf, tabs, h, final_g,
                                              tq=PEER_TOKEN_TILE, te=PEER_EXPERT_TILE)

    xp = x_prompt.reshape(n_p * seq, d)
    mk_p, mv_p = rms_matmul(mem_prompt.reshape(n_p * N_MEM, d), mem_norm_g[l], w_mem_bf, 2, tm=MEM_PROJ_ROWS)
    q_p, k_p, v_p, ga_p, gg_p = rms_matmul(xp, norm1_g[l], w_in_bf, 5, tm=IN_PROJ_ROWS)
    attn_p = moba_prompt(q_p, k_p, v_p, bias_p, c31_h, n_p, seq)
    zero_buf = jnp.zeros((n_p, CONV_K - 1, CONV_CH), F32)
    conv_p, buf_p = conformer_conv(ga_p, gg_p, zero_buf, conv_w[l], conv_b[l], conv_ln_g[l], conv_ln_b[l],
                                   n_p, seq, tt=CONV_ROWS, sb=1)
    h_p, xnt_p, qh_p = cross_and_query(xp, attn_p, conv_p, mk_p.reshape(n_p, N_MEM, X_WIDTH),
                                       mv_p.reshape(n_p, N_MEM, X_WIDTH), 1, CROSS_ROWS, seq // CROSS_ROWS)

    xs = x_sample.reshape(n_s * t_s, d)
    q_s, k_s, v_s, ga_s, gg_s = rms_matmul(xs, norm1_g[l], w_in_bf, 5, tm=IN_PROJ_ROWS)
    tok3 = lambda a: a.reshape(n_s, t_s, ATTN_WIDTH)
    units_p = PEER_HEADS * (n_p * seq // ROUTE_TQ)
    fuse = units_p % n_s == 0 and PEER_HEADS % (units_p // n_s) == 0
    attn_s, *tabs_p = moba_sample(tok3(q_s), tok3(k_s), tok3(v_s),
                                  cache_k[l].reshape(n_pool, PAGE_SIZE * N_HEADS, HEAD_DIM),
                                  cache_v[l].reshape(n_pool, PAGE_SIZE * N_HEADS, HEAD_DIM),
                                  page_table, t_far, t_last, t_own, route=(qh_p, sk_bf) if fuse else None)
    if not fuse:
        tabs_p = peer_route(qh_p, sk_bf)
    y_p = experts(xnt_p, tabs_p, h_p)
    conv_s, buf_s = conformer_conv(ga_s, gg_s, cache_conv[l], conv_w[l], conv_b[l], conv_ln_g[l], conv_ln_b[l],
                                   n_s, t_s, tt=t_s, sb=SHORT_SEQS_PER_STEP)
    h_s, xnt_s, qh_s = cross_and_query(xs, attn_s.reshape(n_s * t_s, ATTN_WIDTH), conv_s,
                                       cache_mem_k[l].reshape(n_s, N_MEM * X_HEADS, HEAD_DIM),
                                       cache_mem_v[l].reshape(n_s, N_MEM * X_HEADS, HEAD_DIM),
                                       SHORT_SEQS_PER_STEP, t_s, 1)
    y_s = experts(xnt_s, peer_route(qh_s, sk_bf), h_s)

    n_pg = seq // PAGE_SIZE
    return (y_p.reshape(n_p, seq, d), y_s.reshape(n_s, t_s, d),
            k_p.reshape(1, n_p, n_pg, PAGE_SIZE, N_HEADS, HEAD_DIM),
            v_p.reshape(1, n_p, n_pg, PAGE_SIZE, N_HEADS, HEAD_DIM),
            buf_p[None],
            mk_p.reshape(1, n_p, N_MEM, X_HEADS, HEAD_DIM), mv_p.reshape(1, n_p, N_MEM, X_HEADS, HEAD_DIM),
            k_s.reshape(1, n_s, t_s, N_HEADS, HEAD_DIM), v_s.reshape(1, n_s, t_s, N_HEADS, HEAD_DIM),
            buf_s[None])
```

```python
import functools
import math

import numpy as np
import jax
import jax.numpy as jnp
from jax import lax
from jax.experimental import pallas as pl
from jax.experimental.pallas import tpu as pltpu

F32 = jnp.float32
BF16 = jnp.bfloat16

D_MODEL = 2048
HEAD_DIM = 128
N_HEADS = 8
ATTN_WIDTH = N_HEADS * HEAD_DIM
CONV_CH = 1024
MOBA_BLOCK = 256
MOBA_TOPK = 3
PAGE_SIZE = 128
NUM_BUCKETS = 32
MAX_EXACT = 16
REL_MAX_DIST = 128
CONV_K = 31
N_MEM = 256
X_HEADS = 4
X_WIDTH = X_HEADS * HEAD_DIM
PEER_HEADS = 8
PEER_NKEYS = 128
PEER_TOPK = 16
PEER_QDIM = 256
PEER_ROW_CHUNK = 512
ROUTE_TQ = 256
QUERY_TQ = 512
ROUTE_STEPS_PER_PAGE = 2
EPS = 1e-6
NEG = -1e30

IN_PROJ_ROWS = 512
MEM_PROJ_ROWS = 512
OUT_PROJ_ROWS = 512
CONV_ROWS = 512
SHORT_SEQS_PER_STEP = 8
CROSS_ROWS = 512
PEER_TOKEN_TILE = 512
PEER_EXPERT_TILE = 1024

MIB = 1024 * 1024
NT_DIMS = (((1,), (1,)), ((), ()))


def _cparams(n_grid, vmem_mib):
    return pltpu.CompilerParams(dimension_semantics=("arbitrary",) * n_grid,
                                vmem_limit_bytes=vmem_mib * MIB)


def _rms(x, g):
    return x * lax.rsqrt(jnp.mean(x * x, axis=-1, keepdims=True) + EPS) * g


def _rms_matmul_kernel(x_ref, g_ref, w_ref, *outs):
    xn = _rms(x_ref[...], g_ref[...]).astype(BF16)
    tn = outs[0].shape[1]
    for s, o_ref in enumerate(outs):
        o_ref[...] = jnp.dot(xn, w_ref[:, s * tn:(s + 1) * tn], preferred_element_type=F32)


def rms_matmul(x, g, w_bf, n_out, tm):
    t, d = x.shape
    tn = w_bf.shape[1] // n_out
    return pl.pallas_call(
        _rms_matmul_kernel,
        grid=(t // tm,),
        in_specs=[pl.BlockSpec((tm, d), lambda i: (i, 0)),
                  pl.BlockSpec((1, d), lambda i: (0, 0)),
                  pl.BlockSpec(w_bf.shape, lambda i: (0, 0), pipeline_mode=pl.Buffered(1))],
        out_specs=[pl.BlockSpec((tm, tn), lambda i: (i, 0))] * n_out,
        out_shape=[jax.ShapeDtypeStruct((t, tn), F32)] * n_out,
        compiler_params=_cparams(1, 56),
        name="rms_matmul",
    )(x, g.reshape(1, d), w_bf)


def _bucket_np(rel):
    n = np.maximum(rel, 0)
    nf = np.maximum(n, 1).astype(np.float32)
    large = MAX_EXACT + (np.log(nf / MAX_EXACT) / np.float32(math.log(REL_MAX_DIST / MAX_EXACT))
                         * (NUM_BUCKETS - MAX_EXACT)).astype(np.int32)
    large = np.minimum(large, NUM_BUCKETS - 1)
    return np.where(n < MAX_EXACT, n, large).astype(np.int32)


def _bias_kernel(rb_ref, rbx_ref, bkp_ref, bkf_ref, bkl_ref, bko_ref, tp_ref, tf_ref, tl_ref, to_ref):
    def lookup(bk, table):
        acc = jnp.full(bk.shape, NEG, F32)
        for b in range(NUM_BUCKETS):
            acc = jnp.where(bk == b, table(b), acc)
        return acc

    for h in range(N_HEADS):
        for t in range(2):
            tp_ref[h, t] = lookup(bkp_ref[t], lambda b: rb_ref[b, h])
    by_row = lambda b: rbx_ref[b]
    tf_ref[...] = lookup(bkf_ref[...], by_row)
    tl_ref[...] = lookup(bkl_ref[...], by_row)
    to_ref[...] = lookup(bko_ref[...], by_row)


def bias_tiles(rel_bias, n_past):
    key = np.arange(MOBA_BLOCK)[:, None]
    qry = np.arange(MOBA_BLOCK)[None, :]
    bkp = np.stack([_bucket_np(qry - key), _bucket_np(MOBA_BLOCK + qry - key)])
    r = np.arange(N_HEADS * 8)[:, None]
    rh, rq = r // 8, r % 8
    c = np.arange(PAGE_SIZE * N_HEADS)[None, :]
    ct, ch = c // N_HEADS, c % N_HEADS
    bkf = np.where(rh == ch, NUM_BUCKETS - 1, -1)
    bkl = np.where(rh == ch, _bucket_np(n_past + rq - (n_past - PAGE_SIZE + ct)), -1)
    assert PAGE_SIZE >= REL_MAX_DIST
    co = np.arange(128)[None, :]
    coh, cot = co // 8, co % 8
    bko = np.where((rh == coh) & (cot <= rq), _bucket_np(rq - cot), -1)
    rbx = jnp.repeat(rel_bias, 8, axis=1)[:, :, None]
    vm = pl.BlockSpec(memory_space=pltpu.VMEM)
    i32 = lambda a: jnp.asarray(a.astype(np.int32))
    return pl.pallas_call(
        _bias_kernel,
        in_specs=[pl.BlockSpec(memory_space=pltpu.SMEM), vm, vm, vm, vm, vm],
        out_specs=[vm, vm, vm, vm],
        out_shape=[jax.ShapeDtypeStruct((N_HEADS, 2, MOBA_BLOCK, MOBA_BLOCK), F32),
                   jax.ShapeDtypeStruct(bkf.shape, F32),
                   jax.ShapeDtypeStruct(bkl.shape, F32),
                   jax.ShapeDtypeStruct(bko.shape, F32)],
        compiler_params=pltpu.CompilerParams(vmem_limit_bytes=32 * MIB),
        name="bias_tiles",
    )(rel_bias, rbx, i32(bkp), i32(bkf), i32(bkl), i32(bko))


def _split_bf16(x):
    hi = x.astype(BF16)
    lo = (x - hi.astype(F32)).astype(BF16)
    return hi, lo


def _moba_prompt_kernel(q_ref, k_ref, v_ref, bias_ref, c31_ref, o_ref, qbf, kbf, vt, pen_ref, s_ref):
    nb = kbf.shape[0] // MOBA_BLOCK
    scale = HEAD_DIM ** -0.5
    blk = lambda b: slice(b * MOBA_BLOCK, (b + 1) * MOBA_BLOCK)

    q = q_ref[...]
    q_hi, q_lo = _split_bf16(q)
    qbf[...] = q_hi
    kbf[...] = k_ref[...].astype(BF16)
    km = jnp.concatenate([jnp.mean(k_ref[blk(b), :], axis=0, keepdims=True) for b in range(nb)]
                         + [jnp.zeros((16 - nb, HEAD_DIM), F32)], axis=0)
    for b in range(nb):
        vt[b] = v_ref[blk(b), :].T.astype(BF16)
    km_hi, km_lo = _split_bf16(km)
    gate = (lax.dot_general(km_hi, q_hi, NT_DIMS, preferred_element_type=F32)
            + lax.dot_general(km_lo, q_hi, NT_DIMS, preferred_element_type=F32)
            + lax.dot_general(km_hi, q_lo, NT_DIMS, preferred_element_type=F32))

    row = lax.broadcasted_iota(jnp.int32, gate.shape, 0)
    own = lax.broadcasted_iota(jnp.int32, gate.shape, 1) // MOBA_BLOCK
    rank = jnp.zeros(gate.shape, F32)
    for b2 in range(nb):
        gb = gate[b2:b2 + 1, :]
        beats = ((gb > gate) | ((gb == gate) & (b2 < row))) & (b2 < own)
        rank = rank + jnp.where(beats, 1.0, 0.0)
    pen_ref[...] = jnp.where((row < own) & (rank < float(MOBA_TOPK)), 0.0, NEG)

    key = lax.broadcasted_iota(jnp.int32, (MOBA_BLOCK, MOBA_BLOCK), 0)
    qry = lax.broadcasted_iota(jnp.int32, (MOBA_BLOCK, MOBA_BLOCK), 1)
    c31 = c31_ref[pl.program_id(1)]
    for qi in range(nb):
        m = None
        for kb in range(qi + 1):
            s = lax.dot_general(kbf[blk(kb), :], qbf[blk(qi), :], NT_DIMS, preferred_element_type=F32) * scale
            if kb == qi:
                s = jnp.where(key <= qry, s + bias_ref[0, 0], NEG)
            else:
                s = s + (bias_ref[0, 1] if kb == qi - 1 else c31) + pen_ref[kb:kb + 1, blk(qi)]
            s_ref[kb] = s
            cm = jnp.max(s, axis=0, keepdims=True)
            m = cm if m is None else jnp.maximum(m, cm)
        lsum = jnp.zeros((1, MOBA_BLOCK), F32)
        acc = jnp.zeros((HEAD_DIM, MOBA_BLOCK), F32)
        for kb in range(qi + 1):
            p = jnp.exp(s_ref[kb] - m)
            lsum = lsum + jnp.sum(p, axis=0, keepdims=True)
            acc = acc + jnp.dot(vt[kb], p.astype(BF16), preferred_element_type=F32)
        o_ref[blk(qi), :] = (acc / lsum).T.astype(o_ref.dtype)


def moba_prompt(q, k, v, bias_p, c31_h, n_seq, seq):
    nq = seq // MOBA_BLOCK
    tok = pl.BlockSpec((seq, HEAD_DIM), lambda n, h: (n, h))
    return pl.pallas_call(
        _moba_prompt_kernel,
        grid=(n_seq, N_HEADS),
        in_specs=[tok, tok, tok,
                  pl.BlockSpec((1, 2, MOBA_BLOCK, MOBA_BLOCK), lambda n, h: (h, 0, 0, 0)),
                  pl.BlockSpec(memory_space=pltpu.SMEM)],
        out_specs=tok,
        out_shape=jax.ShapeDtypeStruct(q.shape, BF16),
        scratch_shapes=[pltpu.VMEM((seq, HEAD_DIM), BF16), pltpu.VMEM((seq, HEAD_DIM), BF16),
                        pltpu.VMEM((nq, HEAD_DIM, MOBA_BLOCK), BF16), pltpu.VMEM((16, seq), F32),
                        pltpu.VMEM((nq, MOBA_BLOCK, MOBA_BLOCK), F32)],
        compiler_params=_cparams(2, 48),
        name="moba_prompt",
    )(q, k, v, bias_p, c31_h)


def _by_head_rows(x):
    return jnp.concatenate([x[:, h * HEAD_DIM:(h + 1) * HEAD_DIM] for h in range(x.shape[1] // HEAD_DIM)], axis=0)


def _by_head_lanes(x, n_heads):
    t = x.shape[0] // n_heads
    return jnp.concatenate([x[h * t:(h + 1) * t] for h in range(n_heads)], axis=1)


def _moba_sample_kernel(pt_ref, q_ref, kn_ref, vn_ref, tf_ref, tl_ref, to_ref, *refs, n_pages, n_route):
    kp, vp = refs[:n_pages], refs[n_pages:2 * n_pages]
    refs = refs[2 * n_pages:]
    if n_route:
        qh_ref, sk_ref, o_ref, cnt_ref, p1_ref, rk2_ref, e2_ref, s_ref = refs
    else:
        o_ref, s_ref = refs
    ppb = MOBA_BLOCK // PAGE_SIZE
    nb = n_pages // ppb
    scale = HEAD_DIM ** -0.5
    t_new = q_ref.shape[1]
    nr = N_HEADS * t_new

    route = _RouteUnits(qh_ref, sk_ref, (cnt_ref, p1_ref, rk2_ref, e2_ref), n_route) if n_route else None
    interleave = (lambda: route.advance(ROUTE_STEPS_PER_PAGE)) if n_route else (lambda: None)

    q = _by_head_rows(q_ref[0])
    q_bf = (q * scale).astype(BF16)
    zpad = jnp.zeros((128 - nr, HEAD_DIM), F32)
    kn = jnp.concatenate([_by_head_rows(kn_ref[0]), zpad], axis=0).astype(BF16)
    vn = jnp.concatenate([_by_head_rows(vn_ref[0]), zpad], axis=0).astype(BF16)

    s_own = lax.dot_general(q_bf, kn, NT_DIMS, preferred_element_type=F32) + to_ref[...]
    blk_max, gates = [], []
    for b in range(nb):
        bm = None
        ksum = jnp.zeros((N_HEADS, HEAD_DIM), F32)
        for pg in range(b * ppb, (b + 1) * ppb):
            kpage = kp[pg][...]
            ksum = ksum + jnp.sum(kpage.reshape(PAGE_SIZE, N_HEADS, HEAD_DIM), axis=0)
            s = lax.dot_general(q_bf, kpage.astype(BF16), NT_DIMS, preferred_element_type=F32)
            s = s + (tl_ref[...] if pg == n_pages - 1 else tf_ref[...])
            s_ref[pg] = s
            pm = jnp.max(s, axis=-1, keepdims=True)
            bm = pm if bm is None else jnp.maximum(bm, pm)
            interleave()
        blk_max.append(bm)
        kmean = ksum / float(MOBA_BLOCK)
        krep = jnp.concatenate([jnp.broadcast_to(kmean[h:h + 1, :], (t_new, HEAD_DIM)) for h in range(N_HEADS)],
                               axis=0)
        gates.append(jnp.sum(q * krep, axis=1, keepdims=True))

    penalty = []
    for b in range(nb):
        rank = jnp.zeros((nr, 1), F32)
        for b2 in range(nb):
            if b2 != b:
                beats = (gates[b2] > gates[b]) | ((gates[b2] == gates[b]) & (b2 < b))
                rank = rank + jnp.where(beats, 1.0, 0.0)
        penalty.append(jnp.where(rank < float(MOBA_TOPK), 0.0, NEG))

    m = jnp.max(s_own, axis=-1, keepdims=True)
    for b in range(nb):
        m = jnp.maximum(m, blk_max[b] + penalty[b])

    p = jnp.exp(s_own - m)
    lsum = jnp.sum(p, axis=-1, keepdims=True)
    acc = jnp.dot(p.astype(BF16), vn, preferred_element_type=F32)
    for pg in range(n_pages):
        p = jnp.exp(s_ref[pg] + (penalty[pg // ppb] - m))
        lsum = lsum + jnp.sum(p, axis=-1, keepdims=True)
        acc = acc + jnp.dot(p.astype(BF16), vp[pg][...].astype(BF16), preferred_element_type=F32)
        interleave()
    o_ref[0] = _by_head_lanes(acc / lsum, N_HEADS)

    if n_route:
        route.finish()


def moba_sample(q, kn, vn, ck, cv, page_table, t_far, t_last, t_own, route=None):
    nseq, n_pages = page_table.shape
    tok = pl.BlockSpec((1,) + q.shape[1:], lambda b, pt: (b, 0, 0))

    def page_spec(p):
        return pl.BlockSpec((None,) + ck.shape[1:], lambda b, pt, p=p: (pt[b, p], 0, 0))

    const = lambda a: pl.BlockSpec(a.shape, lambda b, pt: (0, 0))
    in_specs = ([tok, tok, tok, const(t_far), const(t_last), const(t_own)]
                + [page_spec(p) for p in range(n_pages)] * 2)
    out_specs, out_shape, extra, n_route = [tok], [jax.ShapeDtypeStruct(q.shape, F32)], [], 0
    if route is not None:
        qh, sk_bf = route
        t = qh.shape[1]
        units = PEER_HEADS * (t // ROUTE_TQ)
        n_route = units // nseq
        assert n_route * nseq == units and PEER_HEADS % n_route == 0
        hps = PEER_HEADS // n_route
        in_specs += [pl.BlockSpec((n_route, ROUTE_TQ, PEER_QDIM), lambda b, pt: (b % hps, b // hps, 0)),
                     pl.BlockSpec((n_route, 2, PEER_NKEYS, PEER_QDIM // 2), lambda b, pt: (b % hps, 0, 0, 0))]
        tab = pl.BlockSpec((n_route, PEER_NKEYS, ROUTE_TQ), lambda b, pt: (b % hps, 0, b // hps))
        out_specs += [tab] * 4
        out_shape += [jax.ShapeDtypeStruct((PEER_HEADS, PEER_NKEYS, t), dt) for dt in (F32, F32, BF16, BF16)]
        extra = [qh, sk_bf]
    grid_spec = pltpu.PrefetchScalarGridSpec(
        num_scalar_prefetch=1, grid=(nseq,), in_specs=in_specs, out_specs=out_specs,
        scratch_shapes=[pltpu.VMEM((n_pages, N_HEADS * q.shape[1], ck.shape[1]), F32)])
    return pl.pallas_call(
        functools.partial(_moba_sample_kernel, n_pages=n_pages, n_route=n_route),
        grid_spec=grid_spec,
        out_shape=out_shape,
        compiler_params=_cparams(1, 56),
        name="moba_sample",
    )(page_table, q, kn, vn, t_far, t_last, t_own, *([ck] * n_pages), *([cv] * n_pages), *extra)


HIST_ROWS = 32
HIST_OFF = HIST_ROWS - (CONV_K - 1)


def _conv_kernel(ga_ref, gg_ref, hist_ref, w_ref, b_ref, lg_ref, lb_ref, o_ref, nb_ref, ext_ref, y_ref,
                 *, tt, n_t, sb):
    t = pl.program_id(1)

    @pl.when(t == 0)
    def _():
        for s in range(sb):
            ext_ref[s, HIST_OFF:HIST_ROWS, :] = hist_ref[s]

    rt = min(tt, 128)
    for s in range(sb):
        rows = slice(s * tt, (s + 1) * tt)
        ext_ref[s, HIST_ROWS:HIST_ROWS + tt, :] = ga_ref[rows, :] * jax.nn.sigmoid(gg_ref[rows, :])
        for c in range(CONV_CH // 128):
            cs = slice(c * 128, (c + 1) * 128)
            for r0 in range(0, tt, rt):
                acc = jnp.zeros((rt, 128), F32)
                for res in range(8):
                    taps = [j for j in range(CONV_K) if (HIST_OFF + j) % 8 == res]
                    q0 = (HIST_OFF + taps[0]) // 8
                    q1 = (HIST_OFF + taps[-1]) // 8
                    if res:
                        win = ext_ref[s, r0 + 8 * q0:r0 + 8 * (q1 + 1) + rt, cs]
                        win = pltpu.roll(win, win.shape[0] - res, axis=0)
                    else:
                        win = ext_ref[s, r0 + 8 * q0:r0 + 8 * q1 + rt, cs]
                    for j in taps:
                        off = 8 * ((HIST_OFF + j) // 8 - q0)
                        acc = acc + w_ref[j:j + 1, cs] * win[off:off + rt]
                y_ref[s * tt + r0:s * tt + r0 + rt, cs] = acc + b_ref[:, cs]
    y = y_ref[...]
    mu = jnp.mean(y, axis=-1, keepdims=True)
    yc = y - mu
    var = jnp.mean(yc * yc, axis=-1, keepdims=True)
    yn = yc * lax.rsqrt(var + EPS) * lg_ref[...] + lb_ref[...]
    o_ref[...] = (yn * jax.nn.sigmoid(yn)).astype(o_ref.dtype)

    @pl.when(t == n_t - 1)
    def _():
        for s in range(sb):
            nb_ref[s] = ext_ref[s, tt + HIST_OFF:tt + HIST_ROWS, :]

    if n_t > 1:
        for s in range(sb):
            ext_ref[s, 0:HIST_ROWS, :] = ext_ref[s, tt:tt + HIST_ROWS, :]


def conformer_conv(ga, gg, hist, w_dw, b_dw, ln_g, ln_b, n_seq, seq, tt, sb):
    n_t = seq // tt
    assert sb == 1 or n_t == 1
    row = lambda a: a.reshape(1, CONV_CH)
    cvec = pl.BlockSpec((1, CONV_CH), lambda n, t: (0, 0))
    tile = pl.BlockSpec((sb * tt, CONV_CH), lambda n, t: (n * n_t + t, 0))
    hist_spec = pl.BlockSpec((sb, CONV_K - 1, CONV_CH), lambda n, t: (n, 0, 0))
    return pl.pallas_call(
        functools.partial(_conv_kernel, tt=tt, n_t=n_t, sb=sb),
        grid=(n_seq // sb, n_t),
        in_specs=[tile, tile, hist_spec, pl.BlockSpec((CONV_K, CONV_CH), lambda n, t: (0, 0)),
                  cvec, cvec, cvec],
        out_specs=[tile, hist_spec],
        out_shape=[jax.ShapeDtypeStruct((n_seq * seq, CONV_CH), BF16),
                   jax.ShapeDtypeStruct((n_seq, CONV_K - 1, CONV_CH), F32)],
        scratch_shapes=[pltpu.VMEM((sb, HIST_ROWS + tt, CONV_CH), F32), pltpu.VMEM((sb * tt, CONV_CH), F32)],
        compiler_params=_cparams(2, 32),
        name="conformer_conv",
    )(ga, gg, hist, w_dw, row(b_dw), row(ln_g), row(ln_b))


def _out_proj_kernel(x_ref, a_ref, c_ref, w_ref, o_ref):
    wa = a_ref.shape[1]
    y = jnp.dot(a_ref[...].astype(BF16), w_ref[0:wa, :], preferred_element_type=F32)
    y = y + jnp.dot(c_ref[...].astype(BF16), w_ref[wa:, :], preferred_element_type=F32)
    o_ref[...] = x_ref[...] + y


def out_proj(x, a, c, w_bf, tm):
    t, d = x.shape
    return pl.pallas_call(
        _out_proj_kernel,
        grid=(t // tm,),
        in_specs=[pl.BlockSpec((tm, d), lambda i: (i, 0)),
                  pl.BlockSpec((tm, a.shape[1]), lambda i: (i, 0)),
                  pl.BlockSpec((tm, c.shape[1]), lambda i: (i, 0)),
                  pl.BlockSpec(w_bf.shape, lambda i: (0, 0))],
        out_specs=pl.BlockSpec((tm, d), lambda i: (i, 0)),
        out_shape=jax.ShapeDtypeStruct((t, d), F32),
        compiler_params=_cparams(1, 48),
        name="out_proj",
    )(x, a, c, w_bf)


def _softmax_pv(s, mv_bf):
    m = jnp.max(s, axis=-1, keepdims=True)
    p = jnp.exp(s - m)
    o = jnp.dot(p.astype(BF16), mv_bf, preferred_element_type=F32)
    return o / jnp.sum(p, axis=-1, keepdims=True)


def _cross_kernel(h_ref, g_ref, wq_ref, wo_ref, mk_ref, mv_ref, o_ref, *, n_grp, rows):
    scale = HEAD_DIM ** -0.5
    h = h_ref[...]
    hn = _rms(h, g_ref[...]).astype(BF16)
    q = jnp.dot(hn, wq_ref[...], preferred_element_type=F32)
    outs = []
    for g in range(n_grp):
        mk = mk_ref[g].astype(BF16)
        mv = mv_ref[g].astype(BF16)
        qg = q[g * rows:(g + 1) * rows]
        if rows >= 128:
            heads = []
            for hd in range(X_HEADS):
                cs = slice(hd * HEAD_DIM, (hd + 1) * HEAD_DIM)
                s = lax.dot_general(qg[:, cs].astype(BF16), mk[:, cs], NT_DIMS, preferred_element_type=F32)
                heads.append(_softmax_pv(s * scale, mv[:, cs]))
            outs.append(jnp.concatenate(heads, axis=1))
        else:
            qx = _by_head_rows(qg).astype(BF16)
            s = lax.dot_general(qx, mk, NT_DIMS, preferred_element_type=F32) * scale
            rowh = lax.broadcasted_iota(jnp.int32, s.shape, 0) // rows
            colh = lax.broadcasted_iota(jnp.int32, s.shape, 1) % X_HEADS
            o = _softmax_pv(jnp.where(rowh == colh, s, NEG), mv)
            outs.append(_by_head_lanes(o, X_HEADS))
    o_all = outs[0] if n_grp == 1 else jnp.concatenate(outs, axis=0)
    o_ref[...] = h + jnp.dot(o_all.astype(BF16), wo_ref[...], preferred_element_type=F32)


def cross_block(h, g, wq_bf, wo_bf, mk, mv, n_grp, rows, tiles_per_seq):
    t, d = h.shape
    tm = n_grp * rows
    if n_grp == 1:
        mem_map = lambda i: (i // tiles_per_seq, 0, 0)
    else:
        mem_map = lambda i: (i, 0, 0)
    mem_spec = pl.BlockSpec((n_grp,) + mk.shape[1:], mem_map)
    return pl.pallas_call(
        functools.partial(_cross_kernel, n_grp=n_grp, rows=rows),
        grid=(t // tm,),
        in_specs=[pl.BlockSpec((tm, d), lambda i: (i, 0)),
                  pl.BlockSpec((1, d), lambda i: (0, 0)),
                  pl.BlockSpec(wq_bf.shape, lambda i: (0, 0)),
                  pl.BlockSpec(wo_bf.shape, lambda i: (0, 0)),
                  mem_spec, mem_spec],
        out_specs=pl.BlockSpec((tm, d), lambda i: (i, 0)),
        out_shape=jax.ShapeDtypeStruct((t, d), F32),
        compiler_params=_cparams(1, 48),
        name="cross_block",
    )(h, g.reshape(1, d), wq_bf, wo_bf, mk, mv)


def _drain(steps):
    try:
        while True:
            next(steps)
    except StopIteration as done:
        return done.value


def _top16(s, exact_ties, want_rank=True):
    n, t = s.shape
    row = lax.broadcasted_iota(jnp.int32, (n, t), 0).astype(F32)
    row16 = lax.broadcasted_iota(jnp.int32, (PEER_TOPK, t), 0)
    rank = jnp.full((n, t), float(PEER_TOPK), F32)
    vals = jnp.zeros((PEER_TOPK, t), F32)
    for r in range(PEER_TOPK):
        m = jnp.max(s, axis=0, keepdims=True)
        if exact_ties:
            idx = jnp.min(jnp.where(s == m, row, float(n)), axis=0, keepdims=True)
            hit = row == idx
        else:
            hit = s == m
        if want_rank:
            rank = jnp.where(hit, float(r), rank)
        s = jnp.where(hit, -jnp.inf, s)
        vals = jnp.where(row16 == r, m, vals)
        yield
    return vals, rank


def _pair_pieces(v1, v2, e1, e2):
    t = v1.shape[1]
    sub = lax.broadcasted_iota(jnp.int32, (8, t), 0)
    subf = sub.astype(F32)
    pieces = []

    def col(b, a0, a_max):
        a = sub + a0
        pieces.append(dict(c=v1[a0:a0 + 8] + v2[b:b + 1], e=e1[a0:a0 + 8] * e2[b:b + 1],
                           f=(subf + a0) * 16.0 + b, ok=a <= a_max, a0=a0, row_a=None))

    def rowp(a, b0, b_min, b_max):
        b = sub + b0
        pieces.append(dict(c=v1[a:a + 1] + v2[b0:b0 + 8], e=e1[a:a + 1] * e2[b0:b0 + 8],
                           f=a * 16.0 + (subf + b0), ok=(b >= b_min) & (b <= b_max), a0=None, row_a=a))

    col(0, 0, 15), col(0, 8, 15), col(1, 0, 7), col(2, 0, 4), col(3, 0, 3)
    rowp(0, 8, 8, 15), rowp(0, 0, 4, 7), rowp(1, 0, 4, 7), rowp(2, 0, 4, 4)
    for p in pieces:
        p["c"] = jnp.where(p["ok"], p["c"], -jnp.inf)
        p["f"] = jnp.where(p["ok"], p["f"], -1.0)
    return pieces


def _route_head(s1, s2, exact_ties):
    tq = s1.shape[1]
    v1, rank1 = yield from _top16(s1, exact_ties, want_rank=exact_ties)
    v2, rank2 = yield from _top16(s2, exact_ties)
    e1 = jnp.exp(v1 - v1[0:1])
    e2 = jnp.exp(v2 - v2[0:1])
    pieces = _pair_pieces(v1, v2, e1, e2)

    taken = [jnp.zeros((8, tq), F32) for _ in pieces]
    cs = [p["c"] for p in pieces]
    for _ in range(PEER_TOPK):
        m = cs[0]
        for c in cs[1:]:
            m = jnp.maximum(m, c)
        m = jnp.max(m, axis=0, keepdims=True)
        if exact_ties:
            fm = None
            for c, p in zip(cs, pieces):
                cand = jnp.where(c == m, p["f"], 1e9)
                fm = cand if fm is None else jnp.minimum(fm, cand)
            fm = jnp.min(fm, axis=0, keepdims=True)
        for i, p in enumerate(pieces):
            hit = (p["f"] == fm) if exact_ties else (cs[i] == m)
            taken[i] = jnp.where(hit, 1.0, taken[i])
            cs[i] = jnp.where(hit, -jnp.inf, cs[i])
        yield

    row16 = lax.broadcasted_iota(jnp.int32, (PEER_TOPK, tq), 0)
    n1 = jnp.zeros((PEER_TOPK, tq), F32)
    z = jnp.zeros((1, tq), F32)
    for tf, p in zip(taken, pieces):
        z = z + jnp.sum(tf * p["e"], axis=0, keepdims=True)
        if p["row_a"] is None:
            pad = jnp.zeros((8, tq), F32)
            n1 = n1 + (jnp.concatenate([tf, pad], axis=0) if p["a0"] == 0
                       else jnp.concatenate([pad, tf], axis=0))
        else:
            n1 = n1 + jnp.where(row16 == p["row_a"], jnp.sum(tf, axis=0, keepdims=True), 0.0)

    in2 = rank2 < float(PEER_TOPK)
    cnt1 = jnp.zeros(s1.shape, F32)
    if exact_ties:
        in1 = rank1 < float(PEER_TOPK)
        for r in range(PEER_TOPK):
            cnt1 = jnp.where(rank1 == float(r), n1[r:r + 1], cnt1)
    else:
        in1 = s1 >= v1[PEER_TOPK - 1:PEER_TOPK]
        for r in range(PEER_TOPK):
            cnt1 = jnp.where(s1 == v1[r:r + 1], n1[r:r + 1], cnt1)
    p1 = jnp.where(in1, jnp.exp(s1 - v1[0:1]) / z, 0.0)
    e2_dense = jnp.where(in2, jnp.exp(s2 - v2[0:1]), 0.0)
    winners = (jnp.sum(jnp.where(in1, 1.0, 0.0), axis=0, keepdims=True)
               + jnp.sum(jnp.where(in2, 1.0, 0.0), axis=0, keepdims=True)
               + jnp.sum(n1, axis=0, keepdims=True))
    return cnt1, p1, rank2, e2_dense, winners


def _route_unit(qh, sk, tabs, exact_ties):
    half = PEER_QDIM // 2
    s1 = lax.dot_general(sk[0], qh[:, :half], NT_DIMS, preferred_element_type=F32)
    s2 = lax.dot_general(sk[1], qh[:, half:], NT_DIMS, preferred_element_type=F32)
    cnt1, p1, rank2, e2_dense, winners = yield from _route_head(s1, s2, exact_ties)
    cnt_ref, p1_ref, rk2_ref, e2_ref = tabs
    cnt_ref[...] = cnt1
    p1_ref[...] = p1
    rk2_ref[...] = rank2.astype(BF16)
    e2_ref[...] = e2_dense.astype(BF16)
    return winners


class _RouteUnits:
    def __init__(self, qh_ref, sk_ref, tabs, n):
        self.args = [(qh_ref, sk_ref, [t.at[u] for t in tabs], u) for u in range(n)]
        self.steps = [_route_unit(q[u], k[u], t, False) for q, k, t, u in self.args]
        self.winners = [None] * n

    def advance(self, k):
        for u, steps in enumerate(self.steps):
            if self.winners[u] is None:
                try:
                    for _ in range(k):
                        next(steps)
                except StopIteration as done:
                    self.winners[u] = done.value

    def finish(self):
        while any(w is None for w in self.winners):
            self.advance(1)
        for (q, k, t, u), w in zip(self.args, self.winners):
            @pl.when(jnp.max(w) > float(3 * PEER_TOPK))
            def _(q=q, k=k, t=t, u=u):
                _drain(_route_unit(q[u], k[u], t, True))


def _peer_query_kernel(h_ref, g_ref, wpq_ref, xnt_ref, qh_ref):
    hn = _rms(h_ref[...], g_ref[...])
    xnt_ref[...] = hn.T.astype(BF16)
    q = jnp.dot(hn.astype(BF16), wpq_ref[...], preferred_element_type=F32)
    for hh in range(PEER_HEADS):
        qh_ref[hh] = q[:, hh * PEER_QDIM:(hh + 1) * PEER_QDIM].astype(BF16)


def peer_query(h, g, wpq_bf):
    t, d = h.shape
    tq = QUERY_TQ
    return pl.pallas_call(
        _peer_query_kernel,
        grid=(t // tq,),
        in_specs=[pl.BlockSpec((tq, d), lambda i: (i, 0)),
                  pl.BlockSpec((1, d), lambda i: (0, 0)),
                  pl.BlockSpec(wpq_bf.shape, lambda i: (0, 0), pipeline_mode=pl.Buffered(1))],
        out_specs=[pl.BlockSpec((d, tq), lambda i: (0, i)),
                   pl.BlockSpec((PEER_HEADS, tq, PEER_QDIM), lambda i: (0, i, 0))],
        out_shape=[jax.ShapeDtypeStruct((d, t), BF16), jax.ShapeDtypeStruct((PEER_HEADS, t, PEER_QDIM), BF16)],
        compiler_params=_cparams(1, 48),
        name="peer_query",
    )(h, g.reshape(1, d), wpq_bf)


def _peer_route_kernel(qh_ref, sk_ref, cnt_ref, p1_ref, rk2_ref, e2_ref, *, n_route):
    _RouteUnits(qh_ref, sk_ref, (cnt_ref, p1_ref, rk2_ref, e2_ref), n_route).finish()


def peer_route(qh, sk_bf, n_route=2):
    t = qh.shape[1]
    hps = PEER_HEADS // n_route
    tab = pl.BlockSpec((n_route, PEER_NKEYS, ROUTE_TQ), lambda i, j: (j, 0, i))
    return pl.pallas_call(
        functools.partial(_peer_route_kernel, n_route=n_route),
        grid=(t // ROUTE_TQ, hps),
        in_specs=[pl.BlockSpec((n_route, ROUTE_TQ, PEER_QDIM), lambda i, j: (j, i, 0)),
                  pl.BlockSpec((n_route, 2, PEER_NKEYS, PEER_QDIM // 2), lambda i, j: (j, 0, 0, 0))],
        out_specs=[tab] * 4,
        out_shape=[jax.ShapeDtypeStruct((PEER_HEADS, PEER_NKEYS, t), dt) for dt in (F32, F32, BF16, BF16)],
        compiler_params=_cparams(2, 32),
        name="peer_route",
    )(qh, sk_bf)


def _peer_dense_kernel(xnt_ref, u_ref, vt_ref, cnt_ref, p1_ref, rk2_ref, e2_ref, h_ref, fg_ref, y_ref, acc_ref,
                       *, te, n_e):
    j = pl.program_id(1)

    @pl.when(j == 0)
    def _():
        acc_ref[...] = jnp.zeros(acc_ref.shape, F32)

    tq = xnt_ref.shape[1]
    groups = te // PEER_NKEYS
    gpc = PEER_ROW_CHUNK // PEER_NKEYS
    parts = []
    for a in range(groups):
        if a % gpc == 0:
            rows = slice(a * PEER_NKEYS, (a + gpc) * PEER_NKEYS)
            act = jax.nn.gelu(jnp.dot(u_ref[rows, :], xnt_ref[...], preferred_element_type=F32).astype(BF16))
        i1 = j * groups + a
        gate = jnp.zeros((PEER_NKEYS, tq), BF16)
        for hh in range(PEER_HEADS):
            rep = lambda ref: jnp.concatenate(
                [jnp.broadcast_to(ref[hh, pl.ds(i1, 1), :], (16, tq)).astype(BF16)] * (PEER_NKEYS // 16), axis=0)
            gate = gate + jnp.where(rk2_ref[hh] < rep(cnt_ref), rep(p1_ref) * e2_ref[hh], jnp.zeros((), BF16))
        lo = (a % gpc) * PEER_NKEYS
        parts.append(act[lo:lo + PEER_NKEYS] * gate)
    hm_t = parts[0] if groups == 1 else jnp.concatenate(parts, axis=0)
    acc_ref[...] += jnp.dot(vt_ref[...], hm_t, preferred_element_type=F32)

    @pl.when(j == n_e - 1)
    def _():
        y_ref[...] = _rms(h_ref[...] + acc_ref[...].T, fg_ref[...])


def peer_dense(xnt, u_bf, vt_bf, tabs, h, final_g, tq, te):
    t, d = h.shape
    n_e = u_bf.shape[0] // te
    once = dict(pipeline_mode=pl.Buffered(1))
    tab = pl.BlockSpec((PEER_HEADS, PEER_NKEYS, tq), lambda i, j: (0, 0, i), **once)
    return pl.pallas_call(
        functools.partial(_peer_dense_kernel, te=te, n_e=n_e),
        grid=(t // tq, n_e),
        in_specs=[pl.BlockSpec((d, tq), lambda i, j: (0, i), **once),
                  pl.BlockSpec((te, d), lambda i, j: (j, 0)),
                  pl.BlockSpec((d, te), lambda i, j: (0, j)),
                  tab, tab, tab, tab,
                  pl.BlockSpec((tq, d), lambda i, j: (i, 0), **once),
                  pl.BlockSpec((1, d), lambda i, j: (0, 0))],
        out_specs=pl.BlockSpec((tq, d), lambda i, j: (i, 0)),
        out_shape=jax.ShapeDtypeStruct((t, d), F32),
        scratch_shapes=[pltpu.VMEM((d, tq), F32)],
        compiler_params=_cparams(2, 56),
        name="peer_dense",
    )(xnt, u_bf, vt_bf, *tabs, h, final_g.reshape(1, d))


def _cast_transpose_kernel(x_ref, o_ref):
    o_ref[...] = x_ref[...].T.astype(o_ref.dtype)


def cast_transpose(w, rows):
    n, d = w.shape
    return pl.pallas_call(
        _cast_transpose_kernel,
        grid=(n // rows,),
        in_specs=[pl.BlockSpec((rows, d), lambda i: (i, 0))],
        out_specs=pl.BlockSpec((d, rows), lambda i: (0, i)),
        out_shape=jax.ShapeDtypeStruct((d, n), BF16),
        compiler_params=_cparams(1, 48),
        name="cast_transpose",
    )(w)


def kernel(x_prompt, x_sample, mem_prompt, cache_k, cache_v, cache_conv, cache_mem_k, cache_mem_v, page_table,
           norm1_g, w_in, rel_bias, conv_w, conv_b, conv_ln_g, conv_ln_b, w_out, norm2_g, mem_norm_g, w_cq,
           w_mk, w_mv, w_co, norm3_g, w_pq, peer_sub_keys, peer_u, peer_v, final_g):
    depth = w_in.shape[0]
    assert depth == 1, "single-layer step"
    n_p, seq, d = x_prompt.shape
    n_s, t_s, _ = x_sample.shape
    l = 0
    bf = lambda a: a.astype(BF16)

    w_in_bf, w_out_bf = bf(w_in[l]), bf(w_out[l])
    w_cq_bf, w_co_bf, w_pq_bf = bf(w_cq[l]), bf(w_co[l]), bf(w_pq[l])
    w_mem_bf = bf(jnp.concatenate([w_mk[l], w_mv[l]], axis=1))
    sk_bf, u_bf, vt_bf = bf(peer_sub_keys[l]), bf(peer_u[l]), cast_transpose(peer_v[l], PEER_EXPERT_TILE)

    n_pool, n_pages = cache_k.shape[1], page_table.shape[1]
    bias_p, t_far, t_last, t_own = bias_tiles(rel_bias, n_pages * PAGE_SIZE)
    c31_h = rel_bias[NUM_BUCKETS - 1]

    def cross_and_query(x, attn, conv, mk, mv, n_grp, rows, tiles_per_seq):
        h = out_proj(x, attn, conv, w_out_bf, tm=OUT_PROJ_ROWS)
        h = cross_block(h, norm2_g[l], w_cq_bf, w_co_bf, mk, mv, n_grp, rows, tiles_per_seq)
        return (h,) + tuple(peer_query(h, norm3_g[l], w_pq_bf))

    experts = lambda xnt, tabs, h: peer_dense(xnt, u_bf, vt_bf, tabs, h, final_g,
                                              tq=PEER_TOKEN_TILE, te=PEER_EXPERT_TILE)

    xp = x_prompt.reshape(n_p * seq, d)
    mk_p, mv_p = rms_matmul(mem_prompt.reshape(n_p * N_MEM, d), mem_norm_g[l], w_mem_bf, 2, tm=MEM_PROJ_ROWS)
    q_p, k_p, v_p, ga_p, gg_p = rms_matmul(xp, norm1_g[l], w_in_bf, 5, tm=IN_PROJ_ROWS)
    attn_p = moba_prompt(q_p, k_p, v_p, bias_p, c31_h, n_p, seq)
    zero_buf = jnp.zeros((n_p, CONV_K - 1, CONV_CH), F32)
    conv_p, buf_p = conformer_conv(ga_p, gg_p, zero_buf, conv_w[l], conv_b[l], conv_ln_g[l], conv_ln_b[l],
                                   n_p, seq, tt=CONV_ROWS, sb=1)
    h_p, xnt_p, qh_p = cross_and_query(xp, attn_p, conv_p, mk_p.reshape(n_p, N_MEM, X_WIDTH),
                                       mv_p.reshape(n_p, N_MEM, X_WIDTH), 1, CROSS_ROWS, seq // CROSS_ROWS)

    xs = x_sample.reshape(n_s * t_s, d)
    q_s, k_s, v_s, ga_s, gg_s = rms_matmul(xs, norm1_g[l], w_in_bf, 5, tm=IN_PROJ_ROWS)
    tok3 = lambda a: a.reshape(n_s, t_s, ATTN_WIDTH)
    units_p = PEER_HEADS * (n_p * seq // ROUTE_TQ)
    fuse = units_p % n_s == 0 and PEER_HEADS % (units_p // n_s) == 0
    attn_s, *tabs_p = moba_sample(tok3(q_s), tok3(k_s), tok3(v_s),
                                  cache_k[l].reshape(n_pool, PAGE_SIZE * N_HEADS, HEAD_DIM),
                                  cache_v[l].reshape(n_pool, PAGE_SIZE * N_HEADS, HEAD_DIM),
                                  page_table, t_far, t_last, t_own, route=(qh_p, sk_bf) if fuse else None)
    if not fuse:
        tabs_p = peer_route(qh_p, sk_bf)
    y_p = experts(xnt_p, tabs_p, h_p)
    conv_s, buf_s = conformer_conv(ga_s, gg_s, cache_conv[l], conv_w[l], conv_b[l], conv_ln_g[l], conv_ln_b[l],
                                   n_s, t_s, tt=t_s, sb=SHORT_SEQS_PER_STEP)
    h_s, xnt_s, qh_s = cross_and_query(xs, attn_s.reshape(n_s * t_s, ATTN_WIDTH), conv_s,
                                       cache_mem_k[l].reshape(n_s, N_MEM * X_HEADS, HEAD_DIM),
                                       cache_mem_v[l].reshape(n_s, N_MEM * X_HEADS, HEAD_DIM),
                                       SHORT_SEQS_PER_STEP, t_s, 1)
    y_s = experts(xnt_s, peer_route(qh_s, sk_bf), h_s)

    n_pg = seq // PAGE_SIZE
    return (y_p.reshape(n_p, seq, d), y_s.reshape(n_s, t_s, d),
            k_p.reshape(1, n_p, n_pg, PAGE_SIZE, N_HEADS, HEAD_DIM),
            v_p.reshape(1, n_p, n_pg, PAGE_SIZE, N_HEADS, HEAD_DIM),
            buf_p[None],
            mk_p.reshape(1, n_p, N_MEM, X_HEADS, HEAD_DIM), mv_p.reshape(1, n_p, N_MEM, X_HEADS, HEAD_DIM),
            k_s.reshape(1, n_s, t_s, N_HEADS, HEAD_DIM), v_s.reshape(1, n_s, t_s, N_HEADS, HEAD_DIM),
            buf_s[None])
```
